```python
import math, functools
import jax, jax.numpy as jnp
from jax import lax
import numpy as np

D_MODEL = 2048
BATCH = 4
SEQ = 2048
DEPTH = 2
DEC_BATCH = 128
DEC_SEQ = 4
PAST_LEN = 8192
PAGE_SIZE = 128

N_A_LAYERS = (DEPTH + 1) // 2
N_C_LAYERS = DEPTH // 2
CONV_W = 4
GDN_HEADS = 8
GDN_DK = 128
GDN_DV = 128
GDN_QK = GDN_HEADS * GDN_DK
GDN_WIDTH = GDN_HEADS * GDN_DV
GDN_QKV = 2 * GDN_QK + GDN_WIDTH
GDN_CHUNK = 64
LRU_WIDTH = D_MODEL // 2
LRU_BLOCKS = 8
LRU_BW = LRU_WIDTH // LRU_BLOCKS
LRU_C = 8.0
IN_A = GDN_QKV + GDN_WIDTH + 2 * GDN_HEADS + 2 * LRU_WIDTH
MIX_A = GDN_WIDTH + LRU_WIDTH
MLA_HEADS = 16
Q_LORA = 512
KV_LORA = 512
QK_NOPE = 128
QK_ROPE = 64
V_HEAD = 128
IN_C = Q_LORA + KV_LORA + QK_ROPE
MLA_SCALE = (QK_NOPE + QK_ROPE) ** -0.5
ROPE_BASE = 10000.0
Q_BLOCK = 128
D_FF = 4 * D_MODEL
PLE_DIM = 256
EPS = 1e-6

kernel_name = 'hybrid_gdn_rglru_mla_decoder_step'


def _rmsnorm(x, g):
    xf = x.astype(jnp.float32)
    xf = xf * lax.rsqrt(jnp.mean(xf * xf, axis=-1, keepdims=True) + EPS)
    return (xf * g.astype(jnp.float32)).astype(x.dtype)


def _l2norm(x):
    xf = x.astype(jnp.float32)
    return xf * lax.rsqrt(jnp.sum(xf * xf, axis=-1, keepdims=True) + EPS)


def _causal_conv(x, buf, w):
    xx = jnp.concatenate([buf.astype(x.dtype), x], axis=1)
    y = lax.conv_general_dilated(xx, w[:, None, :].astype(x.dtype), (1,), 'VALID',
                                 dimension_numbers=('NWC', 'WIO', 'NWC'),
                                 feature_group_count=x.shape[-1])
    return y, xx[:, xx.shape[1] - (CONV_W - 1):]


def _rope(x, positions):
    half = QK_ROPE // 2
    inv = ROPE_BASE ** (-jnp.arange(half, dtype=jnp.float32) / half)
    ang = positions.astype(jnp.float32)[:, None] * inv[None, :]
    cos = jnp.cos(ang)[None, :, None, :]
    sin = jnp.sin(ang)[None, :, None, :]
    xf = x.astype(jnp.float32)
    x1, x2 = xf[..., :half], xf[..., half:]
    return jnp.concatenate([x1 * cos - x2 * sin, x1 * sin + x2 * cos], axis=-1).astype(x.dtype)


def _gated_delta(q, k, v, beta, g, s0):
    f32 = jnp.float32
    B, L, H, DK = q.shape
    DV = v.shape[-1]
    C = min(GDN_CHUNK, L)
    N = -(-L // C)
    pad = N * C - L
    q, k, v, beta, g = (t.astype(f32) for t in (q, k, v, beta, g))
    if pad:
        pw = lambda t: jnp.pad(t, [(0, 0), (0, pad)] + [(0, 0)] * (t.ndim - 2))
        q, k, v, beta, g = (pw(t) for t in (q, k, v, beta, g))

    def chunks(t):
        t = t.reshape((B, N, C, H) + t.shape[3:])
        return t.transpose((1, 0, 3, 2) + tuple(range(4, t.ndim)))

    qc, kc, vc, bc = chunks(q), chunks(k), chunks(v), chunks(beta)
    gc = jnp.cumsum(chunks(g), axis=-1)
    tri = jnp.tril(jnp.ones((C, C), bool))
    strict = jnp.tril(jnp.ones((C, C), bool), -1)
    decay = jnp.exp(jnp.where(tri, gc[..., :, None] - gc[..., None, :], -jnp.inf))
    kb = kc * bc[..., None]
    a = jnp.where(strict, jnp.einsum('nbhcd,nbhsd->nbhcs', kb, kc) * decay, 0.0)
    eye = jnp.eye(C, dtype=f32)
    rhs = jnp.concatenate([vc * bc[..., None], kb * jnp.exp(gc)[..., None]], axis=-1)
    sol = lax.linalg.triangular_solve(a + eye, rhs, left_side=True, lower=True)
    u, w = sol[..., :DV], sol[..., DV:]
    qk = jnp.where(tri, jnp.einsum('nbhcd,nbhsd->nbhcs', qc, kc) * decay, 0.0)

    def step(s, xs):
        q_i, k_i, u_i, w_i, qk_i, g_i = xs
        v_new = u_i - jnp.einsum('bhcd,bhde->bhce', w_i, s)
        o_i = (jnp.einsum('bhcd,bhde->bhce', q_i * jnp.exp(g_i)[..., None], s)
               + jnp.einsum('bhcs,bhse->bhce', qk_i, v_new))
        g_last = g_i[..., -1:]
        s = (s * jnp.exp(g_last)[..., None]
             + jnp.einsum('bhcd,bhce->bhde', k_i * jnp.exp(g_last - g_i)[..., None], v_new))
        return s, o_i

    s, o = lax.scan(step, s0.astype(f32), (qc, kc, u, w, qk, gc))
    o = o.transpose(1, 0, 3, 2, 4).reshape(B, N * C, H, DV)[:, :L]
    return o, s


def _lin_comb(left, right):
    a1, b1 = left
    a2, b2 = right
    return a1 * a2, a2 * b1 + b2


def _mixer_ab(hn, s0, qkv_buf, h0, lru_buf, w_in, conv_qkv_w, a_log, dt_bias, gdn_norm_g,
              conv_lru_w, conv_lru_b, lru_wa, lru_ba, lru_wi, lru_bi, lru_lambda, w_out):
    f32 = jnp.float32
    B, L, _ = hn.shape
    cuts = [GDN_QKV, GDN_QKV + GDN_WIDTH, GDN_QKV + GDN_WIDTH + GDN_HEADS,
            GDN_QKV + GDN_WIDTH + 2 * GDN_HEADS, GDN_QKV + GDN_WIDTH + 2 * GDN_HEADS + LRU_WIDTH]
    qkv, z, b_raw, a_raw, lx, ly = jnp.split(hn @ w_in, cuts, axis=-1)
    qkv, qkv_buf_new = _causal_conv(qkv, qkv_buf, conv_qkv_w)
    qkv = jax.nn.silu(qkv)
    q = _l2norm(qkv[..., :GDN_QK].reshape(B, L, GDN_HEADS, GDN_DK)) * (GDN_DK ** -0.5)
    k = _l2norm(qkv[..., GDN_QK:2 * GDN_QK].reshape(B, L, GDN_HEADS, GDN_DK))
    v = qkv[..., 2 * GDN_QK:].reshape(B, L, GDN_HEADS, GDN_DV)
    beta = jax.nn.sigmoid(b_raw.astype(f32))
    g = -jnp.exp(a_log.astype(f32)) * jax.nn.softplus(a_raw.astype(f32) + dt_bias.astype(f32))
    o, s_new = _gated_delta(q, k, v, beta, g, s0)
    o = _rmsnorm(o, gdn_norm_g) * jax.nn.silu(z.astype(f32).reshape(B, L, GDN_HEADS, GDN_DV))
    xb, lru_buf_new = _causal_conv(lx, lru_buf, conv_lru_w)
    xb = (xb + conv_lru_b).astype(f32)
    xblk = xb.reshape(B, L, LRU_BLOCKS, LRU_BW)
    r = jax.nn.sigmoid(jnp.einsum('blki,kij->blkj', xblk, lru_wa.astype(f32)) + lru_ba).reshape(B, L, LRU_WIDTH)
    ig = jax.nn.sigmoid(jnp.einsum('blki,kij->blkj', xblk, lru_wi.astype(f32)) + lru_bi).reshape(B, L, LRU_WIDTH)
    log_a = -LRU_C * r * jax.nn.softplus(-lru_lambda.astype(f32))
    a = jnp.exp(log_a)
    bt = jnp.sqrt(-jnp.expm1(2.0 * log_a)) * (ig * xb)
    bt = bt.at[:, 0].add(a[:, 0] * h0.astype(f32))
    _, hs = lax.associative_scan(_lin_comb, (a, bt), axis=1)
    ob = hs * jax.nn.gelu(ly.astype(f32))
    mix = jnp.concatenate([o.reshape(B, L, GDN_WIDTH), ob], axis=-1).astype(hn.dtype) @ w_out
    return mix, s_new.astype(s0.dtype), qkv_buf_new, hs[:, -1].astype(h0.dtype), lru_buf_new


def _attend_prompt(q_nope, q_pe, ckv, kpe, w_uk, w_uv):
    f32 = jnp.float32
    B, S, H, _ = q_nope.shape
    k_nope = jnp.einsum('bsc,chn->bshn', ckv, w_uk)
    v = jnp.einsum('bsc,chv->bshv', ckv, w_uv)
    nb = S // Q_BLOCK
    qn = q_nope.reshape(B, nb, Q_BLOCK, H, QK_NOPE).swapaxes(0, 1)
    qp = q_pe.reshape(B, nb, Q_BLOCK, H, QK_ROPE).swapaxes(0, 1)
    key_pos = jnp.arange(S)

    def block(args):
        qn_i, qp_i, start = args
        s = (jnp.einsum('bqhn,bkhn->bhqk', qn_i.astype(f32), k_nope.astype(f32))
             + jnp.einsum('bqhr,bkr->bhqk', qp_i.astype(f32), kpe.astype(f32))) * MLA_SCALE
        mask = key_pos[None, :] <= (start + jnp.arange(Q_BLOCK))[:, None]
        pr = jax.nn.softmax(jnp.where(mask, s, -jnp.inf), axis=-1)
        return jnp.einsum('bhqk,bkhv->bqhv', pr.astype(v.dtype), v)

    o = lax.map(block, (qn, qp, jnp.arange(nb) * Q_BLOCK))
    return o.swapaxes(0, 1).reshape(B, S, H, V_HEAD)


def _attend_paged(q_nope, q_pe, ckv_new, kpe_new, w_uk, w_uv, pool_ckv, pool_kpe, layer, page_table):
    f32 = jnp.float32
    B, T, H, _ = q_nope.shape
    q_lat = jnp.einsum('bthn,chn->bhtc', q_nope.astype(f32), w_uk.astype(f32)) * MLA_SCALE
    q_rot = jnp.swapaxes(q_pe.astype(f32), 1, 2) * MLA_SCALE

    def online(carry, ck, kp, mask):
        m, l, acc = carry
        s = jnp.einsum('bhtc,bkc->bhtk', q_lat, ck) + jnp.einsum('bhtr,bkr->bhtk', q_rot, kp)
        if mask is not None:
            s = jnp.where(mask, s, -jnp.inf)
        m_new = jnp.maximum(m, jnp.max(s, axis=-1))
        pr = jnp.exp(s - m_new[..., None])
        corr = jnp.exp(m - m_new)
        return (m_new, l * corr + jnp.sum(pr, axis=-1),
                acc * corr[..., None] + jnp.einsum('bhtk,bkc->bhtc', pr, ck))

    def page_step(carry, pid):
        ck = pool_ckv[layer, pid].astype(f32)
        kp = pool_kpe[layer, pid].astype(f32)
        return online(carry, ck, kp, None), None

    init = (jnp.full((B, H, T), -jnp.inf, f32), jnp.zeros((B, H, T), f32), jnp.zeros((B, H, T, KV_LORA), f32))
    carry, _ = lax.scan(page_step, init, page_table.T)
    causal = jnp.tril(jnp.ones((T, T), bool))
    _, l, acc = online(carry, ckv_new.astype(f32), kpe_new.astype(f32), causal)
    o = jnp.einsum('bhtc,chv->bthv', acc / l[..., None], w_uv.astype(f32))
    return o.astype(q_nope.dtype)


def _mixer_c(hn, positions, attend, w_in, q_norm_g, kv_norm_g, w_uq, w_ukv, w_o):
    B, L, _ = hn.shape
    cq, ckv, kpe = jnp.split(hn @ w_in, [Q_LORA, Q_LORA + KV_LORA], axis=-1)
    q = (_rmsnorm(cq, q_norm_g) @ w_uq).reshape(B, L, MLA_HEADS, QK_NOPE + QK_ROPE)
    q_nope, q_pe = q[..., :QK_NOPE], _rope(q[..., QK_NOPE:], positions)
    ckv = _rmsnorm(ckv, kv_norm_g)
    kpe = _rope(kpe[:, :, None, :], positions)[:, :, 0]
    w_r = w_ukv.reshape(KV_LORA, MLA_HEADS, QK_NOPE + V_HEAD)
    o = attend(q_nope, q_pe, ckv, kpe, w_r[..., :QK_NOPE], w_r[..., QK_NOPE:])
    return o.reshape(B, L, MLA_HEADS * V_HEAD) @ w_o, ckv, kpe


def _trunk(x, p, positions, attend, gdn_s, gdn_c, lru_s, lru_c, wa, wc, wl):
    norm_mix_g, norm_ffn_g, w_up, w_down, w_ple, w_ple_gate, norm_final_g = wl
    h = x
    o_s, o_c, o_h, o_lc, o_ckv, o_kpe = [], [], [], [], [], []
    for i in range(DEPTH):
        j = i // 2
        hn = _rmsnorm(h, norm_mix_g[i])
        if i % 2 == 0:
            mix, s_new, c_new, h_new, lc_new = _mixer_ab(hn, gdn_s[j], gdn_c[j], lru_s[j], lru_c[j],
                                                         *[w[j] for w in wa])
            o_s.append(s_new); o_c.append(c_new); o_h.append(h_new); o_lc.append(lc_new)
        else:
            mix, ckv, kpe = _mixer_c(hn, positions, functools.partial(attend, j), *[w[j] for w in wc])
            o_ckv.append(ckv); o_kpe.append(kpe)
        h = h + mix
        h = h + jnp.square(jax.nn.relu(_rmsnorm(h, norm_ffn_g[i]) @ w_up[i])) @ w_down[i]
        h = h + (p[i] @ w_ple[i]) * jax.nn.sigmoid(h @ w_ple_gate[i])
    y = _rmsnorm(h, norm_final_g)
    return (y, jnp.stack(o_s), jnp.stack(o_c), jnp.stack(o_h), jnp.stack(o_lc),
            jnp.stack(o_ckv), jnp.stack(o_kpe))


def setup_inputs(seed: int = 0) -> dict:
    key = jax.random.key(seed)
    ks = iter(jax.random.split(key, 64))
    f32 = jnp.float32
    nrm = lambda shape, scale=1.0: jax.random.normal(next(ks), shape, f32) * scale
    gain = lambda shape: 1.0 + 0.01 * jax.random.normal(next(ks), shape, f32)
    NA, NC = N_A_LAYERS, N_C_LAYERS
    n_pages = PAST_LEN // PAGE_SIZE
    n_used = DEC_BATCH * n_pages
    n_pool = n_used + max(1, n_used // 4)
    page_table = jax.random.permutation(next(ks), n_pool)[:n_used].reshape(DEC_BATCH, n_pages).astype(jnp.int32)
    dt = jnp.exp(jax.random.uniform(next(ks), (NA, GDN_HEADS), f32, math.log(1e-3), math.log(1e-1)))
    gdn_dt_bias = dt + jnp.log(-jnp.expm1(-dt))
    gdn_a_log = jnp.log(jax.random.uniform(next(ks), (NA, GDN_HEADS), f32, 1.0, 16.0))
    u = jax.random.uniform(next(ks), (NA, LRU_WIDTH), f32, 0.9, 0.999)
    lru_lambda = jnp.log(u) - jnp.log1p(-u)
    return {
        'x_prompt': nrm((BATCH, SEQ, D_MODEL)),
        'x_sample': nrm((DEC_BATCH, DEC_SEQ, D_MODEL)),
        'p_prompt': nrm((DEPTH, BATCH, SEQ, PLE_DIM)),
        'p_sample': nrm((DEPTH, DEC_BATCH, DEC_SEQ, PLE_DIM)),
        'state_gdn': nrm((NA, DEC_BATCH, GDN_HEADS, GDN_DK, GDN_DV), 0.05),
        'state_gdn_conv': nrm((NA, DEC_BATCH, CONV_W - 1, GDN_QKV)),
        'state_lru': nrm((NA, DEC_BATCH, LRU_WIDTH), 0.5),
        'state_lru_conv': nrm((NA, DEC_BATCH, CONV_W - 1, LRU_WIDTH)),
        'cache_ckv': nrm((NC, n_pool, PAGE_SIZE, KV_LORA)),
        'cache_kpe': nrm((NC, n_pool, PAGE_SIZE, QK_ROPE)),
        'page_table': page_table,
        'w_in_a': nrm((NA, D_MODEL, IN_A), D_MODEL ** -0.5),
        'conv_qkv_w': nrm((NA, CONV_W, GDN_QKV), CONV_W ** -0.5),
        'gdn_a_log': gdn_a_log,
        'gdn_dt_bias': gdn_dt_bias,
        'gdn_norm_g': gain((NA, GDN_DV)),
        'conv_lru_w': nrm((NA, CONV_W, LRU_WIDTH), CONV_W ** -0.5),
        'conv_lru_b': nrm((NA, LRU_WIDTH), 0.01),
        'lru_wa': nrm((NA, LRU_BLOCKS, LRU_BW, LRU_BW), LRU_BW ** -0.5),
        'lru_ba': nrm((NA, LRU_BLOCKS, LRU_BW), 0.01),
        'lru_wi': nrm((NA, LRU_BLOCKS, LRU_BW, LRU_BW), LRU_BW ** -0.5),
        'lru_bi': nrm((NA, LRU_BLOCKS, LRU_BW), 0.01),
        'lru_lambda': lru_lambda,
        'w_out_a': nrm((NA, MIX_A, D_MODEL), MIX_A ** -0.5),
        'w_in_c': nrm((NC, D_MODEL, IN_C), D_MODEL ** -0.5),
        'q_norm_g': gain((NC, Q_LORA)),
        'kv_norm_g': gain((NC, KV_LORA)),
        'w_uq': nrm((NC, Q_LORA, MLA_HEADS * (QK_NOPE + QK_ROPE)), Q_LORA ** -0.5),
        'w_ukv': nrm((NC, KV_LORA, MLA_HEADS * (QK_NOPE + V_HEAD)), KV_LORA ** -0.5),
        'w_o_c': nrm((NC, MLA_HEADS * V_HEAD, D_MODEL), (MLA_HEADS * V_HEAD) ** -0.5),
        'norm_mix_g': gain((DEPTH, D_MODEL)),
        'norm_ffn_g': gain((DEPTH, D_MODEL)),
        'w_up': nrm((DEPTH, D_MODEL, D_FF), D_MODEL ** -0.5),
        'w_down': nrm((DEPTH, D_FF, D_MODEL), D_FF ** -0.5),
        'w_ple': nrm((DEPTH, PLE_DIM, D_MODEL), PLE_DIM ** -0.5),
        'w_ple_gate': nrm((DEPTH, D_MODEL, D_MODEL), D_MODEL ** -0.5),
        'norm_final_g': gain((D_MODEL,)),
    }


def reference(x_prompt, x_sample, p_prompt, p_sample, state_gdn, state_gdn_conv, state_lru, state_lru_conv,
              cache_ckv, cache_kpe, page_table,
              w_in_a, conv_qkv_w, gdn_a_log, gdn_dt_bias, gdn_norm_g, conv_lru_w, conv_lru_b,
              lru_wa, lru_ba, lru_wi, lru_bi, lru_lambda, w_out_a,
              w_in_c, q_norm_g, kv_norm_g, w_uq, w_ukv, w_o_c,
              norm_mix_g, norm_ffn_g, w_up, w_down, w_ple, w_ple_gate, norm_final_g):
    wa = (w_in_a, conv_qkv_w, gdn_a_log, gdn_dt_bias, gdn_norm_g, conv_lru_w, conv_lru_b,
          lru_wa, lru_ba, lru_wi, lru_bi, lru_lambda, w_out_a)
    wc = (w_in_c, q_norm_g, kv_norm_g, w_uq, w_ukv, w_o_c)
    wl = (norm_mix_g, norm_ffn_g, w_up, w_down, w_ple, w_ple_gate, norm_final_g)
    b_p, seq = x_prompt.shape[0], x_prompt.shape[1]
    t_s = x_sample.shape[1]
    past = page_table.shape[1] * PAGE_SIZE

    def zeros_like_batch(ref, b):
        return jnp.zeros((ref.shape[0], b) + ref.shape[2:], ref.dtype)

    def attend_prompt(layer, q_nope, q_pe, ckv, kpe, w_uk, w_uv):
        return _attend_prompt(q_nope, q_pe, ckv, kpe, w_uk, w_uv)

    def attend_sample(layer, q_nope, q_pe, ckv, kpe, w_uk, w_uv):
        return _attend_paged(q_nope, q_pe, ckv, kpe, w_uk, w_uv, cache_ckv, cache_kpe, layer, page_table)

    y_prompt, gdn_p, gdn_conv_p, lru_p, lru_conv_p, ckv_p, kpe_p = _trunk(
        x_prompt, p_prompt, jnp.arange(seq), attend_prompt,
        zeros_like_batch(state_gdn, b_p), zeros_like_batch(state_gdn_conv, b_p),
        zeros_like_batch(state_lru, b_p), zeros_like_batch(state_lru_conv, b_p), wa, wc, wl)
    y_sample, gdn_s, gdn_conv_s, lru_s, lru_conv_s, ckv_s, kpe_s = _trunk(
        x_sample, p_sample, past + jnp.arange(t_s), attend_sample,
        state_gdn, state_gdn_conv, state_lru, state_lru_conv, wa, wc, wl)
    return (y_prompt, y_sample, gdn_p, gdn_s, gdn_conv_p, gdn_conv_s, lru_p, lru_s,
            lru_conv_p, lru_conv_s, ckv_p, ckv_s, kpe_p, kpe_s)
```

```python
import functools
import math

import jax
import jax.numpy as jnp
from jax import lax
from jax.experimental import pallas as pl
from jax.experimental.pallas import tpu as pltpu

F32 = jnp.float32
BF16 = jnp.bfloat16

D_MODEL = 2048
CONV_W = 4
GDN_HEADS = 8
GDN_DK = 128
GDN_DV = 128
GDN_QK = GDN_HEADS * GDN_DK
GDN_WIDTH = GDN_HEADS * GDN_DV
GDN_QKV = 2 * GDN_QK + GDN_WIDTH
GDN_CHUNK = 64
LRU_WIDTH = D_MODEL // 2
LRU_BLOCKS = 8
LRU_BW = LRU_WIDTH // LRU_BLOCKS
LRU_C = 8.0
MLA_HEADS = 16
Q_LORA = 512
KV_LORA = 512
QK_NOPE = 128
QK_ROPE = 64
V_HEAD = 128
MLA_SCALE = (QK_NOPE + QK_ROPE) ** -0.5
ROPE_BASE = 10000.0
PAGE_SIZE = 128
EPS = 1e-6

LANES = 128
SUBLANES = 8
VMEM_LIMIT_BYTES = 56 * 1024 * 1024

A_COL_Z = GDN_QKV
A_COL_LX = A_COL_Z + GDN_WIDTH
A_COL_LY = A_COL_LX + LRU_WIDTH
A_COL_BA = A_COL_LY + LRU_WIDTH
A_COLS = A_COL_BA + LANES
C_COL_KPE = Q_LORA + KV_LORA
C_COLS = C_COL_KPE + LANES
Q_HEAD_COLS = 2 * LANES
PAGES_PER_STEP = 8

NN = (((1,), (0,)), ((), ()))
NT = (((1,), (1,)), ((), ()))
TN = (((0,), (0,)), ((), ()))


def _params(semantics, est_bytes):
    limit = int(min(max(2 * est_bytes, 32 * 1024 * 1024), VMEM_LIMIT_BYTES))
    return pltpu.CompilerParams(dimension_semantics=semantics, vmem_limit_bytes=limit)


def _nbytes(shape, dtype):
    return math.prod(shape) * jnp.dtype(dtype).itemsize


def _dot(a, b, dims=NN):
    return lax.dot_general(a.astype(BF16), b.astype(BF16), dims, preferred_element_type=F32)


def _dot_hi(a, b, dims=NN):
    return lax.dot_general(a, b, dims, precision=lax.Precision.HIGHEST, preferred_element_type=F32)


def _sigmoid(x):
    return jax.nn.sigmoid(x)


def _softplus(x):
    return jnp.maximum(x, 0.0) + jnp.log1p(jnp.exp(-jnp.abs(x)))


def _rms(x, g):
    return x * lax.rsqrt(jnp.mean(x * x, axis=-1, keepdims=True) + EPS) * g


def _rmsnorm_kernel(x_ref, g_ref, o_ref):
    o_ref[...] = _rms(x_ref[...], g_ref[...]).astype(o_ref.dtype)


def _rmsnorm(x, g, *, tm, out_dtype):
    m, d = x.shape
    est = 2 * (_nbytes((tm, d), F32) + _nbytes((tm, d), out_dtype)) + 2 * _nbytes((tm, d), F32)
    return pl.pallas_call(
        _rmsnorm_kernel,
        grid=(m // tm,),
        in_specs=[pl.BlockSpec((tm, d), lambda i: (i, 0)), pl.BlockSpec((1, d), lambda i: (0, 0))],
        out_specs=pl.BlockSpec((tm, d), lambda i: (i, 0)),
        out_shape=jax.ShapeDtypeStruct((m, d), out_dtype),
        compiler_params=_params(("parallel",), est),
        name="rmsnorm",
    )(x, g.reshape(1, d))


def _mm_kernel(a_ref, w_ref, *rest, epilogue):
    o_ref = rest[-1]
    acc = _dot(a_ref[...], w_ref[...])
    if epilogue == "relu2":
        acc = jnp.square(jnp.maximum(acc, 0.0))
    elif epilogue == "residual":
        acc = acc + rest[0][...]
    o_ref[...] = acc.astype(o_ref.dtype)


def _mm(a, w, *, tm, tn, out_dtype, epilogue=None, residual=None, name):
    m, k = a.shape
    _, n = w.shape
    in_specs = [pl.BlockSpec((tm, k), lambda j, i: (i, 0)), pl.BlockSpec((k, tn), lambda j, i: (0, j))]
    args = [a, w]
    est = 2 * (_nbytes((tm, k), a.dtype) + _nbytes((k, tn), w.dtype) + _nbytes((tm, tn), out_dtype))
    est += 2 * _nbytes((tm, tn), F32)
    if epilogue == "residual":
        in_specs.append(pl.BlockSpec((tm, tn), lambda j, i: (i, j)))
        args.append(residual)
        est += 2 * _nbytes((tm, tn), F32)
    return pl.pallas_call(
        functools.partial(_mm_kernel, epilogue=epilogue),
        grid=(n // tn, m // tm),
        in_specs=in_specs,
        out_specs=pl.BlockSpec((tm, tn), lambda j, i: (i, j)),
        out_shape=jax.ShapeDtypeStruct((m, n), out_dtype),
        compiler_params=_params(("parallel", "parallel"), est),
        name=name,
    )(*args)


def _ple_kernel(h_ref, p_ref, wp_ref, wg_ref, g_ref, h_out_ref, n_out_ref):
    h = h_ref[...]
    gate = _sigmoid(_dot(h, wg_ref[...]))
    h2 = h + _dot(p_ref[...], wp_ref[...]) * gate
    h_out_ref[...] = h2
    n_out_ref[...] = _rms(h2, g_ref[...]).astype(n_out_ref.dtype)


def _ple(h, p, wp, wg, g, *, tm, norm_dtype):
    m, d = h.shape
    pd = p.shape[1]
    est = 2 * (2 * _nbytes((tm, d), F32) + _nbytes((tm, pd), F32) + _nbytes((pd, d), BF16)
               + _nbytes((d, d), BF16) + _nbytes((tm, d), norm_dtype)) + 3 * _nbytes((tm, d), F32)
    return pl.pallas_call(
        _ple_kernel,
        grid=(m // tm,),
        in_specs=[
            pl.BlockSpec((tm, d), lambda i: (i, 0)),
            pl.BlockSpec((tm, pd), lambda i: (i, 0)),
            pl.BlockSpec((pd, d), lambda i: (0, 0)),
            pl.BlockSpec((d, d), lambda i: (0, 0)),
            pl.BlockSpec((1, d), lambda i: (0, 0)),
        ],
        out_specs=[pl.BlockSpec((tm, d), lambda i: (i, 0)), pl.BlockSpec((tm, d), lambda i: (i, 0))],
        out_shape=[jax.ShapeDtypeStruct((m, d), F32), jax.ShapeDtypeStruct((m, d), norm_dtype)],
        compiler_params=_params(("parallel",), est),
        name="ple",
    )(h, p, wp, wg, g.reshape(1, d))


def _gdn_kernel(qkv_ref, ba_ref, z_ref, buf_ref, s0_ref, cw_ref, alog_ref, dtb_ref, ng_ref,
                o_ref, sout_ref, xs_ref, s_ref, *, chunk, bb, l_real):
    c = chunk
    i = pl.program_id(1)

    @pl.when(i == 0)
    def _():
        s_ref[...] = s0_ref[...]
        xs_ref[:, 0:SUBLANES, :] = buf_ref[...]

    @pl.when(i > 0)
    def _():
        xs_ref[:, 0:SUBLANES, :] = xs_ref[:, c:c + SUBLANES, :]

    xs_ref[:, SUBLANES:SUBLANES + c, :] = qkv_ref[...]

    row = lax.broadcasted_iota(jnp.int32, (c, c), 0)
    col = lax.broadcasted_iota(jnp.int32, (c, c), 1)
    lower = row >= col
    strict = row > col
    tril = lower.astype(F32)
    ones = jnp.ones((c, c), F32)
    lane = lax.broadcasted_iota(jnp.int32, (c, LANES), 1)
    rowmask = (lax.broadcasted_iota(jnp.int32, (c, 1), 0) < l_real).astype(F32)
    cw = cw_ref[...]
    n_factors = int(math.log2(c))

    def seq(b):
        halo0 = SUBLANES - (CONV_W - 1)
        y = cw[CONV_W - 1:CONV_W, :] * xs_ref[b, SUBLANES:SUBLANES + c, :]
        for j in range(CONV_W - 1):
            y = y + cw[j:j + 1, :] * xs_ref[b, halo0 + j:halo0 + j + c, :]
        y = y * _sigmoid(y)
        ba = ba_ref[b]
        beta_all = _sigmoid(ba)
        g_all = -jnp.exp(alog_ref[...]) * _softplus(ba + dtb_ref[...])
        g_all = jnp.where(lane >= GDN_HEADS, g_all, 0.0)
        if l_real < c:
            y = y * rowmask
            beta_all = beta_all * rowmask
            g_all = g_all * rowmask
        gc_all = _dot_hi(tril, g_all)
        zt = z_ref[b]
        for h in range(GDN_HEADS):
            qs = slice(h * GDN_DK, (h + 1) * GDN_DK)
            ks = slice(GDN_QK + h * GDN_DK, GDN_QK + (h + 1) * GDN_DK)
            vs = slice(2 * GDN_QK + h * GDN_DV, 2 * GDN_QK + (h + 1) * GDN_DV)
            q = y[:, qs]
            q = q * lax.rsqrt(jnp.sum(q * q, axis=-1, keepdims=True) + EPS) * (GDN_DK ** -0.5)
            k = y[:, ks]
            k = k * lax.rsqrt(jnp.sum(k * k, axis=-1, keepdims=True) + EPS)
            v = y[:, vs]
            beta = beta_all[:, h:h + 1]
            g = g_all[:, GDN_HEADS + h:GDN_HEADS + h + 1]
            gc = gc_all[:, GDN_HEADS + h:GDN_HEADS + h + 1]
            gc_rows = jnp.broadcast_to(gc, (c, c))
            gc_cols = _dot_hi(ones, jnp.where(row <= col, jnp.broadcast_to(g, (c, c)), 0.0))
            decay = jnp.where(lower, jnp.exp(gc_rows - gc_cols), 0.0)
            kb = k * beta
            a = jnp.where(strict, _dot(kb, k, NT) * decay, 0.0)
            x = jnp.concatenate([v * beta, kb * jnp.exp(gc)], axis=-1)
            p = -a
            for f in range(n_factors):
                x = x + _dot_hi(p, x)
                if f + 1 < n_factors:
                    p = _dot_hi(p, p)
            u = x[:, :GDN_DV]
            w = x[:, GDN_DV:]
            qk = jnp.where(lower, _dot(q, k, NT) * decay, 0.0)
            s = s_ref[b, h]
            v_new = u - _dot(w, s)
            o = _dot(q * jnp.exp(gc), s) + _dot(qk, v_new)
            g_last = gc[c - 1:c, :]
            s_ref[b, h] = s * jnp.exp(g_last) + _dot(k * jnp.exp(g_last - gc), v_new, TN)
            zh = zt[:, h * GDN_DV:(h + 1) * GDN_DV]
            on = _rms(o, ng_ref[...]) * (zh * _sigmoid(zh))
            o_ref[b, :, h * GDN_DV:(h + 1) * GDN_DV] = on.astype(o_ref.dtype)

    if bb == 1:
        seq(0)
    else:
        def body(b, carry):
            seq(b)
            return carry
        lax.fori_loop(0, bb, body, 0)

    @pl.when(i == pl.num_programs(1) - 1)
    def _():
        sout_ref[...] = s_ref[...]


def _gdn(ya3, buf8, s0, conv_w, a_log, dt_bias, norm_g, *, n_seq, seq_rows, row0, chunk, bb, l_real):
    nt = seq_rows // chunk
    three_d = ya3.shape[0] != 1

    def rows(b, i):
        return (b, i) if three_d else (0, (row0 + b * seq_rows) // chunk + i)

    def spec(width, col_block):
        return pl.BlockSpec((bb, chunk, width), lambda b, i: (*rows(b, i), col_block))

    lane_pad = jnp.zeros((1, LANES), F32)
    alog = lane_pad.at[0, GDN_HEADS:2 * GDN_HEADS].set(a_log)
    dtb = lane_pad.at[0, GDN_HEADS:2 * GDN_HEADS].set(dt_bias)
    est = (2 * bb * chunk * (GDN_QKV + LANES + GDN_WIDTH) * 4 + 2 * bb * chunk * GDN_WIDTH * 2
           + 5 * bb * GDN_HEADS * GDN_DK * GDN_DV * 4 + 3 * bb * (chunk + SUBLANES) * GDN_QKV * 4)
    return pl.pallas_call(
        functools.partial(_gdn_kernel, chunk=chunk, bb=bb, l_real=l_real),
        grid=(n_seq // bb, nt),
        in_specs=[
            spec(GDN_QKV, 0),
            spec(LANES, A_COL_BA // LANES),
            spec(GDN_WIDTH, A_COL_Z // GDN_WIDTH),
            pl.BlockSpec((bb, SUBLANES, GDN_QKV), lambda b, i: (b, 0, 0)),
            pl.BlockSpec((bb, GDN_HEADS, GDN_DK, GDN_DV), lambda b, i: (b, 0, 0, 0)),
            pl.BlockSpec((CONV_W, GDN_QKV), lambda b, i: (0, 0)),
            pl.BlockSpec((1, LANES), lambda b, i: (0, 0)),
            pl.BlockSpec((1, LANES), lambda b, i: (0, 0)),
            pl.BlockSpec((1, GDN_DV), lambda b, i: (0, 0)),
        ],
        out_specs=[
            pl.BlockSpec((bb, chunk, GDN_WIDTH), lambda b, i: (b, i, 0)),
            pl.BlockSpec((bb, GDN_HEADS, GDN_DK, GDN_DV), lambda b, i: (b, 0, 0, 0)),
        ],
        out_shape=[
            jax.ShapeDtypeStruct((n_seq, seq_rows, GDN_WIDTH), BF16),
            jax.ShapeDtypeStruct((n_seq, GDN_HEADS, GDN_DK, GDN_DV), F32),
        ],
        scratch_shapes=[
            pltpu.VMEM((bb, chunk + SUBLANES, GDN_QKV), F32),
            pltpu.VMEM((bb, GDN_HEADS, GDN_DK, GDN_DV), F32),
        ],
        compiler_params=_params(("parallel", "arbitrary"), est),
        name="gdn",
    )(ya3, ya3, ya3, buf8, s0, conv_w, alog, dtb, norm_g.reshape(1, GDN_DV))


def _lru_gates(xb, wa_ref, ba_ref, wi_ref, bi_ref, lam_ref):
    c = -LRU_C * _softplus(-lam_ref[...])
    a_parts, b_parts = [], []
    for kb in range(LRU_BLOCKS):
        sl = slice(kb * LRU_BW, (kb + 1) * LRU_BW)
        xk = xb[:, sl]
        r = _sigmoid(_dot(xk, wa_ref[kb]) + ba_ref[:, sl])
        ig = _sigmoid(_dot(xk, wi_ref[kb]) + bi_ref[:, sl])
        log_a = c[:, sl] * r
        a_parts.append(jnp.exp(log_a))
        b_parts.append(jnp.sqrt(1.0 - jnp.exp(2.0 * log_a)) * (ig * xk))
    return jnp.concatenate(a_parts, axis=-1), jnp.concatenate(b_parts, axis=-1)


def _lru_prompt_kernel(lx_ref, ly_ref, buf_ref, h0_ref, cw_ref, cb_ref, wa_ref, ba_ref, wi_ref, bi_ref,
                       lam_ref, ob_ref, ht_ref, xs_ref, a_ref, b_ref, hs_ref, h_ref, *, tl):
    i = pl.program_id(1)

    @pl.when(i == 0)
    def _():
        h_ref[...] = h0_ref[0]
        xs_ref[0:SUBLANES, :] = buf_ref[0]

    @pl.when(i > 0)
    def _():
        xs_ref[0:SUBLANES, :] = xs_ref[tl:tl + SUBLANES, :]

    xs_ref[SUBLANES:SUBLANES + tl, :] = lx_ref[0]
    cw = cw_ref[...]
    halo0 = SUBLANES - (CONV_W - 1)
    xb = cw[CONV_W - 1:CONV_W, :] * xs_ref[SUBLANES:SUBLANES + tl, :] + cb_ref[...]
    for j in range(CONV_W - 1):
        xb = xb + cw[j:j + 1, :] * xs_ref[halo0 + j:halo0 + j + tl, :]
    a, b = _lru_gates(xb, wa_ref, ba_ref, wi_ref, bi_ref, lam_ref)
    a_ref[...] = a
    b_ref[...] = b

    def body(t, h):
        h = a_ref[pl.ds(t, 1), :] * h + b_ref[pl.ds(t, 1), :]
        hs_ref[pl.ds(t, 1), :] = h
        return h

    h = lax.fori_loop(0, tl, body, h_ref[...])
    h_ref[...] = h
    ob_ref[0] = (hs_ref[...] * jax.nn.gelu(ly_ref[0])).astype(ob_ref.dtype)

    @pl.when(i == pl.num_programs(1) - 1)
    def _():
        ht_ref[0] = h


def _lru_prompt(ya3, buf8, h0, cw, cb, wa, ba, wi, bi, lam, *, n_seq, seq_rows, tl):
    nt = seq_rows // tl
    w = LRU_WIDTH

    def spec(col_block):
        return pl.BlockSpec((1, tl, w), lambda b, i: (0, b * nt + i, col_block))

    vec = pl.BlockSpec((1, w), lambda b, i: (0, 0))
    blk = pl.BlockSpec((LRU_BLOCKS, LRU_BW, LRU_BW), lambda b, i: (0, 0, 0))
    est = 8 * tl * w * 4 + 4 * (tl + SUBLANES) * w * 4 + 4 * LRU_BLOCKS * LRU_BW * LRU_BW * 2
    return pl.pallas_call(
        functools.partial(_lru_prompt_kernel, tl=tl),
        grid=(n_seq, nt),
        in_specs=[
            spec(A_COL_LX // w), spec(A_COL_LY // w),
            pl.BlockSpec((1, SUBLANES, w), lambda b, i: (b, 0, 0)),
            pl.BlockSpec((1, 1, w), lambda b, i: (b, 0, 0)),
            pl.BlockSpec((CONV_W, w), lambda b, i: (0, 0)), vec, blk, vec, blk, vec, vec,
        ],
        out_specs=[
            pl.BlockSpec((1, tl, w), lambda b, i: (b, i, 0)),
            pl.BlockSpec((1, 1, w), lambda b, i: (b, 0, 0)),
        ],
        out_shape=[
            jax.ShapeDtypeStruct((n_seq, seq_rows, w), BF16),
            jax.ShapeDtypeStruct((n_seq, 1, w), F32),
        ],
        scratch_shapes=[
            pltpu.VMEM((tl + SUBLANES, w), F32), pltpu.VMEM((tl, w), F32), pltpu.VMEM((tl, w), F32),
            pltpu.VMEM((tl, w), F32), pltpu.VMEM((1, w), F32),
        ],
        compiler_params=_params(("parallel", "arbitrary"), est),
        name="lru_prompt",
    )(ya3, ya3, buf8, h0, cw, cb.reshape(1, w), wa, ba.reshape(1, w), wi, bi.reshape(1, w),
      lam.reshape(1, w))


def _lru_sample_kernel(lx_ref, ly_ref, buf_ref, h0_ref, cw_ref, cb_ref, wa_ref, ba_ref, wi_ref, bi_ref,
                       lam_ref, ob_ref, ht_ref, *, steps):
    cw = cw_ref[...]
    n_buf = CONV_W - 1

    def tap(t):
        return buf_ref[t] if t < n_buf else lx_ref[t - n_buf]

    h = h0_ref[...]
    for t in range(steps):
        xb = cb_ref[...] + cw[0:1, :] * tap(t)
        for j in range(1, CONV_W):
            xb = xb + cw[j:j + 1, :] * tap(t + j)
        a, b = _lru_gates(xb, wa_ref, ba_ref, wi_ref, bi_ref, lam_ref)
        h = a * h + b
        ob_ref[t] = (h * jax.nn.gelu(ly_ref[t])).astype(ob_ref.dtype)
    ht_ref[...] = h


def _lru_sample(lx_tm, ly_tm, buf_tm, h0, cw, cb, wa, ba, wi, bi, lam):
    steps, n, w = lx_tm.shape
    est = 16 * n * w * 4
    return pl.pallas_call(
        functools.partial(_lru_sample_kernel, steps=steps),
        out_shape=[jax.ShapeDtypeStruct((steps, n, w), BF16), jax.ShapeDtypeStruct((n, w), F32)],
        compiler_params=_params(None, est),
        name="lru_sample",
    )(lx_tm, ly_tm, buf_tm, h0, cw, cb.reshape(1, w), wa, ba.reshape(1, w), wi, bi.reshape(1, w),
      lam.reshape(1, w))


def _rope_lanes(x, cos, sin, lane):
    half = QK_ROPE // 2
    rot = jnp.where(lane % QK_ROPE < half, pltpu.roll(x, LANES - half, 1), pltpu.roll(x, half, 1))
    return x * cos + rot * sin


def _mla_prep_kernel(cq_ref, ckv_ref, kpe_ref, cos_ref, sin_ref, qg_ref, kg_ref,
                     cqn_ref, ckv_out_ref, ckv_bf_ref, kpe_out_ref, kpe_bf_ref):
    cqn_ref[...] = _rms(cq_ref[...], qg_ref[...]).astype(cqn_ref.dtype)
    ckv = _rms(ckv_ref[...], kg_ref[...])
    ckv_out_ref[...] = ckv
    ckv_bf_ref[...] = ckv.astype(ckv_bf_ref.dtype)
    x = kpe_ref[...]
    lane = lax.broadcasted_iota(jnp.int32, x.shape, 1)
    kpe = _rope_lanes(x, cos_ref[...], sin_ref[...], lane)
    kpe_out_ref[...] = kpe
    kpe_bf_ref[...] = kpe.astype(kpe_bf_ref.dtype)


def _mla_prep(yc, cos, sin, q_g, kv_g, *, tm):
    m = yc.shape[0]
    row = lambda width, col_block: pl.BlockSpec((tm, width), lambda i: (i, col_block))
    vec = pl.BlockSpec((1, Q_LORA), lambda i: (0, 0))
    est = 2 * tm * (3 * Q_LORA + 4 * LANES) * 4 * 3
    return pl.pallas_call(
        _mla_prep_kernel,
        grid=(m // tm,),
        in_specs=[row(Q_LORA, 0), row(KV_LORA, 1), row(LANES, C_COL_KPE // LANES), row(LANES, 0),
                  row(LANES, 0), vec, vec],
        out_specs=[row(Q_LORA, 0), row(KV_LORA, 0), row(KV_LORA, 0), row(LANES, 0), row(LANES, 0)],
        out_shape=[
            jax.ShapeDtypeStruct((m, Q_LORA), BF16), jax.ShapeDtypeStruct((m, KV_LORA), F32),
            jax.ShapeDtypeStruct((m, KV_LORA), BF16), jax.ShapeDtypeStruct((m, LANES), F32),
            jax.ShapeDtypeStruct((m, LANES), BF16),
        ],
        compiler_params=_params(("parallel",), est),
        name="mla_prep",
    )(yc, yc, yc, cos, sin, q_g.reshape(1, Q_LORA), kv_g.reshape(1, KV_LORA))


def _mm_qrope_kernel(a_ref, w_ref, cos_ref, sin_ref, o_ref, *, heads):
    acc = _dot(a_ref[...], w_ref[...])
    cos = cos_ref[...]
    sin = sin_ref[...]
    lane = lax.broadcasted_iota(jnp.int32, cos.shape, 1)
    for h in range(heads):
        c0 = h * Q_HEAD_COLS
        o_ref[:, c0:c0 + QK_NOPE] = (acc[:, c0:c0 + QK_NOPE] * MLA_SCALE).astype(o_ref.dtype)
        pe = _rope_lanes(acc[:, c0 + QK_NOPE:c0 + Q_HEAD_COLS], cos, sin, lane)
        o_ref[:, c0 + QK_NOPE:c0 + Q_HEAD_COLS] = (pe * MLA_SCALE).astype(o_ref.dtype)


def _mm_qrope(a, w, cos, sin, *, tm, tn):
    m, k = a.shape
    n = w.shape[1]
    est = 2 * (tm * k * 2 + k * tn * 2 + tm * tn * 2 + 2 * tm * LANES * 4) + 2 * tm * tn * 4
    return pl.pallas_call(
        functools.partial(_mm_qrope_kernel, heads=tn // Q_HEAD_COLS),
        grid=(n // tn, m // tm),
        in_specs=[
            pl.BlockSpec((tm, k), lambda j, i: (i, 0)), pl.BlockSpec((k, tn), lambda j, i: (0, j)),
            pl.BlockSpec((tm, LANES), lambda j, i: (i, 0)), pl.BlockSpec((tm, LANES), lambda j, i: (i, 0)),
        ],
        out_specs=pl.BlockSpec((tm, tn), lambda j, i: (i, j)),
        out_shape=jax.ShapeDtypeStruct((m, n), BF16),
        compiler_params=_params(("parallel", "parallel"), est),
        name="mm_qrope",
    )(a, w, cos, sin)


def _flash_kernel(q_ref, kv_ref, kpe_ref, o_ref, m_ref, l_ref, acc_ref, *, tq, tk):
    qi = pl.program_id(2)
    q = q_ref[...]
    m_ref[...] = jnp.full(m_ref.shape, -jnp.inf, F32)
    l_ref[...] = jnp.zeros(l_ref.shape, F32)
    acc_ref[...] = jnp.zeros(acc_ref.shape, F32)
    q_pos = qi * tq + lax.broadcasted_iota(jnp.int32, (tq, tk), 0)
    k_off = lax.broadcasted_iota(jnp.int32, (tq, tk), 1)

    def body(j, carry):
        k0 = pl.multiple_of(j * tk, tk)
        kn = kv_ref[pl.ds(k0, tk), 0:QK_NOPE]
        v = kv_ref[pl.ds(k0, tk), QK_NOPE:QK_NOPE + V_HEAD]
        k = jnp.concatenate([kn, kpe_ref[pl.ds(k0, tk), :]], axis=-1)
        s = lax.dot_general(q, k, NT, preferred_element_type=F32)
        s = jnp.where(k0 + k_off <= q_pos, s, -jnp.inf)
        m_old = m_ref[...]
        m_new = jnp.maximum(m_old, jnp.max(s, axis=-1, keepdims=True))
        p = jnp.exp(s - m_new)
        corr = jnp.exp(m_old - m_new)
        l_ref[...] = l_ref[...] * corr + jnp.sum(p, axis=-1, keepdims=True)
        acc_ref[...] = acc_ref[...] * corr + lax.dot_general(p.astype(BF16), v, NN,
                                                             preferred_element_type=F32)
        m_ref[...] = m_new
        return carry

    lax.fori_loop(0, (qi * tq) // tk + 1, body, 0)
    o_ref[...] = (acc_ref[...] / l_ref[...]).astype(o_ref.dtype)


def _flash(q, kv, kpe_bf, *, n_seq, seq_len, tq, tk):
    nq = seq_len // tq
    est = 2 * (tq * Q_HEAD_COLS * 2 + seq_len * Q_HEAD_COLS * 2 + seq_len * LANES * 2 + tq * V_HEAD * 2)
    est += 6 * tq * tk * 4
    return pl.pallas_call(
        functools.partial(_flash_kernel, tq=tq, tk=tk),
        grid=(n_seq, MLA_HEADS, nq),
        in_specs=[
            pl.BlockSpec((tq, Q_HEAD_COLS), lambda b, h, i: (b * nq + i, h)),
            pl.BlockSpec((seq_len, Q_HEAD_COLS), lambda b, h, i: (b, h)),
            pl.BlockSpec((seq_len, LANES), lambda b, h, i: (b, 0)),
        ],
        out_specs=pl.BlockSpec((tq, V_HEAD), lambda b, h, i: (b * nq + i, h)),
        out_shape=jax.ShapeDtypeStruct((n_seq * seq_len, MLA_HEADS * V_HEAD), BF16),
        scratch_shapes=[pltpu.VMEM((tq, 1), F32), pltpu.VMEM((tq, 1), F32), pltpu.VMEM((tq, V_HEAD), F32)],
        compiler_params=_params(("parallel", "parallel", "arbitrary"), est),
        name="flash_prompt",
    )(q, kv, kpe_bf)


def _head_mm_kernel(a_ref, w_ref, o_ref):
    o_ref[...] = _dot(a_ref[...], w_ref[0]).astype(o_ref.dtype)


def _head_mm(a, w, *, a_width, a_col_stride, out_dtype, name):
    m = a.shape[0]
    heads, k, n = w.shape
    est = 2 * (m * a_width * 2 + k * n * 2 + m * n * 4) + m * n * 4
    return pl.pallas_call(
        _head_mm_kernel,
        grid=(heads,),
        in_specs=[
            pl.BlockSpec((m, a_width), lambda h: (0, h * a_col_stride)),
            pl.BlockSpec((1, k, n), lambda h: (h, 0, 0)),
        ],
        out_specs=pl.BlockSpec((m, n), lambda h: (0, h)),
        out_shape=jax.ShapeDtypeStruct((m, heads * n), out_dtype),
        compiler_params=_params(("parallel",), est),
        name=name,
    )(a, w)


def _paged_kernel(pt_ref, qlat_ref, qpe_ref, ckn_ref, kpn_ref, *rest, pages, t_new):
    ck_refs = rest[:pages]
    kp_refs = rest[pages:2 * pages]
    o_ref, m_ref, l_ref, acc_ref = rest[2 * pages:]
    step = pl.program_id(1)

    @pl.when(step == 0)
    def _():
        m_ref[...] = jnp.full(m_ref.shape, -jnp.inf, F32)
        l_ref[...] = jnp.zeros(l_ref.shape, F32)
        acc_ref[...] = jnp.zeros(acc_ref.shape, F32)

    ql = qlat_ref[0]
    qp = qpe_ref[0][:, :QK_ROPE]

    def update(s, ck):
        m_old = m_ref[...]
        m_new = jnp.maximum(m_old, jnp.max(s, axis=-1, keepdims=True))
        p = jnp.exp(s - m_new)
        corr = jnp.exp(m_old - m_new)
        l_ref[...] = l_ref[...] * corr + jnp.sum(p, axis=-1, keepdims=True)
        acc_ref[...] = acc_ref[...] * corr + lax.dot_general(p.astype(BF16), ck, NN,
                                                             preferred_element_type=F32)
        m_ref[...] = m_new

    ck = jnp.concatenate([r[0, 0].astype(BF16) for r in ck_refs], axis=0)
    kp = jnp.concatenate([r[0, 0].astype(BF16) for r in kp_refs], axis=0)
    s = lax.dot_general(ql, ck, NT, preferred_element_type=F32)
    s = s + lax.dot_general(qp, kp, NT, preferred_element_type=F32)
    update(s, ck)

    @pl.when(step == pl.num_programs(1) - 1)
    def _():
        rows = ql.shape[0]
        pad = PAGE_SIZE - ckn_ref.shape[1]
        ckn = jnp.concatenate([ckn_ref[0], jnp.zeros((pad, KV_LORA), F32)], axis=0).astype(BF16)
        kpn = jnp.concatenate([kpn_ref[0], jnp.zeros((pad, LANES), F32)], axis=0)[:, :QK_ROPE].astype(BF16)
        sn = lax.dot_general(ql, ckn, NT, preferred_element_type=F32)
        sn = sn + lax.dot_general(qp, kpn, NT, preferred_element_type=F32)
        t_row = lax.broadcasted_iota(jnp.int32, (rows, PAGE_SIZE), 0) // MLA_HEADS
        j_col = lax.broadcasted_iota(jnp.int32, (rows, PAGE_SIZE), 1)
        sn = jnp.where((j_col < t_new) & (j_col <= t_row), sn, -jnp.inf)
        update(sn, ckn)
        o_ref[0] = (acc_ref[...] / l_ref[...]).astype(o_ref.dtype)


def _paged(page_table, qlat, qpe, ckv_new8, kpe_new8, cache_ckv, cache_kpe, *, layer, t_new):
    n_seq, rows, _ = qlat.shape
    n_pages = page_table.shape[1]
    pages = PAGES_PER_STEP
    steps = n_pages // pages
    pt = page_table.reshape(-1).astype(jnp.int32)
    t_pad = ckv_new8.shape[1]

    def page_spec(width, k):
        return pl.BlockSpec((1, 1, PAGE_SIZE, width),
                            lambda b, s, pt_ref: (layer, pt_ref[b * n_pages + s * pages + k], 0, 0))

    in_specs = [
        pl.BlockSpec((1, rows, KV_LORA), lambda b, s, pt_ref: (b, 0, 0)),
        pl.BlockSpec((1, rows, LANES), lambda b, s, pt_ref: (b, 0, 0)),
        pl.BlockSpec((1, t_pad, KV_LORA), lambda b, s, pt_ref: (b, 0, 0)),
        pl.BlockSpec((1, t_pad, LANES), lambda b, s, pt_ref: (b, 0, 0)),
    ]
    in_specs += [page_spec(KV_LORA, k) for k in range(pages)]
    in_specs += [page_spec(QK_ROPE, k) for k in range(pages)]
    est = 2 * pages * PAGE_SIZE * (KV_LORA + LANES) * 4 + 3 * pages * PAGE_SIZE * (KV_LORA + LANES) * 2
    est += 8 * rows * (KV_LORA + pages * PAGE_SIZE) * 4
    grid_spec = pltpu.PrefetchScalarGridSpec(
        num_scalar_prefetch=1,
        grid=(n_seq, steps),
        in_specs=in_specs,
        out_specs=pl.BlockSpec((1, rows, KV_LORA), lambda b, s, pt_ref: (b, 0, 0)),
        scratch_shapes=[pltpu.VMEM((rows, 1), F32), pltpu.VMEM((rows, 1), F32),
                        pltpu.VMEM((rows, KV_LORA), F32)],
    )
    return pl.pallas_call(
        functools.partial(_paged_kernel, pages=pages, t_new=t_new),
        grid_spec=grid_spec,
        out_shape=jax.ShapeDtypeStruct((n_seq, rows, KV_LORA), BF16),
        compiler_params=_params(("parallel", "arbitrary"), est),
        name="paged_attn",
    )(pt, qlat, qpe, ckv_new8, kpe_new8, *([cache_ckv] * pages), *([cache_kpe] * pages))


def _pad_rows(x, rows_before, rows_total):
    pad = [(0, 0)] * x.ndim
    pad[1] = (rows_before, rows_total - rows_before - x.shape[1])
    return jnp.pad(x, pad)


def _conv_state(buf, x):
    keep = CONV_W - 1
    if x.shape[1] >= keep:
        return x[:, x.shape[1] - keep:]
    return jnp.concatenate([buf.astype(x.dtype), x], axis=1)[:, -keep:]


def _rope_tables(positions):
    half = QK_ROPE // 2
    inv = ROPE_BASE ** (-jnp.arange(half, dtype=F32) / half)
    ang = positions.astype(F32)[:, None] * inv[None, :]
    cos, sin = jnp.cos(ang), jnp.sin(ang)
    zeros = jnp.zeros((positions.shape[0], LANES - QK_ROPE), F32)
    return (jnp.concatenate([cos, cos, zeros], axis=-1), jnp.concatenate([-sin, sin, zeros], axis=-1))


def kernel(x_prompt, x_sample, p_prompt, p_sample, state_gdn, state_gdn_conv, state_lru, state_lru_conv, cache_ckv, cache_kpe, page_table, w_in_a, conv_qkv_w, gdn_a_log, gdn_dt_bias, gdn_norm_g, conv_lru_w, conv_lru_b, lru_wa, lru_ba, lru_wi, lru_bi, lru_lambda, w_out_a, w_in_c, q_norm_g, kv_norm_g, w_uq, w_ukv, w_o_c, norm_mix_g, norm_ffn_g, w_up, w_down, w_ple, w_ple_gate, norm_final_g):
    bp, seq, d = x_prompt.shape
    bs, ts, _ = x_sample.shape
    mp, ms = bp * seq, bs * ts
    m = mp + ms
    depth = norm_mix_g.shape[0]
    past = page_table.shape[1] * PAGE_SIZE
    tm = 512

    h = jnp.concatenate([x_prompt.reshape(mp, d), x_sample.reshape(ms, d)], axis=0)
    pos = jnp.concatenate([jnp.tile(jnp.arange(seq), bp), jnp.tile(past + jnp.arange(ts), bs)])
    cos, sin = _rope_tables(pos)
    hn = _rmsnorm(h, norm_mix_g[0], tm=tm, out_dtype=BF16)

    gdn_p, gdn_s, gconv_p, gconv_s, lru_p, lru_s, lconv_p, lconv_s = [], [], [], [], [], [], [], []
    ckv_p, ckv_s, kpe_p, kpe_s = [], [], [], []

    for i in range(depth):
        j = i // 2
        if i % 2 == 0:
            wa = w_in_a[j]
            cut_b = GDN_QKV + GDN_WIDTH
            cut_lx = cut_b + 2 * GDN_HEADS
            w_a = jnp.concatenate([
                wa[:, :cut_b], wa[:, cut_lx:], wa[:, cut_b:cut_lx],
                jnp.zeros((d, LANES - 2 * GDN_HEADS), F32)], axis=1).astype(BF16)
            ya = _mm(hn, w_a, tm=tm, tn=A_COLS // 7, out_dtype=F32, name="in_proj_a")
            ya_p = ya.reshape(1, m, A_COLS)
            ya_s = ya[mp:].reshape(bs, ts, A_COLS)

            zero_buf = jnp.zeros((bp, SUBLANES, GDN_QKV), F32)
            o_gp, s_gp = _gdn(ya_p, zero_buf, jnp.zeros((bp,) + state_gdn.shape[2:], F32), conv_qkv_w[j],
                              gdn_a_log[j], gdn_dt_bias[j], gdn_norm_g[j], n_seq=bp, seq_rows=seq, row0=0,
                              chunk=GDN_CHUNK, bb=1, l_real=GDN_CHUNK)
            o_lp, h_lp = _lru_prompt(ya_p, jnp.zeros((bp, SUBLANES, LRU_WIDTH), F32),
                                     jnp.zeros((bp, 1, LRU_WIDTH), F32), conv_lru_w[j], conv_lru_b[j],
                                     lru_wa[j].astype(BF16), lru_ba[j].reshape(-1), lru_wi[j].astype(BF16),
                                     lru_bi[j].reshape(-1), lru_lambda[j], n_seq=bp, seq_rows=seq, tl=256)

            t_pad = SUBLANES
            ya_s8 = _pad_rows(ya_s, 0, t_pad)
            buf_g8 = _pad_rows(state_gdn_conv[j], SUBLANES - (CONV_W - 1), SUBLANES)
            o_gs, s_gs = _gdn(ya_s8, buf_g8, state_gdn[j], conv_qkv_w[j], gdn_a_log[j], gdn_dt_bias[j],
                              gdn_norm_g[j], n_seq=bs, seq_rows=t_pad, row0=0, chunk=t_pad, bb=8, l_real=ts)
            lx_tm = ya_s[:, :, A_COL_LX:A_COL_LY].transpose(1, 0, 2)
            ly_tm = ya_s[:, :, A_COL_LY:A_COL_BA].transpose(1, 0, 2)
            o_ls_tm, h_ls = _lru_sample(lx_tm, ly_tm, state_lru_conv[j].transpose(1, 0, 2), state_lru[j],
                                        conv_lru_w[j], conv_lru_b[j], lru_wa[j].astype(BF16),
                                        lru_ba[j].reshape(-1), lru_wi[j].astype(BF16), lru_bi[j].reshape(-1),
                                        lru_lambda[j])
            mix = jnp.concatenate([
                jnp.concatenate([o_gp.reshape(mp, GDN_WIDTH), o_lp.reshape(mp, LRU_WIDTH)], axis=1),
                jnp.concatenate([o_gs[:, :ts].reshape(ms, GDN_WIDTH),
                                 o_ls_tm.transpose(1, 0, 2).reshape(ms, LRU_WIDTH)], axis=1)], axis=0)
            h = _mm(mix, w_out_a[j].astype(BF16), tm=tm, tn=1024, out_dtype=F32, epilogue="residual",
                    residual=h, name="out_proj_a")

            ya_pp = ya[:mp].reshape(bp, seq, A_COLS)[:, seq - (CONV_W - 1):]
            gdn_p.append(s_gp)
            gdn_s.append(s_gs)
            gconv_p.append(ya_pp[:, :, :GDN_QKV])
            gconv_s.append(_conv_state(state_gdn_conv[j], ya_s[:, :, :GDN_QKV]))
            lru_p.append(h_lp.reshape(bp, LRU_WIDTH))
            lru_s.append(h_ls)
            lconv_p.append(ya_pp[:, :, A_COL_LX:A_COL_LY])
            lconv_s.append(_conv_state(state_lru_conv[j], ya_s[:, :, A_COL_LX:A_COL_LY]))
        else:
            w_c = jnp.concatenate([w_in_c[j], jnp.zeros((d, LANES - QK_ROPE), F32)], axis=1).astype(BF16)
            yc = _mm(hn, w_c, tm=tm, tn=C_COLS // 3, out_dtype=F32, name="in_proj_c")
            cqn, ckv, ckv_bf, kpe, kpe_bf = _mla_prep(yc, cos, sin, q_norm_g[j], kv_norm_g[j], tm=tm)
            w_q = w_uq[j].reshape(Q_LORA, MLA_HEADS, QK_NOPE + QK_ROPE)
            w_q = jnp.pad(w_q, ((0, 0), (0, 0), (0, Q_HEAD_COLS - QK_NOPE - QK_ROPE)))
            q = _mm_qrope(cqn, w_q.reshape(Q_LORA, MLA_HEADS * Q_HEAD_COLS).astype(BF16), cos, sin,
                          tm=tm, tn=1024)
            w_r = w_ukv[j].reshape(KV_LORA, MLA_HEADS, QK_NOPE + V_HEAD)

            kv = _mm(ckv_bf[:mp], w_ukv[j].astype(BF16), tm=tm, tn=1024, out_dtype=BF16, name="kv_up")
            o_p = _flash(q, kv, kpe_bf, n_seq=bp, seq_len=seq, tq=512, tk=512)

            q_s = q[mp:]
            w_uk_t = w_r[:, :, :QK_NOPE].transpose(1, 2, 0).astype(BF16)
            w_uv = w_r[:, :, QK_NOPE:].transpose(1, 0, 2).astype(BF16)
            qlat = _head_mm(q_s, w_uk_t, a_width=QK_NOPE, a_col_stride=Q_HEAD_COLS // QK_NOPE,
                            out_dtype=BF16, name="q_latent")
            qlat = qlat.reshape(bs, ts * MLA_HEADS, KV_LORA)
            qpe = q_s.reshape(ms, MLA_HEADS, Q_HEAD_COLS)[:, :, QK_NOPE:].reshape(bs, ts * MLA_HEADS, LANES)
            ckv_new8 = _pad_rows(ckv[mp:].reshape(bs, ts, KV_LORA), 0, SUBLANES)
            kpe_new8 = _pad_rows(kpe[mp:].reshape(bs, ts, LANES), 0, SUBLANES)
            lat = _paged(page_table, qlat, qpe, ckv_new8, kpe_new8, cache_ckv, cache_kpe, layer=j, t_new=ts)
            o_s = _head_mm(lat.reshape(ms, MLA_HEADS * KV_LORA), w_uv, a_width=KV_LORA, a_col_stride=1,
                           out_dtype=BF16, name="v_up")
            h = _mm(jnp.concatenate([o_p, o_s], axis=0), w_o_c[j].astype(BF16), tm=tm, tn=1024,
                    out_dtype=F32, epilogue="residual", residual=h, name="out_proj_c")
            ckv_p.append(ckv[:mp].reshape(bp, seq, KV_LORA))
            ckv_s.append(ckv[mp:].reshape(bs, ts, KV_LORA))
            kpe_p.append(kpe[:mp, :QK_ROPE].reshape(bp, seq, QK_ROPE))
            kpe_s.append(kpe[mp:, :QK_ROPE].reshape(bs, ts, QK_ROPE))

        hn = _rmsnorm(h, norm_ffn_g[i], tm=tm, out_dtype=BF16)
        ff = _mm(hn, w_up[i].astype(BF16), tm=tm, tn=1024, out_dtype=BF16, epilogue="relu2", name="ffn_up")
        h = _mm(ff, w_down[i].astype(BF16), tm=tm, tn=512, out_dtype=F32, epilogue="residual", residual=h,
                name="ffn_down")
        p = jnp.concatenate([p_prompt[i].reshape(mp, -1), p_sample[i].reshape(ms, -1)], axis=0)
        last = i + 1 == depth
        g_next = norm_final_g if last else norm_mix_g[i + 1]
        h, hn = _ple(h, p, w_ple[i].astype(BF16), w_ple_gate[i].astype(BF16), g_next, tm=256,
                     norm_dtype=F32 if last else BF16)

    y = hn
    return (y[:mp].reshape(bp, seq, d), y[mp:].reshape(bs, ts, d),
            jnp.stack(gdn_p), jnp.stack(gdn_s), jnp.stack(gconv_p), jnp.stack(gconv_s),
            jnp.stack(lru_p), jnp.stack(lru_s), jnp.stack(lconv_p), jnp.stack(lconv_s),
            jnp.stack(ckv_p), jnp.stack(ckv_s), jnp.stack(kpe_p), jnp.stack(kpe_s))
```

```python
import functools
import math

import jax
import jax.numpy as jnp
from jax import lax
from jax.experimental import pallas as pl
from jax.experimental.pallas import tpu as pltpu

F32 = jnp.float32
BF16 = jnp.bfloat16

D_MODEL = 2048
CONV_W = 4
GDN_HEADS = 8
GDN_DK = 128
GDN_DV = 128
GDN_QK = GDN_HEADS * GDN_DK
GDN_WIDTH = GDN_HEADS * GDN_DV
GDN_QKV = 2 * GDN_QK + GDN_WIDTH
GDN_CHUNK = 64
LRU_WIDTH = D_MODEL // 2
LRU_BLOCKS = 8
LRU_BW = LRU_WIDTH // LRU_BLOCKS
LRU_C = 8.0
MLA_HEADS = 16
Q_LORA = 512
KV_LORA = 512
QK_NOPE = 128
QK_ROPE = 64
V_HEAD = 128
MLA_SCALE = (QK_NOPE + QK_ROPE) ** -0.5
ROPE_BASE = 10000.0
PAGE_SIZE = 128
EPS = 1e-6

LANES = 128
SUBLANES = 8
VMEM_LIMIT_BYTES = 56 * 1024 * 1024

A_COL_Z = GDN_QKV
A_COL_LX = A_COL_Z + GDN_WIDTH
A_COL_LY = A_COL_LX + LRU_WIDTH
A_COL_BA = A_COL_LY + LRU_WIDTH
A_COLS = A_COL_BA + LANES
C_COL_KPE = Q_LORA + KV_LORA
C_COLS = C_COL_KPE + LANES
Q_HEAD_COLS = 2 * LANES
PAGES_PER_STEP = 8
SEQS_PER_STEP = 2
GDN_ROWS = 64

NN = (((1,), (0,)), ((), ()))
NT = (((1,), (1,)), ((), ()))
TN = (((0,), (0,)), ((), ()))


def _params(semantics, est_bytes):
    limit = int(min(max(2 * est_bytes, 32 * 1024 * 1024), VMEM_LIMIT_BYTES))
    return pltpu.CompilerParams(dimension_semantics=semantics, vmem_limit_bytes=limit)


def _vmem_params(semantics, need_bytes):
    limit = int(min(max(need_bytes + 4 * 1024 * 1024, 32 * 1024 * 1024), VMEM_LIMIT_BYTES))
    return pltpu.CompilerParams(dimension_semantics=semantics, vmem_limit_bytes=limit)


def _nbytes(shape, dtype):
    return math.prod(shape) * jnp.dtype(dtype).itemsize


def _dot(a, b, dims=NN):
    return lax.dot_general(a.astype(BF16), b.astype(BF16), dims, preferred_element_type=F32)


def _dot_hi(a, b, dims=NN):
    return lax.dot_general(a, b, dims, precision=lax.Precision.HIGHEST, preferred_element_type=F32)


def _sigmoid(x):
    return jax.nn.sigmoid(x)


def _softplus(x):
    return jnp.maximum(x, 0.0) + jnp.log1p(jnp.exp(-jnp.abs(x)))


def _rms(x, g):
    return x * lax.rsqrt(jnp.mean(x * x, axis=-1, keepdims=True) + EPS) * g


def _rmsnorm_kernel(x_ref, g_ref, o_ref):
    o_ref[...] = _rms(x_ref[...], g_ref[...]).astype(o_ref.dtype)


def _rmsnorm(x, g, *, tm, out_dtype):
    m, d = x.shape
    est = 2 * (_nbytes((tm, d), F32) + _nbytes((tm, d), out_dtype)) + 2 * _nbytes((tm, d), F32)
    return pl.pallas_call(
        _rmsnorm_kernel,
        grid=(m // tm,),
        in_specs=[pl.BlockSpec((tm, d), lambda i: (i, 0)), pl.BlockSpec((1, d), lambda i: (0, 0))],
        out_specs=pl.BlockSpec((tm, d), lambda i: (i, 0)),
        out_shape=jax.ShapeDtypeStruct((m, d), out_dtype),
        compiler_params=_params(("parallel",), est),
        name="rmsnorm",
    )(x, g.reshape(1, d))


def _mm_kernel(a_ref, w_ref, *rest, epilogue, cast_w):
    if cast_w:
        *rest, wbf_ref = rest

        @pl.when(pl.program_id(1) == 0)
        def _():
            wbf_ref[...] = w_ref[0].astype(BF16)

        w = wbf_ref[...]
    else:
        w = w_ref[...]
    o_ref = rest[-1]
    acc = _dot(a_ref[...], w)
    if epilogue == "relu2":
        acc = jnp.square(jnp.maximum(acc, 0.0))
    elif epilogue == "residual":
        acc = acc + rest[0][...]
    o_ref[...] = acc.astype(o_ref.dtype)


def _mm(a, w, *, tm, tn, out_dtype, epilogue=None, residual=None, layer=None, single_buffer_w=False, name):
    m, k = a.shape
    n = w.shape[-1]
    cast_w = layer is not None
    if cast_w:
        mode = dict(pipeline_mode=pl.Buffered(1)) if single_buffer_w else {}
        w_spec = pl.BlockSpec((1, k, tn), lambda j, i: (layer, 0, j), **mode)
        w_bytes = (1 if single_buffer_w else 2) * _nbytes((k, tn), F32) + _nbytes((k, tn), BF16)
        scratch = [pltpu.VMEM((k, tn), BF16)]
    else:
        w_spec = pl.BlockSpec((k, tn), lambda j, i: (0, j))
        w_bytes = 2 * _nbytes((k, tn), w.dtype)
        scratch = []
    in_specs = [pl.BlockSpec((tm, k), lambda j, i: (i, 0)), w_spec]
    args = [a, w]
    est = w_bytes + 2 * (_nbytes((tm, k), a.dtype) + _nbytes((tm, tn), out_dtype)) + 2 * _nbytes((tm, tn), F32)
    if epilogue == "residual":
        in_specs.append(pl.BlockSpec((tm, tn), lambda j, i: (i, j)))
        args.append(residual)
        est += 2 * _nbytes((tm, tn), F32)
    return pl.pallas_call(
        functools.partial(_mm_kernel, epilogue=epilogue, cast_w=cast_w),
        grid=(n // tn, m // tm),
        in_specs=in_specs,
        out_specs=pl.BlockSpec((tm, tn), lambda j, i: (i, j)),
        out_shape=jax.ShapeDtypeStruct((m, n), out_dtype),
        scratch_shapes=scratch,
        compiler_params=_vmem_params(("parallel", "arbitrary"), est),
        name=name,
    )(*args)


def _ple_kernel(h_ref, p_ref, wp_ref, wg_ref, g_ref, h_out_ref, n_out_ref, wp_bf_ref, wg_bf_ref):
    @pl.when(pl.program_id(0) == 0)
    def _():
        wp_bf_ref[...] = wp_ref[0].astype(BF16)
        wg_bf_ref[...] = wg_ref[0].astype(BF16)

    h = h_ref[...]
    gate = _sigmoid(_dot(h, wg_bf_ref[...]))
    h2 = h + _dot(p_ref[...], wp_bf_ref[...]) * gate
    h_out_ref[...] = h2
    n_out_ref[...] = _rms(h2, g_ref[...]).astype(n_out_ref.dtype)


def _ple(h, p, wp, wg, g, *, layer, tm, norm_dtype):
    m, d = h.shape
    pd = p.shape[1]
    est = (_nbytes((pd, d), F32) + _nbytes((d, d), F32) + _nbytes((pd, d), BF16) + _nbytes((d, d), BF16)
           + 2 * (2 * _nbytes((tm, d), F32) + _nbytes((tm, pd), F32) + _nbytes((tm, d), norm_dtype))
           + 3 * _nbytes((tm, d), F32))
    once = dict(pipeline_mode=pl.Buffered(1))
    return pl.pallas_call(
        _ple_kernel,
        grid=(m // tm,),
        in_specs=[
            pl.BlockSpec((tm, d), lambda i: (i, 0)),
            pl.BlockSpec((tm, pd), lambda i: (i, 0)),
            pl.BlockSpec((1, pd, d), lambda i: (layer, 0, 0), **once),
            pl.BlockSpec((1, d, d), lambda i: (layer, 0, 0), **once),
            pl.BlockSpec((1, d), lambda i: (0, 0)),
        ],
        out_specs=[pl.BlockSpec((tm, d), lambda i: (i, 0)), pl.BlockSpec((tm, d), lambda i: (i, 0))],
        out_shape=[jax.ShapeDtypeStruct((m, d), F32), jax.ShapeDtypeStruct((m, d), norm_dtype)],
        scratch_shapes=[pltpu.VMEM((pd, d), BF16), pltpu.VMEM((d, d), BF16)],
        compiler_params=_vmem_params(("arbitrary",), est),
        name="ple",
    )(h, p, wp, wg, g.reshape(1, d))


def _gdn_intra_kernel(qkv_ref, prev_ref, ba_ref, z_ref, buf_ref, cw_ref, alog_ref, dtb_ref,
                      u_ref, w_ref, qg_ref, kd_ref, qk_ref, el_ref, zs_ref, xs_ref, *, bb, tl, l_real):
    rows = bb * tl
    i = pl.program_id(1)
    xs_ref[:, 0:SUBLANES, :] = jnp.where(i == 0, buf_ref[...], prev_ref[...])
    x_in = qkv_ref[...]
    xs_ref[:, SUBLANES:SUBLANES + tl, :] = x_in
    cw = cw_ref[...]
    halo0 = SUBLANES - (CONV_W - 1)
    y = cw[CONV_W - 1:CONV_W, :] * x_in
    for j in range(CONV_W - 1):
        y = y + cw[j:j + 1, :] * xs_ref[:, halo0 + j:halo0 + j + tl, :]
    y = (y * _sigmoid(y)).reshape(rows, GDN_QKV)
    ba = ba_ref[...].reshape(rows, LANES)
    z = z_ref[...].reshape(rows, GDN_WIDTH)
    zs_ref[0] = (z * _sigmoid(z)).astype(zs_ref.dtype)

    shift = int(math.log2(tl))
    ri = lax.broadcasted_iota(jnp.int32, (rows, rows), 0)
    ci = lax.broadcasted_iota(jnp.int32, (rows, rows), 1)
    if bb == 1:
        lower = ri >= ci
        strict = ri > ci
    else:
        same = (ri >> shift) == (ci >> shift)
        lower = same & (ri >= ci)
        strict = same & (ri > ci)
    tril = lower.astype(F32)
    seg_last = (ci == ((ri >> shift) << shift) + (tl - 1)).astype(F32)
    lane = lax.broadcasted_iota(jnp.int32, (rows, LANES), 1)
    beta_all = _sigmoid(ba)
    g_all = -jnp.exp(alog_ref[...]) * _softplus(ba + dtb_ref[...])
    g_all = jnp.where(lane >= GDN_HEADS, g_all, 0.0)
    if l_real < tl:
        r1 = lax.broadcasted_iota(jnp.int32, (rows, 1), 0)
        rowmask = ((r1 & (tl - 1)) < l_real).astype(F32)
        y = y * rowmask
        beta_all = beta_all * rowmask
        g_all = g_all * rowmask
    gc_all = _dot_hi(tril, g_all)
    gl_all = _dot_hi(seg_last, gc_all)
    el_ref[0] = jnp.exp(gl_all)
    eg_all = jnp.exp(gc_all)
    kdf_all = jnp.exp(gl_all - gc_all)
    gct = jnp.concatenate([gc_all, jnp.zeros((LANES - rows, LANES), F32)], axis=0).T
    n_factors = max(1, math.ceil(math.log2(l_real)))

    for h in range(GDN_HEADS):
        hs = slice(h * GDN_DV, (h + 1) * GDN_DV)
        q = y[:, h * GDN_DK:(h + 1) * GDN_DK]
        q = q * lax.rsqrt(jnp.sum(q * q, axis=-1, keepdims=True) + EPS) * (GDN_DK ** -0.5)
        k = y[:, GDN_QK + h * GDN_DK:GDN_QK + (h + 1) * GDN_DK]
        k = k * lax.rsqrt(jnp.sum(k * k, axis=-1, keepdims=True) + EPS)
        v = y[:, 2 * GDN_QK + h * GDN_DV:2 * GDN_QK + (h + 1) * GDN_DV]
        lh = GDN_HEADS + h
        beta = beta_all[:, h:h + 1]
        gc = gc_all[:, lh:lh + 1]
        decay = jnp.where(lower, jnp.exp(gc - gct[lh:lh + 1, 0:rows]), 0.0)
        kb = k * beta
        a = jnp.where(strict, _dot(kb, k, NT) * decay, 0.0)
        x = jnp.concatenate([v * beta, kb * eg_all[:, lh:lh + 1]], axis=-1)
        p = -a
        for f in range(n_factors):
            x = x + _dot(p, x)
            if f + 1 < n_factors:
                p = _dot(p, p)
        u_ref[0, :, hs] = x[:, :GDN_DV]
        w_ref[0, :, hs] = x[:, GDN_DV:].astype(w_ref.dtype)
        qk = jnp.where(lower, _dot(q, k, NT) * decay, 0.0)
        qk_ref[h, 0] = qk.astype(qk_ref.dtype)
        qg_ref[0, :, hs] = (q * eg_all[:, lh:lh + 1]).astype(qg_ref.dtype)
        kd_ref[0, :, hs] = (k * kdf_all[:, lh:lh + 1]).astype(kd_ref.dtype)


def _gdn_intra(ya3, buf8, conv_w, a_log, dt_bias, *, flat, n_groups, nt, bb, tl, l_real):
    rows = bb * tl
    assert rows == GDN_ROWS and (flat or nt == 1)

    def spec(width, col_block):
        if flat:
            return pl.BlockSpec((1, rows, width), lambda g, i: (0, g * nt + i, col_block))
        return pl.BlockSpec((bb, tl, width), lambda g, i: (g, 0, col_block))

    if flat:
        prev = pl.BlockSpec((1, SUBLANES, GDN_QKV),
                            lambda g, i: (0, jnp.maximum((g * nt + i) * (rows // SUBLANES) - 1, 0), 0))
    else:
        prev = spec(GDN_QKV, 0)
    lane_pad = jnp.zeros((1, LANES), F32)
    alog = lane_pad.at[0, GDN_HEADS:2 * GDN_HEADS].set(a_log)
    dtb = lane_pad.at[0, GDN_HEADS:2 * GDN_HEADS].set(dt_bias)
    total = nt * rows
    out = lambda width: pl.BlockSpec((1, rows, width), lambda g, i: (g, i, 0))
    shape = lambda width, dt: jax.ShapeDtypeStruct((n_groups, total, width), dt)
    est = 4 * rows * (2 * GDN_QKV + 6 * GDN_WIDTH) * 4
    return pl.pallas_call(
        functools.partial(_gdn_intra_kernel, bb=bb, tl=tl, l_real=l_real),
        grid=(n_groups, nt),
        in_specs=[
            spec(GDN_QKV, 0), prev, spec(LANES, A_COL_BA // LANES), spec(GDN_WIDTH, A_COL_Z // GDN_WIDTH),
            pl.BlockSpec((bb, SUBLANES, GDN_QKV), lambda g, i: (g, 0, 0)),
            pl.BlockSpec((CONV_W, GDN_QKV), lambda g, i: (0, 0)),
            pl.BlockSpec((1, LANES), lambda g, i: (0, 0)),
            pl.BlockSpec((1, LANES), lambda g, i: (0, 0)),
        ],
        out_specs=[
            out(GDN_WIDTH), out(GDN_WIDTH), out(GDN_WIDTH), out(GDN_WIDTH),
            pl.BlockSpec((GDN_HEADS, 1, rows, rows), lambda g, i: (0, g, i, 0)),
            out(LANES), out(GDN_WIDTH),
        ],
        out_shape=[
            shape(GDN_WIDTH, F32), shape(GDN_WIDTH, BF16), shape(GDN_WIDTH, BF16), shape(GDN_WIDTH, BF16),
            jax.ShapeDtypeStruct((GDN_HEADS, n_groups, total, rows), BF16),
            shape(LANES, F32), shape(GDN_WIDTH, BF16),
        ],
        scratch_shapes=[pltpu.VMEM((bb, tl + SUBLANES, GDN_QKV), F32)],
        compiler_params=_params(("parallel", "parallel"), est),
        name="gdn_intra",
    )(ya3, ya3, ya3, ya3, buf8, conv_w, alog, dtb)


def _gdn_state_kernel(u_ref, w_ref, qg_ref, kd_ref, qk_ref, el_ref, zs_ref, s0_ref, ng_ref,
                      o_ref, sout_ref, s_ref, *, nb, nseg, tl):
    i = pl.program_id(1)

    @pl.when(i == 0)
    def _():
        s_ref[...] = s0_ref[...]

    rows = nseg * tl
    seg_of_row = lax.broadcasted_iota(jnp.int32, (rows, 1), 0) >> int(math.log2(tl))
    for blk in range(nb):
        for h in range(GDN_HEADS):
            hs = slice(h * GDN_DV, (h + 1) * GDN_DV)
            u = u_ref[blk, :, hs]
            w = w_ref[blk, :, hs].astype(F32)
            qg = qg_ref[blk, :, hs].astype(F32)
            kd = kd_ref[blk, :, hs].astype(F32)
            v_parts, o_parts = [], []
            for s in range(nseg):
                rs = slice(s * tl, (s + 1) * tl)
                r = _dot(jnp.concatenate([w[rs], qg[rs]], axis=0), s_ref[blk * nseg + s, h])
                v_parts.append(u[rs] - r[:tl])
                o_parts.append(r[tl:])
            v_new = v_parts[0] if nseg == 1 else jnp.concatenate(v_parts, axis=0)
            o_inter = o_parts[0] if nseg == 1 else jnp.concatenate(o_parts, axis=0)
            o = o_inter + _dot(qk_ref[h, blk], v_new)
            lh = GDN_HEADS + h
            for s in range(nseg):
                e = el_ref[blk, s * tl:s * tl + 1, lh:lh + 1]
                kds = kd if nseg == 1 else jnp.where(seg_of_row == s, kd, 0.0)
                idx = blk * nseg + s
                s_ref[idx, h] = s_ref[idx, h] * e + _dot(kds, v_new, TN)
            on = _rms(o, ng_ref[...]) * zs_ref[blk, :, hs].astype(F32)
            o_ref[blk, :, hs] = on.astype(o_ref.dtype)

    @pl.when(i == pl.num_programs(1) - 1)
    def _():
        sout_ref[...] = s_ref[...]


def _gdn_state(u, w, qg, kd, qk, el, zs, s0, norm_g, *, nb, nseg, tl):
    n_groups, total, _ = u.shape
    rows = nseg * tl
    assert rows == GDN_ROWS
    blk = lambda width: pl.BlockSpec((nb, rows, width), lambda g, i: (g, i, 0))
    st = pl.BlockSpec((nb * nseg, GDN_HEADS, GDN_DK, GDN_DV), lambda g, i: (g, 0, 0, 0))
    est = 2 * nb * rows * GDN_WIDTH * 16 + 5 * nb * nseg * GDN_HEADS * GDN_DK * GDN_DV * 4
    return pl.pallas_call(
        functools.partial(_gdn_state_kernel, nb=nb, nseg=nseg, tl=tl),
        grid=(n_groups // nb, total // rows),
        in_specs=[
            blk(GDN_WIDTH), blk(GDN_WIDTH), blk(GDN_WIDTH), blk(GDN_WIDTH),
            pl.BlockSpec((GDN_HEADS, nb, rows, rows), lambda g, i: (0, g, i, 0)),
            blk(LANES), blk(GDN_WIDTH), st,
            pl.BlockSpec((1, GDN_DV), lambda g, i: (0, 0)),
        ],
        out_specs=[blk(GDN_WIDTH), st],
        out_shape=[
            jax.ShapeDtypeStruct((n_groups, total, GDN_WIDTH), BF16),
            jax.ShapeDtypeStruct(s0.shape, F32),
        ],
        scratch_shapes=[pltpu.VMEM((nb * nseg, GDN_HEADS, GDN_DK, GDN_DV), F32)],
        compiler_params=_params(("parallel", "arbitrary"), est),
        name="gdn_state",
    )(u, w, qg, kd, qk, el, zs, s0, norm_g.reshape(1, GDN_DV))


def _lru_gates(xb, wa_ref, ba_ref, wi_ref, bi_ref, lam_ref):
    c = -LRU_C * _softplus(-lam_ref[...])
    a_parts, b_parts = [], []
    for kb in range(LRU_BLOCKS):
        sl = slice(kb * LRU_BW, (kb + 1) * LRU_BW)
        xk = xb[:, sl]
        r = _sigmoid(_dot(xk, wa_ref[kb]) + ba_ref[:, sl])
        ig = _sigmoid(_dot(xk, wi_ref[kb]) + bi_ref[:, sl])
        log_a = c[:, sl] * r
        a_parts.append(jnp.exp(log_a))
        b_parts.append(jnp.sqrt(1.0 - jnp.exp(2.0 * log_a)) * (ig * xk))
    return jnp.concatenate(a_parts, axis=-1), jnp.concatenate(b_parts, axis=-1)


def _lru_prompt_kernel(lx_ref, ly_ref, buf_ref, h0_ref, cw_ref, cb_ref, wa_ref, ba_ref, wi_ref, bi_ref,
                       lam_ref, ob_ref, ht_ref, xs_ref, a_ref, b_ref, hs_ref, h_ref, *, tl):
    i = pl.program_id(1)

    @pl.when(i == 0)
    def _():
        h_ref[...] = h0_ref[0]
        xs_ref[0:SUBLANES, :] = buf_ref[0]

    @pl.when(i > 0)
    def _():
        xs_ref[0:SUBLANES, :] = xs_ref[tl:tl + SUBLANES, :]

    xs_ref[SUBLANES:SUBLANES + tl, :] = lx_ref[0]
    cw = cw_ref[...]
    halo0 = SUBLANES - (CONV_W - 1)
    xb = cw[CONV_W - 1:CONV_W, :] * xs_ref[SUBLANES:SUBLANES + tl, :] + cb_ref[...]
    for j in range(CONV_W - 1):
        xb = xb + cw[j:j + 1, :] * xs_ref[halo0 + j:halo0 + j + tl, :]
    a, b = _lru_gates(xb, wa_ref, ba_ref, wi_ref, bi_ref, lam_ref)
    a_ref[...] = a
    b_ref[...] = b

    def body(t, h):
        h = a_ref[pl.ds(t, 1), :] * h + b_ref[pl.ds(t, 1), :]
        hs_ref[pl.ds(t, 1), :] = h
        return h

    h = lax.fori_loop(0, tl, body, h_ref[...])
    h_ref[...] = h
    ob_ref[0] = (hs_ref[...] * jax.nn.gelu(ly_ref[0])).astype(ob_ref.dtype)

    @pl.when(i == pl.num_programs(1) - 1)
    def _():
        ht_ref[0] = h


def _lru_prompt(ya3, buf8, h0, cw, cb, wa, ba, wi, bi, lam, *, n_seq, seq_rows, tl):
    nt = seq_rows // tl
    w = LRU_WIDTH

    def spec(col_block):
        return pl.BlockSpec((1, tl, w), lambda b, i: (0, b * nt + i, col_block))

    vec = pl.BlockSpec((1, w), lambda b, i: (0, 0))
    blk = pl.BlockSpec((LRU_BLOCKS, LRU_BW, LRU_BW), lambda b, i: (0, 0, 0))
    est = 8 * tl * w * 4 + 4 * (tl + SUBLANES) * w * 4 + 4 * LRU_BLOCKS * LRU_BW * LRU_BW * 2
    return pl.pallas_call(
        functools.partial(_lru_prompt_kernel, tl=tl),
        grid=(n_seq, nt),
        in_specs=[
            spec(A_COL_LX // w), spec(A_COL_LY // w),
            pl.BlockSpec((1, SUBLANES, w), lambda b, i: (b, 0, 0)),
            pl.BlockSpec((1, 1, w), lambda b, i: (b, 0, 0)),
            pl.BlockSpec((CONV_W, w), lambda b, i: (0, 0)), vec, blk, vec, blk, vec, vec,
        ],
        out_specs=[
            pl.BlockSpec((1, tl, w), lambda b, i: (b, i, 0)),
            pl.BlockSpec((1, 1, w), lambda b, i: (b, 0, 0)),
        ],
        out_shape=[
            jax.ShapeDtypeStruct((n_seq, seq_rows, w), BF16),
            jax.ShapeDtypeStruct((n_seq, 1, w), F32),
        ],
        scratch_shapes=[
            pltpu.VMEM((tl + SUBLANES, w), F32), pltpu.VMEM((tl, w), F32), pltpu.VMEM((tl, w), F32),
            pltpu.VMEM((tl, w), F32), pltpu.VMEM((1, w), F32),
        ],
        compiler_params=_params(("parallel", "arbitrary"), est),
        name="lru_prompt",
    )(ya3, ya3, buf8, h0, cw, cb.reshape(1, w), wa, ba.reshape(1, w), wi, bi.reshape(1, w),
      lam.reshape(1, w))


def _lru_sample_kernel(lx_ref, ly_ref, buf_ref, h0_ref, cw_ref, cb_ref, wa_ref, ba_ref, wi_ref, bi_ref,
                       lam_ref, ob_ref, ht_ref, *, steps):
    cw = cw_ref[...]
    n_buf = CONV_W - 1

    def tap(t):
        return buf_ref[t] if t < n_buf else lx_ref[t - n_buf]

    h = h0_ref[...]
    for t in range(steps):
        xb = cb_ref[...] + cw[0:1, :] * tap(t)
        for j in range(1, CONV_W):
            xb = xb + cw[j:j + 1, :] * tap(t + j)
        a, b = _lru_gates(xb, wa_ref, ba_ref, wi_ref, bi_ref, lam_ref)
        h = a * h + b
        ob_ref[t] = (h * jax.nn.gelu(ly_ref[t])).astype(ob_ref.dtype)
    ht_ref[...] = h


def _lru_sample(lx_tm, ly_tm, buf_tm, h0, cw, cb, wa, ba, wi, bi, lam):
    steps, n, w = lx_tm.shape
    est = 16 * n * w * 4
    return pl.pallas_call(
        functools.partial(_lru_sample_kernel, steps=steps),
        out_shape=[jax.ShapeDtypeStruct((steps, n, w), BF16), jax.ShapeDtypeStruct((n, w), F32)],
        compiler_params=_params(None, est),
        name="lru_sample",
    )(lx_tm, ly_tm, buf_tm, h0, cw, cb.reshape(1, w), wa, ba.reshape(1, w), wi, bi.reshape(1, w),
      lam.reshape(1, w))


def _rope_lanes(x, cos, sin, lane):
    half = QK_ROPE // 2
    rot = jnp.where(lane % QK_ROPE < half, pltpu.roll(x, LANES - half, 1), pltpu.roll(x, half, 1))
    return x * cos + rot * sin


def _mla_prep_kernel(cq_ref, ckv_ref, kpe_ref, cos_ref, sin_ref, qg_ref, kg_ref,
                     cqn_ref, ckv_out_ref, ckv_bf_ref, kpe_out_ref, kpe_bf_ref):
    cqn_ref[...] = _rms(cq_ref[...], qg_ref[...]).astype(cqn_ref.dtype)
    ckv = _rms(ckv_ref[...], kg_ref[...])
    ckv_out_ref[...] = ckv
    ckv_bf_ref[...] = ckv.astype(ckv_bf_ref.dtype)
    x = kpe_ref[...]
    lane = lax.broadcasted_iota(jnp.int32, x.shape, 1)
    kpe = _rope_lanes(x, cos_ref[...], sin_ref[...], lane)
    kpe_out_ref[...] = kpe
    kpe_bf_ref[...] = kpe.astype(kpe_bf_ref.dtype)


def _mla_prep(yc, cos, sin, q_g, kv_g, *, tm):
    m = yc.shape[0]
    row = lambda width, col_block: pl.BlockSpec((tm, width), lambda i: (i, col_block))
    vec = pl.BlockSpec((1, Q_LORA), lambda i: (0, 0))
    est = 2 * tm * (3 * Q_LORA + 4 * LANES) * 4 * 3
    return pl.pallas_call(
        _mla_prep_kernel,
        grid=(m // tm,),
        in_specs=[row(Q_LORA, 0), row(KV_LORA, 1), row(LANES, C_COL_KPE // LANES), row(LANES, 0),
                  row(LANES, 0), vec, vec],
        out_specs=[row(Q_LORA, 0), row(KV_LORA, 0), row(KV_LORA, 0), row(LANES, 0), row(LANES, 0)],
        out_shape=[
            jax.ShapeDtypeStruct((m, Q_LORA), BF16), jax.ShapeDtypeStruct((m, KV_LORA), F32),
            jax.ShapeDtypeStruct((m, KV_LORA), BF16), jax.ShapeDtypeStruct((m, LANES), F32),
            jax.ShapeDtypeStruct((m, LANES), BF16),
        ],
        compiler_params=_params(("parallel",), est),
        name="mla_prep",
    )(yc, yc, yc, cos, sin, q_g.reshape(1, Q_LORA), kv_g.reshape(1, KV_LORA))


def _mm_qrope_kernel(a_ref, w_ref, cos_ref, sin_ref, o_ref, *, heads):
    acc = _dot(a_ref[...], w_ref[...])
    cos = cos_ref[...]
    sin = sin_ref[...]
    lane = lax.broadcasted_iota(jnp.int32, cos.shape, 1)
    for h in range(heads):
        c0 = h * Q_HEAD_COLS
        o_ref[:, c0:c0 + QK_NOPE] = (acc[:, c0:c0 + QK_NOPE] * MLA_SCALE).astype(o_ref.dtype)
        pe = _rope_lanes(acc[:, c0 + QK_NOPE:c0 + Q_HEAD_COLS], cos, sin, lane)
        o_ref[:, c0 + QK_NOPE:c0 + Q_HEAD_COLS] = (pe * MLA_SCALE).astype(o_ref.dtype)


def _mm_qrope(a, w, cos, sin, *, tm, tn):
    m, k = a.shape
    n = w.shape[1]
    est = 2 * (tm * k * 2 + k * tn * 2 + tm * tn * 2 + 2 * tm * LANES * 4) + 2 * tm * tn * 4
    return pl.pallas_call(
        functools.partial(_mm_qrope_kernel, heads=tn // Q_HEAD_COLS),
        grid=(n // tn, m // tm),
        in_specs=[
            pl.BlockSpec((tm, k), lambda j, i: (i, 0)), pl.BlockSpec((k, tn), lambda j, i: (0, j)),
            pl.BlockSpec((tm, LANES), lambda j, i: (i, 0)), pl.BlockSpec((tm, LANES), lambda j, i: (i, 0)),
        ],
        out_specs=pl.BlockSpec((tm, tn), lambda j, i: (i, j)),
        out_shape=jax.ShapeDtypeStruct((m, n), BF16),
        compiler_params=_params(("parallel", "parallel"), est),
        name="mm_qrope",
    )(a, w, cos, sin)


def _flash_kernel(q_ref, kv_ref, kpe_ref, o_ref, m_ref, l_ref, acc_ref, *, tq, heads):
    qi = pl.program_id(2)
    m_ref[...] = jnp.full(m_ref.shape, -jnp.inf, F32)
    l_ref[...] = jnp.zeros(l_ref.shape, F32)
    acc_ref[...] = jnp.zeros(acc_ref.shape, F32)
    reps = tq // LANES

    def tile(j, diagonal):
        k0 = pl.multiple_of(j * tq, tq)
        kp = kpe_ref[pl.ds(k0, tq), :]
        for hh in range(heads):
            c0 = hh * Q_HEAD_COLS
            kn = kv_ref[pl.ds(k0, tq), c0:c0 + QK_NOPE]
            v = kv_ref[pl.ds(k0, tq), c0 + QK_NOPE:c0 + Q_HEAD_COLS]
            k = jnp.concatenate([kn, kp], axis=-1)
            s = lax.dot_general(q_ref[:, c0:c0 + Q_HEAD_COLS], k, NT, preferred_element_type=F32)
            if diagonal:
                r = lax.broadcasted_iota(jnp.int32, (tq, tq), 0)
                c = lax.broadcasted_iota(jnp.int32, (tq, tq), 1)
                s = jnp.where(c <= r, s, -jnp.inf)
            m_old = m_ref[hh]
            m_new = jnp.maximum(m_old, jnp.max(s, axis=-1, keepdims=True))
            p = jnp.exp(s - jnp.concatenate([m_new] * reps, axis=-1))
            corr = jnp.exp(m_old - m_new)
            l_ref[hh] = l_ref[hh] * corr + jnp.sum(p, axis=-1, keepdims=True)
            acc_ref[hh] = acc_ref[hh] * corr + lax.dot_general(p.astype(BF16), v, NN,
                                                               preferred_element_type=F32)
            m_ref[hh] = m_new

    def body(j, carry):
        tile(j, False)
        return carry

    lax.fori_loop(0, qi, body, 0)
    tile(qi, True)
    for hh in range(heads):
        o_ref[:, hh * V_HEAD:(hh + 1) * V_HEAD] = (acc_ref[hh] / l_ref[hh]).astype(o_ref.dtype)


def _flash(q, kv, kpe_bf, *, n_seq, seq_len, tq, heads):
    nq = seq_len // tq
    qw = heads * Q_HEAD_COLS
    est = 2 * (tq * qw * 2 + seq_len * qw * 2 + seq_len * LANES * 2 + tq * heads * V_HEAD * 2)
    est += 6 * heads * tq * tq * 4
    return pl.pallas_call(
        functools.partial(_flash_kernel, tq=tq, heads=heads),
        grid=(n_seq, MLA_HEADS // heads, nq),
        in_specs=[
            pl.BlockSpec((tq, qw), lambda b, h, i: (b * nq + i, h)),
            pl.BlockSpec((seq_len, qw), lambda b, h, i: (b, h)),
            pl.BlockSpec((seq_len, LANES), lambda b, h, i: (b, 0)),
        ],
        out_specs=pl.BlockSpec((tq, heads * V_HEAD), lambda b, h, i: (b * nq + i, h)),
        out_shape=jax.ShapeDtypeStruct((n_seq * seq_len, MLA_HEADS * V_HEAD), BF16),
        scratch_shapes=[pltpu.VMEM((heads, tq, LANES), F32), pltpu.VMEM((heads, tq, LANES), F32),
                        pltpu.VMEM((heads, tq, V_HEAD), F32)],
        compiler_params=_params(("parallel", "parallel", "arbitrary"), est),
        name="flash_prompt",
    )(q, kv, kpe_bf)


def _head_mm_kernel(a_ref, w_ref, o_ref):
    o_ref[...] = _dot(a_ref[...], w_ref[0]).astype(o_ref.dtype)


def _head_mm(a, w, *, a_width, a_col_stride, out_dtype, name):
    m = a.shape[0]
    heads, k, n = w.shape
    est = 2 * (m * a_width * 2 + k * n * 2 + m * n * 4) + m * n * 4
    return pl.pallas_call(
        _head_mm_kernel,
        grid=(heads,),
        in_specs=[
            pl.BlockSpec((m, a_width), lambda h: (0, h * a_col_stride)),
            pl.BlockSpec((1, k, n), lambda h: (h, 0, 0)),
        ],
        out_specs=pl.BlockSpec((m, n), lambda h: (0, h)),
        out_shape=jax.ShapeDtypeStruct((m, heads * n), out_dtype),
        compiler_params=_params(("parallel",), est),
        name=name,
    )(a, w)


def _paged_kernel(pt_ref, qlat_ref, qpe_ref, ckn_ref, kpn_ref, *rest, pages, seqs, t_new):
    n_in = seqs * pages
    ck_refs = rest[:n_in]
    kp_refs = rest[n_in:2 * n_in]
    o_ref, m_ref, l_ref, acc_ref = rest[2 * n_in:]
    step = pl.program_id(1)

    @pl.when(step == 0)
    def _():
        m_ref[...] = jnp.full(m_ref.shape, -jnp.inf, F32)
        l_ref[...] = jnp.zeros(l_ref.shape, F32)
        acc_ref[...] = jnp.zeros(acc_ref.shape, F32)

    def update(q, s, ck):
        m_old = m_ref[q]
        m_new = jnp.maximum(m_old, jnp.max(s, axis=-1, keepdims=True))
        p = jnp.exp(s - m_new)
        corr = jnp.exp(m_old - m_new)
        l_ref[q] = l_ref[q] * corr + jnp.sum(p, axis=-1, keepdims=True)
        acc_ref[q] = acc_ref[q] * corr + lax.dot_general(p.astype(BF16), ck, NN, preferred_element_type=F32)
        m_ref[q] = m_new

    for q in range(seqs):
        ql = qlat_ref[q]
        qp = qpe_ref[q][:, :QK_ROPE]
        ck = jnp.concatenate([r[0, 0].astype(BF16) for r in ck_refs[q * pages:(q + 1) * pages]], axis=0)
        kpt = jnp.concatenate([r[0, 0].astype(BF16) for r in kp_refs[q * pages:(q + 1) * pages]], axis=1)
        s = lax.dot_general(ql, ck, NT, preferred_element_type=F32)
        s = s + lax.dot_general(qp, kpt, NN, preferred_element_type=F32)
        update(q, s, ck)

    @pl.when(step == pl.num_programs(1) - 1)
    def _():
        for q in range(seqs):
            ql = qlat_ref[q]
            qp = qpe_ref[q][:, :QK_ROPE]
            rows = ql.shape[0]
            pad = PAGE_SIZE - ckn_ref.shape[1]
            ckn = jnp.concatenate([ckn_ref[q], jnp.zeros((pad, KV_LORA), F32)], axis=0).astype(BF16)
            sn = lax.dot_general(ql, ckn, NT, preferred_element_type=F32)
            sn = sn + lax.dot_general(qp, kpn_ref[q].astype(BF16), NN, preferred_element_type=F32)
            t_row = lax.broadcasted_iota(jnp.int32, (rows, PAGE_SIZE), 0) // MLA_HEADS
            j_col = lax.broadcasted_iota(jnp.int32, (rows, PAGE_SIZE), 1)
            sn = jnp.where(j_col <= jnp.minimum(t_row, t_new - 1), sn, -jnp.inf)
            update(q, sn, ckn)
            o_ref[q] = (acc_ref[q] / l_ref[q]).astype(o_ref.dtype)


def _paged(page_table, qlat, qpe, ckv_new8, kpe_new_t, cache_ckv, cache_kpe_t, *, layer, t_new):
    n_seq, rows, _ = qlat.shape
    n_pages = page_table.shape[1]
    pages, seqs = PAGES_PER_STEP, SEQS_PER_STEP
    steps = n_pages // pages
    pt = page_table.reshape(-1).astype(jnp.int32)
    t_pad = ckv_new8.shape[1]

    def page_spec(shape, q, k):
        return pl.BlockSpec((1, 1) + shape,
                            lambda b, s, pt_ref: (layer, pt_ref[(b * seqs + q) * n_pages + s * pages + k], 0, 0))

    per_seq = lambda d1, d2: pl.BlockSpec((seqs, d1, d2), lambda b, s, pt_ref: (b, 0, 0))
    in_specs = [per_seq(rows, KV_LORA), per_seq(rows, LANES), per_seq(t_pad, KV_LORA), per_seq(QK_ROPE, PAGE_SIZE)]
    in_specs += [page_spec((PAGE_SIZE, KV_LORA), q, k) for q in range(seqs) for k in range(pages)]
    in_specs += [page_spec((QK_ROPE, PAGE_SIZE), q, k) for q in range(seqs) for k in range(pages)]
    n_in = seqs * pages
    est = 2 * n_in * PAGE_SIZE * (KV_LORA + QK_ROPE) * 4 + 3 * n_in * PAGE_SIZE * (KV_LORA + QK_ROPE) * 2
    est += 8 * seqs * rows * (KV_LORA + pages * PAGE_SIZE) * 4
    grid_spec = pltpu.PrefetchScalarGridSpec(
        num_scalar_prefetch=1,
        grid=(n_seq // seqs, steps),
        in_specs=in_specs,
        out_specs=per_seq(rows, KV_LORA),
        scratch_shapes=[pltpu.VMEM((seqs, rows, 1), F32), pltpu.VMEM((seqs, rows, 1), F32),
                        pltpu.VMEM((seqs, rows, KV_LORA), F32)],
    )
    return pl.pallas_call(
        functools.partial(_paged_kernel, pages=pages, seqs=seqs, t_new=t_new),
        grid_spec=grid_spec,
        out_shape=jax.ShapeDtypeStruct((n_seq, rows, KV_LORA), BF16),
        compiler_params=_params(("parallel", "arbitrary"), est),
        name="paged_attn",
    )(pt, qlat, qpe, ckv_new8, kpe_new_t, *([cache_ckv] * n_in), *([cache_kpe_t] * n_in))


def _pad_rows(x, rows_before, rows_total):
    pad = [(0, 0)] * x.ndim
    pad[1] = (rows_before, rows_total - rows_before - x.shape[1])
    return jnp.pad(x, pad)


def _conv_state(buf, x):
    keep = CONV_W - 1
    if x.shape[1] >= keep:
        return x[:, x.shape[1] - keep:]
    return jnp.concatenate([buf.astype(x.dtype), x], axis=1)[:, -keep:]


def _rope_tables(positions):
    half = QK_ROPE // 2
    inv = ROPE_BASE ** (-jnp.arange(half, dtype=F32) / half)
    ang = positions.astype(F32)[:, None] * inv[None, :]
    cos, sin = jnp.cos(ang), jnp.sin(ang)
    zeros = jnp.zeros((positions.shape[0], LANES - QK_ROPE), F32)
    return (jnp.concatenate([cos, cos, zeros], axis=-1), jnp.concatenate([-sin, sin, zeros], axis=-1))


def kernel(x_prompt, x_sample, p_prompt, p_sample, state_gdn, state_gdn_conv, state_lru, state_lru_conv, cache_ckv, cache_kpe, page_table, w_in_a, conv_qkv_w, gdn_a_log, gdn_dt_bias, gdn_norm_g, conv_lru_w, conv_lru_b, lru_wa, lru_ba, lru_wi, lru_bi, lru_lambda, w_out_a, w_in_c, q_norm_g, kv_norm_g, w_uq, w_ukv, w_o_c, norm_mix_g, norm_ffn_g, w_up, w_down, w_ple, w_ple_gate, norm_final_g):
    bp, seq, d = x_prompt.shape
    bs, ts, _ = x_sample.shape
    mp, ms = bp * seq, bs * ts
    m = mp + ms
    depth = norm_mix_g.shape[0]
    past = page_table.shape[1] * PAGE_SIZE
    tm = 512

    h = jnp.concatenate([x_prompt.reshape(mp, d), x_sample.reshape(ms, d)], axis=0)
    pos = jnp.concatenate([jnp.tile(jnp.arange(seq), bp), jnp.tile(past + jnp.arange(ts), bs)])
    cos, sin = _rope_tables(pos)
    hn = _rmsnorm(h, norm_mix_g[0], tm=tm, out_dtype=BF16)

    gdn_p, gdn_s, gconv_p, gconv_s, lru_p, lru_s, lconv_p, lconv_s = [], [], [], [], [], [], [], []
    ckv_p, ckv_s, kpe_p, kpe_s = [], [], [], []

    for i in range(depth):
        j = i // 2
        if i % 2 == 0:
            wa = w_in_a[j]
            cut_b = GDN_QKV + GDN_WIDTH
            cut_lx = cut_b + 2 * GDN_HEADS
            w_a = jnp.concatenate([
                wa[:, :cut_b], wa[:, cut_lx:], wa[:, cut_b:cut_lx],
                jnp.zeros((d, LANES - 2 * GDN_HEADS), F32)], axis=1).astype(BF16)
            ya = _mm(hn, w_a, tm=tm, tn=A_COLS // 7, out_dtype=F32, name="in_proj_a")
            ya_p = ya.reshape(1, m, A_COLS)
            ya_s = ya[mp:].reshape(bs, ts, A_COLS)

            zero_buf = jnp.zeros((bp, SUBLANES, GDN_QKV), F32)
            parts_p = _gdn_intra(ya_p, zero_buf, conv_qkv_w[j], gdn_a_log[j], gdn_dt_bias[j], flat=True,
                                 n_groups=bp, nt=seq // GDN_CHUNK, bb=1, tl=GDN_CHUNK, l_real=GDN_CHUNK)
            o_gp, s_gp = _gdn_state(*parts_p, jnp.zeros((bp,) + state_gdn.shape[2:], F32), gdn_norm_g[j],
                                    nb=bp, nseg=1, tl=GDN_CHUNK)
            o_lp, h_lp = _lru_prompt(ya_p, jnp.zeros((bp, SUBLANES, LRU_WIDTH), F32),
                                     jnp.zeros((bp, 1, LRU_WIDTH), F32), conv_lru_w[j], conv_lru_b[j],
                                     lru_wa[j].astype(BF16), lru_ba[j].reshape(-1), lru_wi[j].astype(BF16),
                                     lru_bi[j].reshape(-1), lru_lambda[j], n_seq=bp, seq_rows=seq, tl=256)

            t_pad = SUBLANES
            ya_s8 = _pad_rows(ya_s, 0, t_pad)
            buf_g8 = _pad_rows(state_gdn_conv[j], SUBLANES - (CONV_W - 1), SUBLANES)
            seg = GDN_ROWS // t_pad
            parts_s = _gdn_intra(ya_s8, buf_g8, conv_qkv_w[j], gdn_a_log[j], gdn_dt_bias[j], flat=False,
                                 n_groups=bs // seg, nt=1, bb=seg, tl=t_pad, l_real=ts)
            o_gs, s_gs = _gdn_state(*parts_s, state_gdn[j], gdn_norm_g[j], nb=1, nseg=seg, tl=t_pad)
            o_gs = o_gs.reshape(bs, t_pad, GDN_WIDTH)
            lx_tm = ya_s[:, :, A_COL_LX:A_COL_LY].transpose(1, 0, 2)
            ly_tm = ya_s[:, :, A_COL_LY:A_COL_BA].transpose(1, 0, 2)
            o_ls_tm, h_ls = _lru_sample(lx_tm, ly_tm, state_lru_conv[j].transpose(1, 0, 2), state_lru[j],
                                        conv_lru_w[j], conv_lru_b[j], lru_wa[j].astype(BF16),
                                        lru_ba[j].reshape(-1), lru_wi[j].astype(BF16), lru_bi[j].reshape(-1),
                                        lru_lambda[j])
            mix = jnp.concatenate([
                jnp.concatenate([o_gp.reshape(mp, GDN_WIDTH), o_lp.reshape(mp, LRU_WIDTH)], axis=1),
                jnp.concatenate([o_gs[:, :ts].reshape(ms, GDN_WIDTH),
                                 o_ls_tm.transpose(1, 0, 2).reshape(ms, LRU_WIDTH)], axis=1)], axis=0)
            h = _mm(mix, w_out_a, layer=j, tm=tm, tn=1024, out_dtype=F32, epilogue="residual", residual=h,
                    name="out_proj_a")

            keep = CONV_W - 1
            ya_pp = jnp.stack([ya[(b + 1) * seq - keep:(b + 1) * seq] for b in range(bp)])
            gdn_p.append(s_gp)
            gdn_s.append(s_gs)
            gconv_p.append(ya_pp[:, :, :GDN_QKV])
            gconv_s.append(_conv_state(state_gdn_conv[j], ya_s[:, :, :GDN_QKV]))
            lru_p.append(h_lp.reshape(bp, LRU_WIDTH))
            lru_s.append(h_ls)
            lconv_p.append(ya_pp[:, :, A_COL_LX:A_COL_LY])
            lconv_s.append(_conv_state(state_lru_conv[j], ya_s[:, :, A_COL_LX:A_COL_LY]))
        else:
            w_c = jnp.concatenate([w_in_c[j], jnp.zeros((d, LANES - QK_ROPE), F32)], axis=1).astype(BF16)
            yc = _mm(hn, w_c, tm=tm, tn=C_COLS // 3, out_dtype=F32, name="in_proj_c")
            cqn, ckv, ckv_bf, kpe, kpe_bf = _mla_prep(yc, cos, sin, q_norm_g[j], kv_norm_g[j], tm=tm)
            w_q = w_uq[j].reshape(Q_LORA, MLA_HEADS, QK_NOPE + QK_ROPE)
            w_q = jnp.pad(w_q, ((0, 0), (0, 0), (0, Q_HEAD_COLS - QK_NOPE - QK_ROPE)))
            q = _mm_qrope(cqn, w_q.reshape(Q_LORA, MLA_HEADS * Q_HEAD_COLS).astype(BF16), cos, sin,
                          tm=tm, tn=1024)
            w_r = w_ukv[j].reshape(KV_LORA, MLA_HEADS, QK_NOPE + V_HEAD)

            kv = _mm(ckv_bf[:mp], w_ukv, layer=j, tm=tm, tn=1024, out_dtype=BF16, name="kv_up")
            o_p = _flash(q, kv, kpe_bf, n_seq=bp, seq_len=seq, tq=512, heads=2)

            q_s = q[mp:]
            w_uk_t = w_r[:, :, :QK_NOPE].transpose(1, 2, 0).astype(BF16)
            w_uv = w_r[:, :, QK_NOPE:].transpose(1, 0, 2).astype(BF16)
            qlat = _head_mm(q_s, w_uk_t, a_width=QK_NOPE, a_col_stride=Q_HEAD_COLS // QK_NOPE,
                            out_dtype=BF16, name="q_latent")
            qlat = qlat.reshape(bs, ts * MLA_HEADS, KV_LORA)
            qpe = q_s.reshape(ms, MLA_HEADS, Q_HEAD_COLS)[:, :, QK_NOPE:].reshape(bs, ts * MLA_HEADS, LANES)
            ckv_new8 = _pad_rows(ckv[mp:].reshape(bs, ts, KV_LORA), 0, SUBLANES)
            kpe_new_t = kpe[mp:, :QK_ROPE].reshape(bs, ts, QK_ROPE).transpose(0, 2, 1)
            kpe_new_t = jnp.pad(kpe_new_t, ((0, 0), (0, 0), (0, PAGE_SIZE - ts)))
            lat = _paged(page_table, qlat, qpe, ckv_new8, kpe_new_t, cache_ckv,
                         cache_kpe.transpose(0, 1, 3, 2), layer=j, t_new=ts)
            o_s = _head_mm(lat.reshape(ms, MLA_HEADS * KV_LORA), w_uv, a_width=KV_LORA, a_col_stride=1,
                           out_dtype=BF16, name="v_up")
            h = _mm(jnp.concatenate([o_p, o_s], axis=0), w_o_c, layer=j, tm=tm, tn=1024, out_dtype=F32,
                    epilogue="residual", residual=h, name="out_proj_c")
            ckv_p.append(ckv[:mp].reshape(bp, seq, KV_LORA))
            ckv_s.append(ckv[mp:].reshape(bs, ts, KV_LORA))
            kpe_p.append(kpe[:mp, :QK_ROPE].reshape(bp, seq, QK_ROPE))
            kpe_s.append(kpe[mp:, :QK_ROPE].reshape(bs, ts, QK_ROPE))

        hn = _rmsnorm(h, norm_ffn_g[i], tm=tm, out_dtype=BF16)
        ff = _mm(hn, w_up, layer=i, tm=tm, tn=1024, out_dtype=BF16, epilogue="relu2", name="ffn_up")
        h = _mm(ff, w_down, layer=i, single_buffer_w=True, tm=tm, tn=512, out_dtype=F32, epilogue="residual",
                residual=h, name="ffn_down")
        p = jnp.concatenate([p_prompt[i].reshape(mp, -1), p_sample[i].reshape(ms, -1)], axis=0)
        last = i + 1 == depth
        g_next = norm_final_g if last else norm_mix_g[i + 1]
        h, hn = _ple(h, p, w_ple, w_ple_gate, g_next, layer=i, tm=256,
                     norm_dtype=F32 if last else BF16)

    y = hn
    return (y[:mp].reshape(bp, seq, d), y[mp:].reshape(bs, ts, d),
            jnp.stack(gdn_p), jnp.stack(gdn_s), jnp.stack(gconv_p), jnp.stack(gconv_s),
            jnp.stack(lru_p), jnp.stack(lru_s), jnp.stack(lconv_p), jnp.stack(lconv_s),
            jnp.stack(ckv_p), jnp.stack(ckv_s), jnp.stack(kpe_p), jnp.stack(kpe_s))
```

```python
import functools
import math

import jax
import jax.numpy as jnp
from jax import lax
from jax.experimental import pallas as pl
from jax.experimental.pallas import tpu as pltpu

F32 = jnp.float32
BF16 = jnp.bfloat16

D_MODEL = 2048
CONV_W = 4
GDN_HEADS = 8
GDN_DK = 128
GDN_DV = 128
GDN_QK = GDN_HEADS * GDN_DK
GDN_WIDTH = GDN_HEADS * GDN_DV
GDN_QKV = 2 * GDN_QK + GDN_WIDTH
GDN_CHUNK = 64
LRU_WIDTH = D_MODEL // 2
LRU_BLOCKS = 8
LRU_BW = LRU_WIDTH // LRU_BLOCKS
LRU_C = 8.0
MLA_HEADS = 16
Q_LORA = 512
KV_LORA = 512
QK_NOPE = 128
QK_ROPE = 64
V_HEAD = 128
MLA_SCALE = (QK_NOPE + QK_ROPE) ** -0.5
ROPE_BASE = 10000.0
PAGE_SIZE = 128
EPS = 1e-6

LANES = 128
SUBLANES = 8
VMEM_LIMIT_BYTES = 56 * 1024 * 1024

A_COL_Z = GDN_QKV
A_COL_LX = A_COL_Z + GDN_WIDTH
A_COL_LY = A_COL_LX + LRU_WIDTH
A_COL_BA = A_COL_LY + LRU_WIDTH
A_COLS = A_COL_BA + LANES
C_COL_KPE = Q_LORA + KV_LORA
C_COLS = C_COL_KPE + LANES
Q_HEAD_COLS = 2 * LANES
PAGES_PER_STEP = 8
SEQS_PER_STEP = 4
GDN_ROWS = 64

NN = (((1,), (0,)), ((), ()))
NT = (((1,), (1,)), ((), ()))
TN = (((0,), (0,)), ((), ()))


def _params(semantics, est_bytes):
    limit = int(min(max(2 * est_bytes, 32 * 1024 * 1024), VMEM_LIMIT_BYTES))
    return pltpu.CompilerParams(dimension_semantics=semantics, vmem_limit_bytes=limit)


def _vmem_params(semantics, need_bytes):
    limit = int(min(max(need_bytes + 4 * 1024 * 1024, 32 * 1024 * 1024), VMEM_LIMIT_BYTES))
    return pltpu.CompilerParams(dimension_semantics=semantics, vmem_limit_bytes=limit)


def _nbytes(shape, dtype):
    return math.prod(shape) * jnp.dtype(dtype).itemsize


def _dot(a, b, dims=NN):
    return lax.dot_general(a.astype(BF16), b.astype(BF16), dims, preferred_element_type=F32)


def _dot_hi(a, b, dims=NN):
    return lax.dot_general(a, b, dims, precision=lax.Precision.HIGHEST, preferred_element_type=F32)


def _sigmoid(x):
    return jax.nn.sigmoid(x)


def _softplus(x):
    return jnp.maximum(x, 0.0) + jnp.log1p(jnp.exp(-jnp.abs(x)))


def _rms(x, g):
    return x * lax.rsqrt(jnp.mean(x * x, axis=-1, keepdims=True) + EPS) * g


def _rmsnorm_kernel(x_ref, g_ref, o_ref):
    o_ref[...] = _rms(x_ref[...], g_ref[...]).astype(o_ref.dtype)


def _rmsnorm(x, g, *, tm, out_dtype):
    m, d = x.shape
    est = 2 * (_nbytes((tm, d), F32) + _nbytes((tm, d), out_dtype)) + 2 * _nbytes((tm, d), F32)
    return pl.pallas_call(
        _rmsnorm_kernel,
        grid=(m // tm,),
        in_specs=[pl.BlockSpec((tm, d), lambda i: (i, 0)), pl.BlockSpec((1, d), lambda i: (0, 0))],
        out_specs=pl.BlockSpec((tm, d), lambda i: (i, 0)),
        out_shape=jax.ShapeDtypeStruct((m, d), out_dtype),
        compiler_params=_params(("parallel",), est),
        name="rmsnorm",
    )(x, g.reshape(1, d))


def _mm_kernel(a_ref, w_ref, *rest, epilogue, cast_w):
    if cast_w:
        *rest, wbf_ref = rest

        @pl.when(pl.program_id(1) == 0)
        def _():
            wbf_ref[...] = w_ref[0].astype(BF16)

        w = wbf_ref[...]
    else:
        w = w_ref[...]
    acc = _dot(a_ref[...], w)
    if epilogue == "relu2":
        acc = jnp.square(jnp.maximum(acc, 0.0))
    elif epilogue in ("residual", "residual_norm"):
        acc = acc + rest[0][...]
    if epilogue == "residual_norm":
        _, g_ref, o_ref, n_ref = rest
        n_ref[...] = _rms(acc, g_ref[...]).astype(n_ref.dtype)
    else:
        o_ref = rest[-1]
    o_ref[...] = acc.astype(o_ref.dtype)


def _mm(a, w, *, tm, tn, out_dtype, epilogue=None, residual=None, norm_g=None, layer=None,
        single_buffer_w=False, name):
    m, k = a.shape
    n = w.shape[-1]
    cast_w = layer is not None
    if cast_w:
        mode = dict(pipeline_mode=pl.Buffered(1)) if single_buffer_w else {}
        w_spec = pl.BlockSpec((1, k, tn), lambda j, i: (layer, 0, j), **mode)
        w_bytes = (1 if single_buffer_w else 2) * _nbytes((k, tn), F32) + _nbytes((k, tn), BF16)
        scratch = [pltpu.VMEM((k, tn), BF16)]
    else:
        w_spec = pl.BlockSpec((k, tn), lambda j, i: (0, j))
        w_bytes = 2 * _nbytes((k, tn), w.dtype)
        scratch = []
    tile = pl.BlockSpec((tm, tn), lambda j, i: (i, j))
    in_specs = [pl.BlockSpec((tm, k), lambda j, i: (i, 0)), w_spec]
    args = [a, w]
    out_specs, out_shape = tile, jax.ShapeDtypeStruct((m, n), out_dtype)
    est = w_bytes + 2 * (_nbytes((tm, k), a.dtype) + _nbytes((tm, tn), out_dtype)) + 2 * _nbytes((tm, tn), F32)
    if epilogue in ("residual", "residual_norm"):
        in_specs.append(tile)
        args.append(residual)
        est += 2 * _nbytes((tm, tn), F32)
    if epilogue == "residual_norm":
        assert tn == n
        in_specs.append(pl.BlockSpec((1, n), lambda j, i: (0, 0)))
        args.append(norm_g.reshape(1, n))
        out_specs = [tile, tile]
        out_shape = [out_shape, jax.ShapeDtypeStruct((m, n), BF16)]
        est += 2 * _nbytes((tm, tn), BF16) + _nbytes((tm, tn), F32)
    return pl.pallas_call(
        functools.partial(_mm_kernel, epilogue=epilogue, cast_w=cast_w),
        grid=(n // tn, m // tm),
        in_specs=in_specs,
        out_specs=out_specs,
        out_shape=out_shape,
        scratch_shapes=scratch,
        compiler_params=_vmem_params(("parallel", "arbitrary"), est),
        name=name,
    )(*args)


def _ple_kernel(h_ref, p_ref, wp_ref, wg_ref, g_ref, h_out_ref, n_out_ref, wp_bf_ref, wg_bf_ref):
    @pl.when(pl.program_id(0) == 0)
    def _():
        wp_bf_ref[...] = wp_ref[0].astype(BF16)
        wg_bf_ref[...] = wg_ref[0].astype(BF16)

    h = h_ref[...]
    gate = _sigmoid(_dot(h, wg_bf_ref[...]))
    h2 = h + _dot(p_ref[...], wp_bf_ref[...]) * gate
    h_out_ref[...] = h2
    n_out_ref[...] = _rms(h2, g_ref[...]).astype(n_out_ref.dtype)


def _ple(h, p, wp, wg, g, *, layer, tm, norm_dtype):
    m, d = h.shape
    pd = p.shape[1]
    est = (_nbytes((pd, d), F32) + _nbytes((d, d), F32) + _nbytes((pd, d), BF16) + _nbytes((d, d), BF16)
           + 2 * (2 * _nbytes((tm, d), F32) + _nbytes((tm, pd), F32) + _nbytes((tm, d), norm_dtype))
           + 3 * _nbytes((tm, d), F32))
    once = dict(pipeline_mode=pl.Buffered(1))
    return pl.pallas_call(
        _ple_kernel,
        grid=(m // tm,),
        in_specs=[
            pl.BlockSpec((tm, d), lambda i: (i, 0)),
            pl.BlockSpec((tm, pd), lambda i: (i, 0)),
            pl.BlockSpec((1, pd, d), lambda i: (layer, 0, 0), **once),
            pl.BlockSpec((1, d, d), lambda i: (layer, 0, 0), **once),
            pl.BlockSpec((1, d), lambda i: (0, 0)),
        ],
        out_specs=[pl.BlockSpec((tm, d), lambda i: (i, 0)), pl.BlockSpec((tm, d), lambda i: (i, 0))],
        out_shape=[jax.ShapeDtypeStruct((m, d), F32), jax.ShapeDtypeStruct((m, d), norm_dtype)],
        scratch_shapes=[pltpu.VMEM((pd, d), BF16), pltpu.VMEM((d, d), BF16)],
        compiler_params=_vmem_params(("arbitrary",), est),
        name="ple",
    )(h, p, wp, wg, g.reshape(1, d))


def _gdn_intra_kernel(qkv_ref, prev_ref, ba_ref, z_ref, buf_ref, cw_ref, alog_ref, dtb_ref,
                      u_ref, w_ref, qg_ref, kd_ref, qk_ref, el_ref, zs_ref, xs_ref, *, bb, tl, l_real):
    rows = bb * tl
    i = pl.program_id(1)
    xs_ref[:, 0:SUBLANES, :] = jnp.where(i == 0, buf_ref[...], prev_ref[...])
    x_in = qkv_ref[...]
    xs_ref[:, SUBLANES:SUBLANES + tl, :] = x_in
    cw = cw_ref[...]
    halo0 = SUBLANES - (CONV_W - 1)
    y = cw[CONV_W - 1:CONV_W, :] * x_in
    for j in range(CONV_W - 1):
        y = y + cw[j:j + 1, :] * xs_ref[:, halo0 + j:halo0 + j + tl, :]
    y = (y * _sigmoid(y)).reshape(rows, GDN_QKV)
    ba = ba_ref[...].reshape(rows, LANES)
    z = z_ref[...].reshape(rows, GDN_WIDTH)
    zs_ref[0] = (z * _sigmoid(z)).astype(zs_ref.dtype)

    shift = int(math.log2(tl))
    ri = lax.broadcasted_iota(jnp.int32, (rows, rows), 0)
    ci = lax.broadcasted_iota(jnp.int32, (rows, rows), 1)
    if bb == 1:
        lower = ri >= ci
        strict = ri > ci
    else:
        same = (ri >> shift) == (ci >> shift)
        lower = same & (ri >= ci)
        strict = same & (ri > ci)
    tril = lower.astype(F32)
    seg_last = (ci == ((ri >> shift) << shift) + (tl - 1)).astype(F32)
    lane = lax.broadcasted_iota(jnp.int32, (rows, LANES), 1)
    beta_all = _sigmoid(ba)
    g_all = -jnp.exp(alog_ref[...]) * _softplus(ba + dtb_ref[...])
    g_all = jnp.where(lane >= GDN_HEADS, g_all, 0.0)
    if l_real < tl:
        r1 = lax.broadcasted_iota(jnp.int32, (rows, 1), 0)
        rowmask = ((r1 & (tl - 1)) < l_real).astype(F32)
        y = y * rowmask
        beta_all = beta_all * rowmask
        g_all = g_all * rowmask
    gc_all = _dot_hi(tril, g_all)
    gl_all = _dot_hi(seg_last, gc_all)
    el_ref[0] = jnp.exp(gl_all)
    eg_all = jnp.exp(gc_all)
    kdf_all = jnp.exp(gl_all - gc_all)
    gct = jnp.concatenate([gc_all, jnp.zeros((LANES - rows, LANES), F32)], axis=0).T
    n_factors = max(1, math.ceil(math.log2(l_real)))

    heads = range(GDN_HEADS)
    lanes = [GDN_HEADS + h for h in heads]
    hsl = [slice(h * GDN_DV, (h + 1) * GDN_DV) for h in heads]
    qs = [y[:, h * GDN_DK:(h + 1) * GDN_DK] for h in heads]
    ks = [y[:, GDN_QK + h * GDN_DK:GDN_QK + (h + 1) * GDN_DK] for h in heads]
    vs = [y[:, 2 * GDN_QK + h * GDN_DV:2 * GDN_QK + (h + 1) * GDN_DV] for h in heads]
    qs = [q * lax.rsqrt(jnp.sum(q * q, axis=-1, keepdims=True) + EPS) * (GDN_DK ** -0.5) for q in qs]
    ks = [k * lax.rsqrt(jnp.sum(k * k, axis=-1, keepdims=True) + EPS) for k in ks]
    decays = [jnp.where(lower, jnp.exp(gc_all[:, lh:lh + 1] - gct[lh:lh + 1, 0:rows]), 0.0) for lh in lanes]
    kbs = [ks[h] * beta_all[:, h:h + 1] for h in heads]
    ps = [-jnp.where(strict, _dot(kbs[h], ks[h], NT) * decays[h], 0.0) for h in heads]
    xs = [jnp.concatenate([vs[h] * beta_all[:, h:h + 1], kbs[h] * eg_all[:, lanes[h]:lanes[h] + 1]], axis=-1)
          for h in heads]
    for h in heads:
        qk = jnp.where(lower, _dot(qs[h], ks[h], NT) * decays[h], 0.0)
        qk_ref[h, 0] = qk.astype(qk_ref.dtype)
        qg_ref[0, :, hsl[h]] = (qs[h] * eg_all[:, lanes[h]:lanes[h] + 1]).astype(qg_ref.dtype)
        kd_ref[0, :, hsl[h]] = (ks[h] * kdf_all[:, lanes[h]:lanes[h] + 1]).astype(kd_ref.dtype)
    for f in range(n_factors):
        xs = [xs[h] + _dot(ps[h], xs[h]) for h in heads]
        if f + 1 < n_factors:
            ps = [_dot(ps[h], ps[h]) for h in heads]
    for h in heads:
        u_ref[0, :, hsl[h]] = xs[h][:, :GDN_DV]
        w_ref[0, :, hsl[h]] = xs[h][:, GDN_DV:].astype(w_ref.dtype)


def _gdn_intra(ya3, buf8, conv_w, a_log, dt_bias, *, flat, n_groups, nt, bb, tl, l_real):
    rows = bb * tl
    assert rows == GDN_ROWS and (flat or nt == 1)

    def spec(width, col_block):
        if flat:
            return pl.BlockSpec((1, rows, width), lambda g, i: (0, g * nt + i, col_block))
        return pl.BlockSpec((bb, tl, width), lambda g, i: (g, 0, col_block))

    if flat:
        prev = pl.BlockSpec((1, SUBLANES, GDN_QKV),
                            lambda g, i: (0, jnp.maximum((g * nt + i) * (rows // SUBLANES) - 1, 0), 0))
    else:
        prev = spec(GDN_QKV, 0)
    lane_pad = jnp.zeros((1, LANES), F32)
    alog = lane_pad.at[0, GDN_HEADS:2 * GDN_HEADS].set(a_log)
    dtb = lane_pad.at[0, GDN_HEADS:2 * GDN_HEADS].set(dt_bias)
    total = nt * rows
    out = lambda width: pl.BlockSpec((1, rows, width), lambda g, i: (g, i, 0))
    shape = lambda width, dt: jax.ShapeDtypeStruct((n_groups, total, width), dt)
    est = 4 * rows * (2 * GDN_QKV + 6 * GDN_WIDTH) * 4
    return pl.pallas_call(
        functools.partial(_gdn_intra_kernel, bb=bb, tl=tl, l_real=l_real),
        grid=(n_groups, nt),
        in_specs=[
            spec(GDN_QKV, 0), prev, spec(LANES, A_COL_BA // LANES), spec(GDN_WIDTH, A_COL_Z // GDN_WIDTH),
            pl.BlockSpec((bb, SUBLANES, GDN_QKV), lambda g, i: (g, 0, 0)),
            pl.BlockSpec((CONV_W, GDN_QKV), lambda g, i: (0, 0)),
            pl.BlockSpec((1, LANES), lambda g, i: (0, 0)),
            pl.BlockSpec((1, LANES), lambda g, i: (0, 0)),
        ],
        out_specs=[
            out(GDN_WIDTH), out(GDN_WIDTH), out(GDN_WIDTH), out(GDN_WIDTH),
            pl.BlockSpec((GDN_HEADS, 1, rows, rows), lambda g, i: (0, g, i, 0)),
            out(LANES), out(GDN_WIDTH),
        ],
        out_shape=[
            shape(GDN_WIDTH, F32), shape(GDN_WIDTH, BF16), shape(GDN_WIDTH, BF16), shape(GDN_WIDTH, BF16),
            jax.ShapeDtypeStruct((GDN_HEADS, n_groups, total, rows), BF16),
            shape(LANES, F32), shape(GDN_WIDTH, BF16),
        ],
        scratch_shapes=[pltpu.VMEM((bb, tl + SUBLANES, GDN_QKV), F32)],
        compiler_params=_params(("parallel", "parallel"), est),
        name="gdn_intra",
    )(ya3, ya3, ya3, ya3, buf8, conv_w, alog, dtb)


def _gdn_state_kernel(u_ref, w_ref, qg_ref, kd_ref, qk_ref, el_ref, zs_ref, s0_ref, ng_ref,
                      o_ref, sout_ref, s_ref, *, nb, nseg, tl):
    i = pl.program_id(1)

    @pl.when(i == 0)
    def _():
        s_ref[...] = s0_ref[...]

    rows = nseg * tl
    seg_of_row = lax.broadcasted_iota(jnp.int32, (rows, 1), 0) >> int(math.log2(tl))
    pairs = [(blk, h) for blk in range(nb) for h in range(GDN_HEADS)]
    hsl = lambda h: slice(h * GDN_DV, (h + 1) * GDN_DV)
    segs = [slice(s * tl, (s + 1) * tl) for s in range(nseg)]
    cat = lambda parts: parts[0] if len(parts) == 1 else jnp.concatenate(parts, axis=0)
    inter = {}
    for blk, h in pairs:
        w = w_ref[blk, :, hsl(h)].astype(F32)
        qg = qg_ref[blk, :, hsl(h)].astype(F32)
        inter[blk, h] = [_dot(jnp.concatenate([w[rs], qg[rs]], axis=0), s_ref[blk * nseg + s, h])
                         for s, rs in enumerate(segs)]
    v_new = {}
    for blk, h in pairs:
        u = u_ref[blk, :, hsl(h)]
        v_new[blk, h] = cat([u[rs] - inter[blk, h][s][:tl] for s, rs in enumerate(segs)])
    for blk, h in pairs:
        o = cat([r[tl:] for r in inter[blk, h]]) + _dot(qk_ref[h, blk], v_new[blk, h])
        on = _rms(o, ng_ref[...]) * zs_ref[blk, :, hsl(h)].astype(F32)
        o_ref[blk, :, hsl(h)] = on.astype(o_ref.dtype)
    for blk, h in pairs:
        kd = kd_ref[blk, :, hsl(h)].astype(F32)
        lh = GDN_HEADS + h
        for s in range(nseg):
            e = el_ref[blk, s * tl:s * tl + 1, lh:lh + 1]
            kds = kd if nseg == 1 else jnp.where(seg_of_row == s, kd, 0.0)
            idx = blk * nseg + s
            s_ref[idx, h] = s_ref[idx, h] * e + _dot(kds, v_new[blk, h], TN)

    @pl.when(i == pl.num_programs(1) - 1)
    def _():
        sout_ref[...] = s_ref[...]


def _gdn_state(u, w, qg, kd, qk, el, zs, s0, norm_g, *, nb, nseg, tl):
    n_groups, total, _ = u.shape
    rows = nseg * tl
    assert rows == GDN_ROWS
    blk = lambda width: pl.BlockSpec((nb, rows, width), lambda g, i: (g, i, 0))
    st = pl.BlockSpec((nb * nseg, GDN_HEADS, GDN_DK, GDN_DV), lambda g, i: (g, 0, 0, 0))
    est = 2 * nb * rows * GDN_WIDTH * 16 + 5 * nb * nseg * GDN_HEADS * GDN_DK * GDN_DV * 4
    return pl.pallas_call(
        functools.partial(_gdn_state_kernel, nb=nb, nseg=nseg, tl=tl),
        grid=(n_groups // nb, total // rows),
        in_specs=[
            blk(GDN_WIDTH), blk(GDN_WIDTH), blk(GDN_WIDTH), blk(GDN_WIDTH),
            pl.BlockSpec((GDN_HEADS, nb, rows, rows), lambda g, i: (0, g, i, 0)),
            blk(LANES), blk(GDN_WIDTH), st,
            pl.BlockSpec((1, GDN_DV), lambda g, i: (0, 0)),
        ],
        out_specs=[blk(GDN_WIDTH), st],
        out_shape=[
            jax.ShapeDtypeStruct((n_groups, total, GDN_WIDTH), BF16),
            jax.ShapeDtypeStruct(s0.shape, F32),
        ],
        scratch_shapes=[pltpu.VMEM((nb * nseg, GDN_HEADS, GDN_DK, GDN_DV), F32)],
        compiler_params=_params(("parallel", "arbitrary"), est),
        name="gdn_state",
    )(u, w, qg, kd, qk, el, zs, s0, norm_g.reshape(1, GDN_DV))


def _lru_gates(xb, wa_ref, ba_ref, wi_ref, bi_ref, lam_ref):
    c = -LRU_C * _softplus(-lam_ref[...])
    sls = [slice(kb * LRU_BW, (kb + 1) * LRU_BW) for kb in range(LRU_BLOCKS)]
    xks = [xb[:, sl] for sl in sls]
    ra = [_dot(xks[kb], wa_ref[kb]) for kb in range(LRU_BLOCKS)]
    ri = [_dot(xks[kb], wi_ref[kb]) for kb in range(LRU_BLOCKS)]
    a_parts, b_parts = [], []
    for kb, sl in enumerate(sls):
        r = _sigmoid(ra[kb] + ba_ref[:, sl])
        ig = _sigmoid(ri[kb] + bi_ref[:, sl])
        log_a = c[:, sl] * r
        a_parts.append(jnp.exp(log_a))
        b_parts.append(jnp.sqrt(1.0 - jnp.exp(2.0 * log_a)) * (ig * xks[kb]))
    return jnp.concatenate(a_parts, axis=-1), jnp.concatenate(b_parts, axis=-1)


def _lru_prompt_kernel(lx_ref, ly_ref, buf_ref, h0_ref, cw_ref, cb_ref, wa_ref, ba_ref, wi_ref, bi_ref,
                       lam_ref, ob_ref, ht_ref, xs_ref, a_ref, b_ref, hs_ref, h_ref, *, tl):
    i = pl.program_id(1)

    @pl.when(i == 0)
    def _():
        h_ref[...] = h0_ref[0]
        xs_ref[0:SUBLANES, :] = buf_ref[0]

    @pl.when(i > 0)
    def _():
        xs_ref[0:SUBLANES, :] = xs_ref[tl:tl + SUBLANES, :]

    xs_ref[SUBLANES:SUBLANES + tl, :] = lx_ref[0]
    cw = cw_ref[...]
    halo0 = SUBLANES - (CONV_W - 1)
    xb = cw[CONV_W - 1:CONV_W, :] * xs_ref[SUBLANES:SUBLANES + tl, :] + cb_ref[...]
    for j in range(CONV_W - 1):
        xb = xb + cw[j:j + 1, :] * xs_ref[halo0 + j:halo0 + j + tl, :]
    a, b = _lru_gates(xb, wa_ref, ba_ref, wi_ref, bi_ref, lam_ref)
    a_ref[...] = a
    b_ref[...] = b

    def body(t, h):
        h = a_ref[pl.ds(t, 1), :] * h + b_ref[pl.ds(t, 1), :]
        hs_ref[pl.ds(t, 1), :] = h
        return h

    h = lax.fori_loop(0, tl, body, h_ref[...], unroll=8)
    h_ref[...] = h
    ob_ref[0] = (hs_ref[...] * jax.nn.gelu(ly_ref[0])).astype(ob_ref.dtype)

    @pl.when(i == pl.num_programs(1) - 1)
    def _():
        ht_ref[0] = h


def _lru_prompt(ya3, buf8, h0, cw, cb, wa, ba, wi, bi, lam, *, n_seq, seq_rows, tl):
    nt = seq_rows // tl
    w = LRU_WIDTH

    def spec(col_block):
        return pl.BlockSpec((1, tl, w), lambda b, i: (0, b * nt + i, col_block))

    vec = pl.BlockSpec((1, w), lambda b, i: (0, 0))
    blk = pl.BlockSpec((LRU_BLOCKS, LRU_BW, LRU_BW), lambda b, i: (0, 0, 0))
    est = 8 * tl * w * 4 + 4 * (tl + SUBLANES) * w * 4 + 4 * LRU_BLOCKS * LRU_BW * LRU_BW * 2
    return pl.pallas_call(
        functools.partial(_lru_prompt_kernel, tl=tl),
        grid=(n_seq, nt),
        in_specs=[
            spec(A_COL_LX // w), spec(A_COL_LY // w),
            pl.BlockSpec((1, SUBLANES, w), lambda b, i: (b, 0, 0)),
            pl.BlockSpec((1, 1, w), lambda b, i: (b, 0, 0)),
            pl.BlockSpec((CONV_W, w), lambda b, i: (0, 0)), vec, blk, vec, blk, vec, vec,
        ],
        out_specs=[
            pl.BlockSpec((1, tl, w), lambda b, i: (b, i, 0)),
            pl.BlockSpec((1, 1, w), lambda b, i: (b, 0, 0)),
        ],
        out_shape=[
            jax.ShapeDtypeStruct((n_seq, seq_rows, w), BF16),
            jax.ShapeDtypeStruct((n_seq, 1, w), F32),
        ],
        scratch_shapes=[
            pltpu.VMEM((tl + SUBLANES, w), F32), pltpu.VMEM((tl, w), F32), pltpu.VMEM((tl, w), F32),
            pltpu.VMEM((tl, w), F32), pltpu.VMEM((1, w), F32),
        ],
        compiler_params=_params(("parallel", "arbitrary"), est),
        name="lru_prompt",
    )(ya3, ya3, buf8, h0, cw, cb.reshape(1, w), wa, ba.reshape(1, w), wi, bi.reshape(1, w),
      lam.reshape(1, w))


def _lru_sample_kernel(lx_ref, ly_ref, buf_ref, h0_ref, cw_ref, cb_ref, wa_ref, ba_ref, wi_ref, bi_ref,
                       lam_ref, ob_ref, ht_ref, *, steps):
    cw = cw_ref[...]
    n_buf = CONV_W - 1

    def tap(t):
        return buf_ref[t] if t < n_buf else lx_ref[t - n_buf]

    h = h0_ref[...]
    for t in range(steps):
        xb = cb_ref[...] + cw[0:1, :] * tap(t)
        for j in range(1, CONV_W):
            xb = xb + cw[j:j + 1, :] * tap(t + j)
        a, b = _lru_gates(xb, wa_ref, ba_ref, wi_ref, bi_ref, lam_ref)
        h = a * h + b
        ob_ref[t] = (h * jax.nn.gelu(ly_ref[t])).astype(ob_ref.dtype)
    ht_ref[...] = h


def _lru_sample(lx_tm, ly_tm, buf_tm, h0, cw, cb, wa, ba, wi, bi, lam):
    steps, n, w = lx_tm.shape
    est = 16 * n * w * 4
    return pl.pallas_call(
        functools.partial(_lru_sample_kernel, steps=steps),
        out_shape=[jax.ShapeDtypeStruct((steps, n, w), BF16), jax.ShapeDtypeStruct((n, w), F32)],
        compiler_params=_params(None, est),
        name="lru_sample",
    )(lx_tm, ly_tm, buf_tm, h0, cw, cb.reshape(1, w), wa, ba.reshape(1, w), wi, bi.reshape(1, w),
      lam.reshape(1, w))


def _rope_lanes(x, cos, sin, lane):
    half = QK_ROPE // 2
    rot = jnp.where(lane % QK_ROPE < half, pltpu.roll(x, LANES - half, 1), pltpu.roll(x, half, 1))
    return x * cos + rot * sin


def _mla_prep_kernel(cq_ref, ckv_ref, kpe_ref, cos_ref, sin_ref, qg_ref, kg_ref,
                     cqn_ref, ckv_out_ref, ckv_bf_ref, kpe_out_ref, kpe_bf_ref):
    cqn_ref[...] = _rms(cq_ref[...], qg_ref[...]).astype(cqn_ref.dtype)
    ckv = _rms(ckv_ref[...], kg_ref[...])
    ckv_out_ref[...] = ckv
    ckv_bf_ref[...] = ckv.astype(ckv_bf_ref.dtype)
    x = kpe_ref[...]
    lane = lax.broadcasted_iota(jnp.int32, x.shape, 1)
    kpe = _rope_lanes(x, cos_ref[...], sin_ref[...], lane)
    kpe_out_ref[...] = kpe
    kpe_bf_ref[...] = kpe.astype(kpe_bf_ref.dtype)


def _mla_prep(yc, cos, sin, q_g, kv_g, *, tm):
    m = yc.shape[0]
    row = lambda width, col_block: pl.BlockSpec((tm, width), lambda i: (i, col_block))
    vec = pl.BlockSpec((1, Q_LORA), lambda i: (0, 0))
    est = 2 * tm * (3 * Q_LORA + 4 * LANES) * 4 * 3
    return pl.pallas_call(
        _mla_prep_kernel,
        grid=(m // tm,),
        in_specs=[row(Q_LORA, 0), row(KV_LORA, 1), row(LANES, C_COL_KPE // LANES), row(LANES, 0),
                  row(LANES, 0), vec, vec],
        out_specs=[row(Q_LORA, 0), row(KV_LORA, 0), row(KV_LORA, 0), row(LANES, 0), row(LANES, 0)],
        out_shape=[
            jax.ShapeDtypeStruct((m, Q_LORA), BF16), jax.ShapeDtypeStruct((m, KV_LORA), F32),
            jax.ShapeDtypeStruct((m, KV_LORA), BF16), jax.ShapeDtypeStruct((m, LANES), F32),
            jax.ShapeDtypeStruct((m, LANES), BF16),
        ],
        compiler_params=_params(("parallel",), est),
        name="mla_prep",
    )(yc, yc, yc, cos, sin, q_g.reshape(1, Q_LORA), kv_g.reshape(1, KV_LORA))


def _mm_qrope_kernel(a_ref, w_ref, cos_ref, sin_ref, o_ref, *, heads):
    acc = _dot(a_ref[...], w_ref[...])
    cos = cos_ref[...]
    sin = sin_ref[...]
    lane = lax.broadcasted_iota(jnp.int32, cos.shape, 1)
    for h in range(heads):
        c0 = h * Q_HEAD_COLS
        o_ref[:, c0:c0 + QK_NOPE] = (acc[:, c0:c0 + QK_NOPE] * MLA_SCALE).astype(o_ref.dtype)
        pe = _rope_lanes(acc[:, c0 + QK_NOPE:c0 + Q_HEAD_COLS], cos, sin, lane)
        o_ref[:, c0 + QK_NOPE:c0 + Q_HEAD_COLS] = (pe * MLA_SCALE).astype(o_ref.dtype)


def _mm_qrope(a, w, cos, sin, *, tm, tn):
    m, k = a.shape
    n = w.shape[1]
    est = 2 * (tm * k * 2 + k * tn * 2 + tm * tn * 2 + 2 * tm * LANES * 4) + 2 * tm * tn * 4
    return pl.pallas_call(
        functools.partial(_mm_qrope_kernel, heads=tn // Q_HEAD_COLS),
        grid=(n // tn, m // tm),
        in_specs=[
            pl.BlockSpec((tm, k), lambda j, i: (i, 0)), pl.BlockSpec((k, tn), lambda j, i: (0, j)),
            pl.BlockSpec((tm, LANES), lambda j, i: (i, 0)), pl.BlockSpec((tm, LANES), lambda j, i: (i, 0)),
        ],
        out_specs=pl.BlockSpec((tm, tn), lambda j, i: (i, j)),
        out_shape=jax.ShapeDtypeStruct((m, n), BF16),
        compiler_params=_params(("parallel", "parallel"), est),
        name="mm_qrope",
    )(a, w, cos, sin)


def _flash_kernel(q_ref, kv_ref, kpe_ref, o_ref, m_ref, l_ref, acc_ref, *, tq, heads):
    qi = pl.program_id(2)
    m_ref[...] = jnp.full(m_ref.shape, -jnp.inf, F32)
    l_ref[...] = jnp.zeros(l_ref.shape, F32)
    acc_ref[...] = jnp.zeros(acc_ref.shape, F32)
    reps = tq // LANES

    def tile(j, diagonal):
        k0 = pl.multiple_of(j * tq, tq)
        kp = kpe_ref[pl.ds(k0, tq), :]
        hd = range(heads)
        cols = [hh * Q_HEAD_COLS for hh in hd]
        ks = [jnp.concatenate([kv_ref[pl.ds(k0, tq), c0:c0 + QK_NOPE], kp], axis=-1) for c0 in cols]
        ss = [lax.dot_general(q_ref[:, c0:c0 + Q_HEAD_COLS], ks[hh], NT, preferred_element_type=F32)
              for hh, c0 in enumerate(cols)]
        if diagonal:
            r = lax.broadcasted_iota(jnp.int32, (tq, tq), 0)
            c = lax.broadcasted_iota(jnp.int32, (tq, tq), 1)
            ss = [jnp.where(c <= r, s, -jnp.inf) for s in ss]
        m_old = [m_ref[hh] for hh in hd]
        m_new = [jnp.maximum(m_old[hh], jnp.max(ss[hh], axis=-1, keepdims=True)) for hh in hd]
        ps = [jnp.exp(ss[hh] - jnp.concatenate([m_new[hh]] * reps, axis=-1)) for hh in hd]
        corr = [jnp.exp(m_old[hh] - m_new[hh]) for hh in hd]
        pv = [lax.dot_general(ps[hh].astype(BF16), kv_ref[pl.ds(k0, tq), c0 + QK_NOPE:c0 + Q_HEAD_COLS], NN,
                              preferred_element_type=F32) for hh, c0 in enumerate(cols)]
        for hh in hd:
            l_ref[hh] = l_ref[hh] * corr[hh] + jnp.sum(ps[hh], axis=-1, keepdims=True)
            acc_ref[hh] = acc_ref[hh] * corr[hh] + pv[hh]
            m_ref[hh] = m_new[hh]

    def body(j, carry):
        tile(j, False)
        return carry

    lax.fori_loop(0, qi, body, 0)
    tile(qi, True)
    for hh in range(heads):
        o_ref[:, hh * V_HEAD:(hh + 1) * V_HEAD] = (acc_ref[hh] / l_ref[hh]).astype(o_ref.dtype)


def _flash(q, kv, kpe_bf, *, n_seq, seq_len, tq, heads):
    nq = seq_len // tq
    qw = heads * Q_HEAD_COLS
    est = 2 * (tq * qw * 2 + seq_len * qw * 2 + seq_len * LANES * 2 + tq * heads * V_HEAD * 2)
    est += 6 * heads * tq * tq * 4
    return pl.pallas_call(
        functools.partial(_flash_kernel, tq=tq, heads=heads),
        grid=(n_seq, MLA_HEADS // heads, nq),
        in_specs=[
            pl.BlockSpec((tq, qw), lambda b, h, i: (b * nq + i, h)),
            pl.BlockSpec((seq_len, qw), lambda b, h, i: (b, h)),
            pl.BlockSpec((seq_len, LANES), lambda b, h, i: (b, 0)),
        ],
        out_specs=pl.BlockSpec((tq, heads * V_HEAD), lambda b, h, i: (b * nq + i, h)),
        out_shape=jax.ShapeDtypeStruct((n_seq * seq_len, MLA_HEADS * V_HEAD), BF16),
        scratch_shapes=[pltpu.VMEM((heads, tq, LANES), F32), pltpu.VMEM((heads, tq, LANES), F32),
                        pltpu.VMEM((heads, tq, V_HEAD), F32)],
        compiler_params=_params(("parallel", "parallel", "arbitrary"), est),
        name="flash_prompt",
    )(q, kv, kpe_bf)


def _head_mm_kernel(a_ref, w_ref, o_ref):
    o_ref[...] = _dot(a_ref[...], w_ref[0]).astype(o_ref.dtype)


def _head_mm(a, w, *, a_width, a_col_stride, out_dtype, name):
    m = a.shape[0]
    heads, k, n = w.shape
    est = 2 * (m * a_width * 2 + k * n * 2 + m * n * 4) + m * n * 4
    return pl.pallas_call(
        _head_mm_kernel,
        grid=(heads,),
        in_specs=[
            pl.BlockSpec((m, a_width), lambda h: (0, h * a_col_stride)),
            pl.BlockSpec((1, k, n), lambda h: (h, 0, 0)),
        ],
        out_specs=pl.BlockSpec((m, n), lambda h: (0, h)),
        out_shape=jax.ShapeDtypeStruct((m, heads * n), out_dtype),
        compiler_params=_params(("parallel",), est),
        name=name,
    )(a, w)


def _paged_kernel(pt_ref, qlat_ref, qpe_ref, ckn_ref, kpn_ref, *rest, pages, seqs, t_new):
    n_in = seqs * pages
    ck_refs = rest[:n_in]
    kp_refs = rest[n_in:2 * n_in]
    o_ref, m_ref, l_ref, acc_ref = rest[2 * n_in:]
    step = pl.program_id(1)

    @pl.when(step == 0)
    def _():
        m_ref[...] = jnp.full(m_ref.shape, -jnp.inf, F32)
        l_ref[...] = jnp.zeros(l_ref.shape, F32)
        acc_ref[...] = jnp.zeros(acc_ref.shape, F32)

    def update(qs, scores, keys):
        m_old = [m_ref[q] for q in qs]
        m_new = [jnp.maximum(m_old[i], jnp.max(scores[i], axis=-1, keepdims=True)) for i in range(len(qs))]
        p = [jnp.exp(scores[i] - m_new[i]) for i in range(len(qs))]
        corr = [jnp.exp(m_old[i] - m_new[i]) for i in range(len(qs))]
        pv = [lax.dot_general(p[i].astype(BF16), keys[i], NN, preferred_element_type=F32) for i in range(len(qs))]
        for i, q in enumerate(qs):
            l_ref[q] = l_ref[q] * corr[i] + jnp.sum(p[i], axis=-1, keepdims=True)
            acc_ref[q] = acc_ref[q] * corr[i] + pv[i]
            m_ref[q] = m_new[i]

    qs = list(range(seqs))
    ql = [qlat_ref[q] for q in qs]
    qp = [qpe_ref[q][:, :QK_ROPE] for q in qs]
    ck = [jnp.concatenate([r[0, 0].astype(BF16) for r in ck_refs[q * pages:(q + 1) * pages]], axis=0) for q in qs]
    kpt = [jnp.concatenate([r[0, 0].astype(BF16) for r in kp_refs[q * pages:(q + 1) * pages]], axis=1) for q in qs]
    s = [lax.dot_general(ql[q], ck[q], NT, preferred_element_type=F32)
         + lax.dot_general(qp[q], kpt[q], NN, preferred_element_type=F32) for q in qs]
    update(qs, s, ck)

    @pl.when(step == pl.num_programs(1) - 1)
    def _():
        rows = ql[0].shape[0]
        pad = PAGE_SIZE - ckn_ref.shape[1]
        ckn = [jnp.concatenate([ckn_ref[q], jnp.zeros((pad, KV_LORA), F32)], axis=0).astype(BF16) for q in qs]
        t_row = lax.broadcasted_iota(jnp.int32, (rows, PAGE_SIZE), 0) // MLA_HEADS
        j_col = lax.broadcasted_iota(jnp.int32, (rows, PAGE_SIZE), 1)
        valid = j_col <= jnp.minimum(t_row, t_new - 1)
        sn = [jnp.where(valid, lax.dot_general(ql[q], ckn[q], NT, preferred_element_type=F32)
                        + lax.dot_general(qp[q], kpn_ref[q].astype(BF16), NN, preferred_element_type=F32),
                        -jnp.inf) for q in qs]
        update(qs, sn, ckn)
        for q in qs:
            o_ref[q] = (acc_ref[q] / l_ref[q]).astype(o_ref.dtype)


def _paged(page_table, qlat, qpe, ckv_new8, kpe_new_t, cache_ckv, cache_kpe_t, *, layer, t_new):
    n_seq, rows, _ = qlat.shape
    n_pages = page_table.shape[1]
    pages, seqs = PAGES_PER_STEP, SEQS_PER_STEP
    steps = n_pages // pages
    pt = page_table.reshape(-1).astype(jnp.int32)
    t_pad = ckv_new8.shape[1]

    def page_spec(shape, q, k):
        return pl.BlockSpec((1, 1) + shape,
                            lambda b, s, pt_ref: (layer, pt_ref[(b * seqs + q) * n_pages + s * pages + k], 0, 0))

    per_seq = lambda d1, d2: pl.BlockSpec((seqs, d1, d2), lambda b, s, pt_ref: (b, 0, 0))
    in_specs = [per_seq(rows, KV_LORA), per_seq(rows, LANES), per_seq(t_pad, KV_LORA), per_seq(QK_ROPE, PAGE_SIZE)]
    in_specs += [page_spec((PAGE_SIZE, KV_LORA), q, k) for q in range(seqs) for k in range(pages)]
    in_specs += [page_spec((QK_ROPE, PAGE_SIZE), q, k) for q in range(seqs) for k in range(pages)]
    n_in = seqs * pages
    est = 2 * n_in * PAGE_SIZE * (KV_LORA + QK_ROPE) * 4 + 3 * n_in * PAGE_SIZE * (KV_LORA + QK_ROPE) * 2
    est += 8 * seqs * rows * (KV_LORA + pages * PAGE_SIZE) * 4
    grid_spec = pltpu.PrefetchScalarGridSpec(
        num_scalar_prefetch=1,
        grid=(n_seq // seqs, steps),
        in_specs=in_specs,
        out_specs=per_seq(rows, KV_LORA),
        scratch_shapes=[pltpu.VMEM((seqs, rows, 1), F32), pltpu.VMEM((seqs, rows, 1), F32),
                        pltpu.VMEM((seqs, rows, KV_LORA), F32)],
    )
    return pl.pallas_call(
        functools.partial(_paged_kernel, pages=pages, seqs=seqs, t_new=t_new),
        grid_spec=grid_spec,
        out_shape=jax.ShapeDtypeStruct((n_seq, rows, KV_LORA), BF16),
        compiler_params=_params(("parallel", "arbitrary"), est),
        name="paged_attn",
    )(pt, qlat, qpe, ckv_new8, kpe_new_t, *([cache_ckv] * n_in), *([cache_kpe_t] * n_in))


def _pad_rows(x, rows_before, rows_total):
    pad = [(0, 0)] * x.ndim
    pad[1] = (rows_before, rows_total - rows_before - x.shape[1])
    return jnp.pad(x, pad)


def _conv_state(buf, x):
    keep = CONV_W - 1
    if x.shape[1] >= keep:
        return x[:, x.shape[1] - keep:]
    return jnp.concatenate([buf.astype(x.dtype), x], axis=1)[:, -keep:]


def _rope_tables(positions):
    half = QK_ROPE // 2
    inv = ROPE_BASE ** (-jnp.arange(half, dtype=F32) / half)
    ang = positions.astype(F32)[:, None] * inv[None, :]
    cos, sin = jnp.cos(ang), jnp.sin(ang)
    zeros = jnp.zeros((positions.shape[0], LANES - QK_ROPE), F32)
    return (jnp.concatenate([cos, cos, zeros], axis=-1), jnp.concatenate([-sin, sin, zeros], axis=-1))


def kernel(x_prompt, x_sample, p_prompt, p_sample, state_gdn, state_gdn_conv, state_lru, state_lru_conv, cache_ckv, cache_kpe, page_table, w_in_a, conv_qkv_w, gdn_a_log, gdn_dt_bias, gdn_norm_g, conv_lru_w, conv_lru_b, lru_wa, lru_ba, lru_wi, lru_bi, lru_lambda, w_out_a, w_in_c, q_norm_g, kv_norm_g, w_uq, w_ukv, w_o_c, norm_mix_g, norm_ffn_g, w_up, w_down, w_ple, w_ple_gate, norm_final_g):
    bp, seq, d = x_prompt.shape
    bs, ts, _ = x_sample.shape
    mp, ms = bp * seq, bs * ts
    m = mp + ms
    depth = norm_mix_g.shape[0]
    past = page_table.shape[1] * PAGE_SIZE
    tm = 512

    h = jnp.concatenate([x_prompt.reshape(mp, d), x_sample.reshape(ms, d)], axis=0)
    pos = jnp.concatenate([jnp.tile(jnp.arange(seq), bp), jnp.tile(past + jnp.arange(ts), bs)])
    cos, sin = _rope_tables(pos)
    hn = _rmsnorm(h, norm_mix_g[0], tm=tm, out_dtype=BF16)

    gdn_p, gdn_s, gconv_p, gconv_s, lru_p, lru_s, lconv_p, lconv_s = [], [], [], [], [], [], [], []
    ckv_p, ckv_s, kpe_p, kpe_s = [], [], [], []

    for i in range(depth):
        j = i // 2
        if i % 2 == 0:
            wa = w_in_a[j]
            cut_b = GDN_QKV + GDN_WIDTH
            cut_lx = cut_b + 2 * GDN_HEADS
            w_a = jnp.concatenate([
                wa[:, :cut_b], wa[:, cut_lx:], wa[:, cut_b:cut_lx],
                jnp.zeros((d, LANES - 2 * GDN_HEADS), F32)], axis=1).astype(BF16)
            ya = _mm(hn, w_a, tm=tm, tn=A_COLS // 7, out_dtype=F32, name="in_proj_a")
            ya_p = ya.reshape(1, m, A_COLS)
            ya_s = ya[mp:].reshape(bs, ts, A_COLS)

            zero_buf = jnp.zeros((bp, SUBLANES, GDN_QKV), F32)
            parts_p = _gdn_intra(ya_p, zero_buf, conv_qkv_w[j], gdn_a_log[j], gdn_dt_bias[j], flat=True,
                                 n_groups=bp, nt=seq // GDN_CHUNK, bb=1, tl=GDN_CHUNK, l_real=GDN_CHUNK)
            o_gp, s_gp = _gdn_state(*parts_p, jnp.zeros((bp,) + state_gdn.shape[2:], F32), gdn_norm_g[j],
                                    nb=bp, nseg=1, tl=GDN_CHUNK)
            o_lp, h_lp = _lru_prompt(ya_p, jnp.zeros((bp, SUBLANES, LRU_WIDTH), F32),
                                     jnp.zeros((bp, 1, LRU_WIDTH), F32), conv_lru_w[j], conv_lru_b[j],
                                     lru_wa[j].astype(BF16), lru_ba[j].reshape(-1), lru_wi[j].astype(BF16),
                                     lru_bi[j].reshape(-1), lru_lambda[j], n_seq=bp, seq_rows=seq, tl=256)

            t_pad = SUBLANES
            ya_s8 = _pad_rows(ya_s, 0, t_pad)
            buf_g8 = _pad_rows(state_gdn_conv[j], SUBLANES - (CONV_W - 1), SUBLANES)
            seg = GDN_ROWS // t_pad
            parts_s = _gdn_intra(ya_s8, buf_g8, conv_qkv_w[j], gdn_a_log[j], gdn_dt_bias[j], flat=False,
                                 n_groups=bs // seg, nt=1, bb=seg, tl=t_pad, l_real=ts)
            o_gs, s_gs = _gdn_state(*parts_s, state_gdn[j], gdn_norm_g[j], nb=1, nseg=seg, tl=t_pad)
            o_gs = o_gs.reshape(bs, t_pad, GDN_WIDTH)
            lx_tm = ya_s[:, :, A_COL_LX:A_COL_LY].transpose(1, 0, 2)
            ly_tm = ya_s[:, :, A_COL_LY:A_COL_BA].transpose(1, 0, 2)
            o_ls_tm, h_ls = _lru_sample(lx_tm, ly_tm, state_lru_conv[j].transpose(1, 0, 2), state_lru[j],
                                        conv_lru_w[j], conv_lru_b[j], lru_wa[j].astype(BF16),
                                        lru_ba[j].reshape(-1), lru_wi[j].astype(BF16), lru_bi[j].reshape(-1),
                                        lru_lambda[j])
            mix = jnp.concatenate([
                jnp.concatenate([o_gp.reshape(mp, GDN_WIDTH), o_lp.reshape(mp, LRU_WIDTH)], axis=1),
                jnp.concatenate([o_gs[:, :ts].reshape(ms, GDN_WIDTH),
                                 o_ls_tm.transpose(1, 0, 2).reshape(ms, LRU_WIDTH)], axis=1)], axis=0)
            h, hn = _mm(mix, w_out_a, layer=j, single_buffer_w=True, tm=tm, tn=d, out_dtype=F32,
                        epilogue="residual_norm", residual=h, norm_g=norm_ffn_g[i], name="out_proj_a")

            keep = CONV_W - 1
            ya_pp = jnp.stack([ya[(b + 1) * seq - keep:(b + 1) * seq] for b in range(bp)])
            gdn_p.append(s_gp)
            gdn_s.append(s_gs)
            gconv_p.append(ya_pp[:, :, :GDN_QKV])
            gconv_s.append(_conv_state(state_gdn_conv[j], ya_s[:, :, :GDN_QKV]))
            lru_p.append(h_lp.reshape(bp, LRU_WIDTH))
            lru_s.append(h_ls)
            lconv_p.append(ya_pp[:, :, A_COL_LX:A_COL_LY])
            lconv_s.append(_conv_state(state_lru_conv[j], ya_s[:, :, A_COL_LX:A_COL_LY]))
        else:
            w_c = jnp.concatenate([w_in_c[j], jnp.zeros((d, LANES - QK_ROPE), F32)], axis=1).astype(BF16)
            yc = _mm(hn, w_c, tm=tm, tn=C_COLS // 3, out_dtype=F32, name="in_proj_c")
            cqn, ckv, ckv_bf, kpe, kpe_bf = _mla_prep(yc, cos, sin, q_norm_g[j], kv_norm_g[j], tm=tm)
            w_q = w_uq[j].reshape(Q_LORA, MLA_HEADS, QK_NOPE + QK_ROPE)
            w_q = jnp.pad(w_q, ((0, 0), (0, 0), (0, Q_HEAD_COLS - QK_NOPE - QK_ROPE)))
            q = _mm_qrope(cqn, w_q.reshape(Q_LORA, MLA_HEADS * Q_HEAD_COLS).astype(BF16), cos, sin,
                          tm=tm, tn=1024)
            w_r = w_ukv[j].reshape(KV_LORA, MLA_HEADS, QK_NOPE + V_HEAD)

            kv = _mm(ckv_bf[:mp], w_ukv, layer=j, tm=tm, tn=1024, out_dtype=BF16, name="kv_up")
            o_p = _flash(q, kv, kpe_bf, n_seq=bp, seq_len=seq, tq=512, heads=2)

            q_s = q[mp:]
            w_uk_t = w_r[:, :, :QK_NOPE].transpose(1, 2, 0).astype(BF16)
            w_uv = w_r[:, :, QK_NOPE:].transpose(1, 0, 2).astype(BF16)
            qlat = _head_mm(q_s, w_uk_t, a_width=QK_NOPE, a_col_stride=Q_HEAD_COLS // QK_NOPE,
                            out_dtype=BF16, name="q_latent")
            qlat = qlat.reshape(bs, ts * MLA_HEADS, KV_LORA)
            qpe = q_s.reshape(ms, MLA_HEADS, Q_HEAD_COLS)[:, :, QK_NOPE:].reshape(bs, ts * MLA_HEADS, LANES)
            ckv_new8 = _pad_rows(ckv[mp:].reshape(bs, ts, KV_LORA), 0, SUBLANES)
            kpe_new_t = kpe[mp:, :QK_ROPE].reshape(bs, ts, QK_ROPE).transpose(0, 2, 1)
            kpe_new_t = jnp.pad(kpe_new_t, ((0, 0), (0, 0), (0, PAGE_SIZE - ts)))
            lat = _paged(page_table, qlat, qpe, ckv_new8, kpe_new_t, cache_ckv,
                         cache_kpe.transpose(0, 1, 3, 2), layer=j, t_new=ts)
            o_s = _head_mm(lat.reshape(ms, MLA_HEADS * KV_LORA), w_uv, a_width=KV_LORA, a_col_stride=1,
                           out_dtype=BF16, name="v_up")
            h, hn = _mm(jnp.concatenate([o_p, o_s], axis=0), w_o_c, layer=j, single_buffer_w=True, tm=tm, tn=d,
                        out_dtype=F32, epilogue="residual_norm", residual=h, norm_g=norm_ffn_g[i],
                        name="out_proj_c")
            ckv_p.append(ckv[:mp].reshape(bp, seq, KV_LORA))
            ckv_s.append(ckv[mp:].reshape(bs, ts, KV_LORA))
            kpe_p.append(kpe[:mp, :QK_ROPE].reshape(bp, seq, QK_ROPE))
            kpe_s.append(kpe[mp:, :QK_ROPE].reshape(bs, ts, QK_ROPE))

        ff = _mm(hn, w_up, layer=i, tm=tm, tn=1024, out_dtype=BF16, epilogue="relu2", name="ffn_up")
        h = _mm(ff, w_down, layer=i, single_buffer_w=True, tm=tm, tn=512, out_dtype=F32, epilogue="residual",
                residual=h, name="ffn_down")
        p = jnp.concatenate([p_prompt[i].reshape(mp, -1), p_sample[i].reshape(ms, -1)], axis=0)
        last = i + 1 == depth
        g_next = norm_final_g if last else norm_mix_g[i + 1]
        h, hn = _ple(h, p, w_ple, w_ple_gate, g_next, layer=i, tm=256,
                     norm_dtype=F32 if last else BF16)

    y = hn
    return (y[:mp].reshape(bp, seq, d), y[mp:].reshape(bs, ts, d),
            jnp.stack(gdn_p), jnp.stack(gdn_s), jnp.stack(gconv_p), jnp.stack(gconv_s),
            jnp.stack(lru_p), jnp.stack(lru_s), jnp.stack(lconv_p), jnp.stack(lconv_s),
            jnp.stack(ckv_p), jnp.stack(ckv_s), jnp.stack(kpe_p), jnp.stack(kpe_s))
```

```python
import functools
import math

import jax
import jax.numpy as jnp
from jax import lax
from jax.experimental import pallas as pl
from jax.experimental.pallas import tpu as pltpu

F32 = jnp.float32
BF16 = jnp.bfloat16

D_MODEL = 2048
CONV_W = 4
GDN_HEADS = 8
GDN_DK = 128
GDN_DV = 128
GDN_QK = GDN_HEADS * GDN_DK
GDN_WIDTH = GDN_HEADS * GDN_DV
GDN_QKV = 2 * GDN_QK + GDN_WIDTH
GDN_CHUNK = 64
LRU_WIDTH = D_MODEL // 2
LRU_BLOCKS = 8
LRU_BW = LRU_WIDTH // LRU_BLOCKS
LRU_C = 8.0
MLA_HEADS = 16
Q_LORA = 512
KV_LORA = 512
QK_NOPE = 128
QK_ROPE = 64
V_HEAD = 128
MLA_SCALE = (QK_NOPE + QK_ROPE) ** -0.5
ROPE_BASE = 10000.0
PAGE_SIZE = 128
EPS = 1e-6

LANES = 128
SUBLANES = 8
VMEM_LIMIT_BYTES = 56 * 1024 * 1024

A_COL_Z = GDN_QKV
A_COL_LX = A_COL_Z + GDN_WIDTH
A_COL_LY = A_COL_LX + LRU_WIDTH
A_COL_BA = A_COL_LY + LRU_WIDTH
A_COLS = A_COL_BA + LANES
C_COL_KPE = Q_LORA + KV_LORA
C_COLS = C_COL_KPE + LANES
Q_HEAD_COLS = 2 * LANES
PAGES_PER_STEP = 8
SEQS_PER_STEP = 4
GDN_ROWS = 64

NN = (((1,), (0,)), ((), ()))
NT = (((1,), (1,)), ((), ()))
TN = (((0,), (0,)), ((), ()))


def _params(semantics, est_bytes):
    limit = int(min(max(2 * est_bytes, 32 * 1024 * 1024), VMEM_LIMIT_BYTES))
    return pltpu.CompilerParams(dimension_semantics=semantics, vmem_limit_bytes=limit)


def _vmem_params(semantics, need_bytes):
    limit = int(min(max(need_bytes + 4 * 1024 * 1024, 32 * 1024 * 1024), VMEM_LIMIT_BYTES))
    return pltpu.CompilerParams(dimension_semantics=semantics, vmem_limit_bytes=limit)


def _nbytes(shape, dtype):
    return math.prod(shape) * jnp.dtype(dtype).itemsize


def _dot(a, b, dims=NN):
    return lax.dot_general(a.astype(BF16), b.astype(BF16), dims, preferred_element_type=F32)


def _dot_hi(a, b, dims=NN):
    return lax.dot_general(a, b, dims, precision=lax.Precision.HIGHEST, preferred_element_type=F32)


def _sigmoid(x):
    return jax.nn.sigmoid(x)


def _softplus(x):
    return jnp.maximum(x, 0.0) + jnp.log1p(jnp.exp(-jnp.abs(x)))


def _rms(x, g):
    return x * lax.rsqrt(jnp.mean(x * x, axis=-1, keepdims=True) + EPS) * g


def _pick(is_prompt, prompt_ref, sample_ref):
    return jnp.where(is_prompt, prompt_ref[...], sample_ref[...])


def _pair_specs(pair, tm, n_prompt_tiles, index):
    prompt, sample = pair
    width = prompt.shape[1]
    return [
        pl.BlockSpec((tm, width), lambda *g: (jnp.minimum(index(*g), n_prompt_tiles - 1), 0)),
        pl.BlockSpec((tm, width), lambda *g: (jnp.maximum(index(*g) - n_prompt_tiles, 0), 0)),
    ]


def _rmsnorm_kernel(xp_ref, xs_ref, g_ref, o_ref, *, n_prompt_tiles):
    x = _pick(pl.program_id(0) < n_prompt_tiles, xp_ref, xs_ref)
    o_ref[...] = _rms(x, g_ref[...]).astype(o_ref.dtype)


def _rmsnorm(x_pair, g, *, tm, out_dtype):
    d = x_pair[0].shape[1]
    m = x_pair[0].shape[0] + x_pair[1].shape[0]
    npt = x_pair[0].shape[0] // tm
    est = 2 * (2 * _nbytes((tm, d), F32) + _nbytes((tm, d), out_dtype)) + 2 * _nbytes((tm, d), F32)
    return pl.pallas_call(
        functools.partial(_rmsnorm_kernel, n_prompt_tiles=npt),
        grid=(m // tm,),
        in_specs=_pair_specs(x_pair, tm, npt, lambda i: i) + [pl.BlockSpec((1, d), lambda i: (0, 0))],
        out_specs=pl.BlockSpec((tm, d), lambda i: (i, 0)),
        out_shape=jax.ShapeDtypeStruct((m, d), out_dtype),
        compiler_params=_params(("arbitrary",), est),
        name="rmsnorm",
    )(*x_pair, g.reshape(1, d))


def _mm_kernel(*refs, a_dual, res_dual, epilogue, cast_w, n_prompt_tiles):
    refs = list(refs)
    is_prompt = pl.program_id(1) < n_prompt_tiles

    def take(dual):
        if dual:
            return _pick(is_prompt, refs.pop(0), refs.pop(0))
        return refs.pop(0)[...]

    cols = [take(dual) for dual in a_dual]
    a = cols[0] if len(cols) == 1 else jnp.concatenate(cols, axis=1)
    w_ref = refs.pop(0)
    if cast_w:
        wbf_ref = refs.pop()

        @pl.when(pl.program_id(1) == 0)
        def _():
            wbf_ref[...] = w_ref[0].astype(BF16)

        w = wbf_ref[...]
    else:
        w = w_ref[...]
    acc = _dot(a, w)
    if epilogue == "relu2":
        acc = jnp.square(jnp.maximum(acc, 0.0))
    elif epilogue in ("residual", "residual_norm"):
        acc = acc + take(res_dual)
    if epilogue == "residual_norm":
        g_ref, o_ref, n_ref = refs
        n_ref[...] = _rms(acc, g_ref[...]).astype(n_ref.dtype)
    else:
        (o_ref,) = refs
    o_ref[...] = acc.astype(o_ref.dtype)


def _mm(a, w, *, tm, tn, out_dtype, epilogue=None, residual=None, norm_g=None, layer=None,
        single_buffer_w=False, name):
    parts = a if isinstance(a, list) else [a]
    rows = lambda part: sum(x.shape[0] for x in part) if isinstance(part, tuple) else part.shape[0]
    width = lambda part: part[0].shape[1] if isinstance(part, tuple) else part.shape[1]
    a_dtype = parts[0][0].dtype if isinstance(parts[0], tuple) else parts[0].dtype
    m, k = rows(parts[0]), sum(width(part) for part in parts)
    pairs = [x for x in parts + [residual] if isinstance(x, tuple)]
    npt = pairs[0][0].shape[0] // tm if pairs else m // tm
    n = w.shape[-1]
    cast_w = layer is not None
    if cast_w:
        mode = dict(pipeline_mode=pl.Buffered(1)) if single_buffer_w else {}
        w_spec = pl.BlockSpec((1, k, tn), lambda j, i: (layer, 0, j), **mode)
        w_bytes = (1 if single_buffer_w else 2) * _nbytes((k, tn), F32) + _nbytes((k, tn), BF16)
        scratch = [pltpu.VMEM((k, tn), BF16)]
    else:
        w_spec = pl.BlockSpec((k, tn), lambda j, i: (0, j))
        w_bytes = 2 * _nbytes((k, tn), w.dtype)
        scratch = []
    tile = pl.BlockSpec((tm, tn), lambda j, i: (i, j))
    in_specs, args = [], []

    def add(part, spec):
        if isinstance(part, tuple):
            in_specs.extend(_pair_specs(part, tm, npt, lambda j, i: i))
            args.extend(part)
        else:
            in_specs.append(spec)
            args.append(part)

    for part in parts:
        add(part, pl.BlockSpec((tm, width(part)), lambda j, i: (i, 0)))
    in_specs.append(w_spec)
    args.append(w)
    out_specs, out_shape = tile, jax.ShapeDtypeStruct((m, n), out_dtype)
    est = w_bytes + 4 * _nbytes((tm, k), a_dtype) + 2 * _nbytes((tm, tn), out_dtype) + 2 * _nbytes((tm, tn), F32)
    if epilogue in ("residual", "residual_norm"):
        assert not isinstance(residual, tuple) or tn == n
        add(residual, tile)
        est += 4 * _nbytes((tm, tn), F32)
    if epilogue == "residual_norm":
        assert tn == n
        in_specs.append(pl.BlockSpec((1, n), lambda j, i: (0, 0)))
        args.append(norm_g.reshape(1, n))
        out_specs = [tile, tile]
        out_shape = [out_shape, jax.ShapeDtypeStruct((m, n), BF16)]
        est += 2 * _nbytes((tm, tn), BF16) + _nbytes((tm, tn), F32)
    return pl.pallas_call(
        functools.partial(_mm_kernel, a_dual=tuple(isinstance(part, tuple) for part in parts),
                          res_dual=isinstance(residual, tuple), epilogue=epilogue, cast_w=cast_w,
                          n_prompt_tiles=npt),
        grid=(n // tn, m // tm),
        in_specs=in_specs,
        out_specs=out_specs,
        out_shape=out_shape,
        scratch_shapes=scratch,
        compiler_params=_vmem_params(("parallel", "arbitrary"), est),
        name=name,
    )(*args)


def _ple_kernel(h_ref, p_ref, wp_ref, wg_ref, g_ref, *rest, n_prompt_tiles):
    *outs, wp_bf_ref, wg_bf_ref = rest
    i = pl.program_id(0)

    @pl.when(i == 0)
    def _():
        wp_bf_ref[...] = wp_ref[0].astype(BF16)
        wg_bf_ref[...] = wg_ref[0].astype(BF16)

    h = h_ref[...]
    gate = _sigmoid(_dot(h, wg_bf_ref[...]))
    h2 = h + _dot(p_ref[...], wp_bf_ref[...]) * gate
    if n_prompt_tiles is None:
        h_out_ref, n_out_ref = outs
        h_out_ref[...] = h2
        n_out_ref[...] = _rms(h2, g_ref[...]).astype(n_out_ref.dtype)
    else:
        yp_ref, ys_ref = outs
        y = _rms(h2, g_ref[...]).astype(yp_ref.dtype)

        @pl.when(i < n_prompt_tiles)
        def _():
            yp_ref[...] = y

        @pl.when(i >= n_prompt_tiles)
        def _():
            ys_ref[...] = y


def _ple(h, p, wp, wg, g, *, layer, tm, norm_dtype, split_rows=None):
    m, d = h.shape
    pd = p.shape[1]
    est = (_nbytes((pd, d), F32) + _nbytes((d, d), F32) + _nbytes((pd, d), BF16) + _nbytes((d, d), BF16)
           + 2 * (2 * _nbytes((tm, d), F32) + _nbytes((tm, pd), F32) + _nbytes((tm, d), norm_dtype))
           + 3 * _nbytes((tm, d), F32))
    once = dict(pipeline_mode=pl.Buffered(1))
    row = pl.BlockSpec((tm, d), lambda i: (i, 0))
    if split_rows is None:
        npt = None
        out_specs = [row, row]
        out_shape = [jax.ShapeDtypeStruct((m, d), F32), jax.ShapeDtypeStruct((m, d), norm_dtype)]
    else:
        npt = split_rows // tm
        out_specs = [pl.BlockSpec((tm, d), lambda i: (jnp.minimum(i, npt - 1), 0)),
                     pl.BlockSpec((tm, d), lambda i: (jnp.maximum(i - npt, 0), 0))]
        out_shape = [jax.ShapeDtypeStruct((split_rows, d), norm_dtype),
                     jax.ShapeDtypeStruct((m - split_rows, d), norm_dtype)]
    return pl.pallas_call(
        functools.partial(_ple_kernel, n_prompt_tiles=npt),
        grid=(m // tm,),
        in_specs=[
            row,
            pl.BlockSpec((tm, pd), lambda i: (i, 0)),
            pl.BlockSpec((1, pd, d), lambda i: (layer, 0, 0), **once),
            pl.BlockSpec((1, d, d), lambda i: (layer, 0, 0), **once),
            pl.BlockSpec((1, d), lambda i: (0, 0)),
        ],
        out_specs=out_specs,
        out_shape=out_shape,
        scratch_shapes=[pltpu.VMEM((pd, d), BF16), pltpu.VMEM((d, d), BF16)],
        compiler_params=_vmem_params(("arbitrary",), est),
        name="ple",
    )(h, p, wp, wg, g.reshape(1, d))


def _gdn_intra_kernel(qkv_ref, prev_ref, ba_ref, z_ref, buf_ref, cw_ref, alog_ref, dtb_ref,
                      u_ref, w_ref, qg_ref, kd_ref, qk_ref, el_ref, zs_ref, xs_ref, *, bb, tl, l_real):
    rows = bb * tl
    i = pl.program_id(1)
    xs_ref[:, 0:SUBLANES, :] = jnp.where(i == 0, buf_ref[...], prev_ref[...])
    x_in = qkv_ref[...]
    xs_ref[:, SUBLANES:SUBLANES + tl, :] = x_in
    cw = cw_ref[...]
    halo0 = SUBLANES - (CONV_W - 1)
    y = cw[CONV_W - 1:CONV_W, :] * x_in
    for j in range(CONV_W - 1):
        y = y + cw[j:j + 1, :] * xs_ref[:, halo0 + j:halo0 + j + tl, :]
    y = (y * _sigmoid(y)).reshape(rows, GDN_QKV)
    ba = ba_ref[...].reshape(rows, LANES)
    z = z_ref[...].reshape(rows, GDN_WIDTH)
    zs_ref[0] = (z * _sigmoid(z)).astype(zs_ref.dtype)

    shift = int(math.log2(tl))
    ri = lax.broadcasted_iota(jnp.int32, (rows, rows), 0)
    ci = lax.broadcasted_iota(jnp.int32, (rows, rows), 1)
    if bb == 1:
        lower = ri >= ci
        strict = ri > ci
    else:
        same = (ri >> shift) == (ci >> shift)
        lower = same & (ri >= ci)
        strict = same & (ri > ci)
    tril = lower.astype(F32)
    seg_last = (ci == ((ri >> shift) << shift) + (tl - 1)).astype(F32)
    lane = lax.broadcasted_iota(jnp.int32, (rows, LANES), 1)
    beta_all = _sigmoid(ba)
    g_all = -jnp.exp(alog_ref[...]) * _softplus(ba + dtb_ref[...])
    g_all = jnp.where(lane >= GDN_HEADS, g_all, 0.0)
    if l_real < tl:
        r1 = lax.broadcasted_iota(jnp.int32, (rows, 1), 0)
        rowmask = ((r1 & (tl - 1)) < l_real).astype(F32)
        y = y * rowmask
        beta_all = beta_all * rowmask
        g_all = g_all * rowmask
    gc_all = _dot_hi(tril, g_all)
    gl_all = _dot_hi(seg_last, gc_all)
    el_ref[0] = jnp.exp(gl_all)
    eg_all = jnp.exp(gc_all)
    kdf_all = jnp.exp(gl_all - gc_all)
    gct = jnp.concatenate([gc_all, jnp.zeros((LANES - rows, LANES), F32)], axis=0).T
    n_factors = max(1, math.ceil(math.log2(l_real)))

    heads = range(GDN_HEADS)
    lanes = [GDN_HEADS + h for h in heads]
    hsl = [slice(h * GDN_DV, (h + 1) * GDN_DV) for h in heads]
    qs = [y[:, h * GDN_DK:(h + 1) * GDN_DK] for h in heads]
    ks = [y[:, GDN_QK + h * GDN_DK:GDN_QK + (h + 1) * GDN_DK] for h in heads]
    vs = [y[:, 2 * GDN_QK + h * GDN_DV:2 * GDN_QK + (h + 1) * GDN_DV] for h in heads]
    qs = [q * lax.rsqrt(jnp.sum(q * q, axis=-1, keepdims=True) + EPS) * (GDN_DK ** -0.5) for q in qs]
    ks = [k * lax.rsqrt(jnp.sum(k * k, axis=-1, keepdims=True) + EPS) for k in ks]
    decays = [jnp.where(lower, jnp.exp(gc_all[:, lh:lh + 1] - gct[lh:lh + 1, 0:rows]), 0.0) for lh in lanes]
    kbs = [ks[h] * beta_all[:, h:h + 1] for h in heads]
    ps = [-jnp.where(strict, _dot(kbs[h], ks[h], NT) * decays[h], 0.0) for h in heads]
    xs = [jnp.concatenate([vs[h] * beta_all[:, h:h + 1], kbs[h] * eg_all[:, lanes[h]:lanes[h] + 1]], axis=-1)
          for h in heads]
    for h in heads:
        qk = jnp.where(lower, _dot(qs[h], ks[h], NT) * decays[h], 0.0)
        qk_ref[h, 0] = qk.astype(qk_ref.dtype)
        qg_ref[0, :, hsl[h]] = (qs[h] * eg_all[:, lanes[h]:lanes[h] + 1]).astype(qg_ref.dtype)
        kd_ref[0, :, hsl[h]] = (ks[h] * kdf_all[:, lanes[h]:lanes[h] + 1]).astype(kd_ref.dtype)
    for f in range(n_factors):
        xs = [xs[h] + _dot(ps[h], xs[h]) for h in heads]
        if f + 1 < n_factors:
            ps = [_dot(ps[h], ps[h]) for h in heads]
    for h in heads:
        u_ref[0, :, hsl[h]] = xs[h][:, :GDN_DV]
        w_ref[0, :, hsl[h]] = xs[h][:, GDN_DV:].astype(w_ref.dtype)


def _gdn_intra(ya3, buf8, conv_w, a_log, dt_bias, *, flat, n_groups, nt, bb, tl, l_real):
    rows = bb * tl
    assert rows == GDN_ROWS and (flat or nt == 1)

    def spec(width, col_block):
        if flat:
            return pl.BlockSpec((1, rows, width), lambda g, i: (0, g * nt + i, col_block))
        return pl.BlockSpec((bb, tl, width), lambda g, i: (g, 0, col_block))

    if flat:
        prev = pl.BlockSpec((1, SUBLANES, GDN_QKV),
                            lambda g, i: (0, jnp.maximum((g * nt + i) * (rows // SUBLANES) - 1, 0), 0))
    else:
        prev = spec(GDN_QKV, 0)
    lane_pad = jnp.zeros((1, LANES), F32)
    alog = lane_pad.at[0, GDN_HEADS:2 * GDN_HEADS].set(a_log)
    dtb = lane_pad.at[0, GDN_HEADS:2 * GDN_HEADS].set(dt_bias)
    total = nt * rows
    out = lambda width: pl.BlockSpec((1, rows, width), lambda g, i: (g, i, 0))
    shape = lambda width, dt: jax.ShapeDtypeStruct((n_groups, total, width), dt)
    est = 4 * rows * (2 * GDN_QKV + 6 * GDN_WIDTH) * 4
    return pl.pallas_call(
        functools.partial(_gdn_intra_kernel, bb=bb, tl=tl, l_real=l_real),
        grid=(n_groups, nt),
        in_specs=[
            spec(GDN_QKV, 0), prev, spec(LANES, A_COL_BA // LANES), spec(GDN_WIDTH, A_COL_Z // GDN_WIDTH),
            pl.BlockSpec((bb, SUBLANES, GDN_QKV), lambda g, i: (g, 0, 0)),
            pl.BlockSpec((CONV_W, GDN_QKV), lambda g, i: (0, 0)),
            pl.BlockSpec((1, LANES), lambda g, i: (0, 0)),
            pl.BlockSpec((1, LANES), lambda g, i: (0, 0)),
        ],
        out_specs=[
            out(GDN_WIDTH), out(GDN_WIDTH), out(GDN_WIDTH), out(GDN_WIDTH),
            pl.BlockSpec((GDN_HEADS, 1, rows, rows), lambda g, i: (0, g, i, 0)),
            out(LANES), out(GDN_WIDTH),
        ],
        out_shape=[
            shape(GDN_WIDTH, F32), shape(GDN_WIDTH, BF16), shape(GDN_WIDTH, BF16), shape(GDN_WIDTH, BF16),
            jax.ShapeDtypeStruct((GDN_HEADS, n_groups, total, rows), BF16),
            shape(LANES, F32), shape(GDN_WIDTH, BF16),
        ],
        scratch_shapes=[pltpu.VMEM((bb, tl + SUBLANES, GDN_QKV), F32)],
        compiler_params=_params(("parallel", "parallel"), est),
        name="gdn_intra",
    )(ya3, ya3, ya3, ya3, buf8, conv_w, alog, dtb)


def _gdn_state_kernel(u_ref, w_ref, qg_ref, kd_ref, qk_ref, el_ref, zs_ref, s0_ref, ng_ref,
                      o_ref, sout_ref, s_ref, *, nb, nseg, tl):
    i = pl.program_id(1)

    @pl.when(i == 0)
    def _():
        s_ref[...] = s0_ref[...]

    rows = nseg * tl
    seg_of_row = lax.broadcasted_iota(jnp.int32, (rows, 1), 0) >> int(math.log2(tl))
    pairs = [(blk, h) for blk in range(nb) for h in range(GDN_HEADS)]
    hsl = lambda h: slice(h * GDN_DV, (h + 1) * GDN_DV)
    segs = [slice(s * tl, (s + 1) * tl) for s in range(nseg)]
    cat = lambda parts: parts[0] if len(parts) == 1 else jnp.concatenate(parts, axis=0)
    inter = {}
    for blk, h in pairs:
        w = w_ref[blk, :, hsl(h)].astype(F32)
        qg = qg_ref[blk, :, hsl(h)].astype(F32)
        inter[blk, h] = [_dot(jnp.concatenate([w[rs], qg[rs]], axis=0), s_ref[blk * nseg + s, h])
                         for s, rs in enumerate(segs)]
    v_new = {}
    for blk, h in pairs:
        u = u_ref[blk, :, hsl(h)]
        v_new[blk, h] = cat([u[rs] - inter[blk, h][s][:tl] for s, rs in enumerate(segs)])
    for blk, h in pairs:
        o = cat([r[tl:] for r in inter[blk, h]]) + _dot(qk_ref[h, blk], v_new[blk, h])
        on = _rms(o, ng_ref[...]) * zs_ref[blk, :, hsl(h)].astype(F32)
        o_ref[blk, :, hsl(h)] = on.astype(o_ref.dtype)
    for blk, h in pairs:
        kd = kd_ref[blk, :, hsl(h)].astype(F32)
        lh = GDN_HEADS + h
        for s in range(nseg):
            e = el_ref[blk, s * tl:s * tl + 1, lh:lh + 1]
            kds = kd if nseg == 1 else jnp.where(seg_of_row == s, kd, 0.0)
            idx = blk * nseg + s
            s_ref[idx, h] = s_ref[idx, h] * e + _dot(kds, v_new[blk, h], TN)

    @pl.when(i == pl.num_programs(1) - 1)
    def _():
        sout_ref[...] = s_ref[...]


def _gdn_state(u, w, qg, kd, qk, el, zs, s0, norm_g, *, nb, nseg, tl):
    n_groups, total, _ = u.shape
    rows = nseg * tl
    assert rows == GDN_ROWS
    blk = lambda width: pl.BlockSpec((nb, rows, width), lambda g, i: (g, i, 0))
    st = pl.BlockSpec((nb * nseg, GDN_HEADS, GDN_DK, GDN_DV), lambda g, i: (g, 0, 0, 0))
    est = 2 * nb * rows * GDN_WIDTH * 16 + 5 * nb * nseg * GDN_HEADS * GDN_DK * GDN_DV * 4
    return pl.pallas_call(
        functools.partial(_gdn_state_kernel, nb=nb, nseg=nseg, tl=tl),
        grid=(n_groups // nb, total // rows),
        in_specs=[
            blk(GDN_WIDTH), blk(GDN_WIDTH), blk(GDN_WIDTH), blk(GDN_WIDTH),
            pl.BlockSpec((GDN_HEADS, nb, rows, rows), lambda g, i: (0, g, i, 0)),
            blk(LANES), blk(GDN_WIDTH), st,
            pl.BlockSpec((1, GDN_DV), lambda g, i: (0, 0)),
        ],
        out_specs=[blk(GDN_WIDTH), st],
        out_shape=[
            jax.ShapeDtypeStruct((n_groups, total, GDN_WIDTH), BF16),
            jax.ShapeDtypeStruct(s0.shape, F32),
        ],
        scratch_shapes=[pltpu.VMEM((nb * nseg, GDN_HEADS, GDN_DK, GDN_DV), F32)],
        compiler_params=_params(("parallel", "arbitrary"), est),
        name="gdn_state",
    )(u, w, qg, kd, qk, el, zs, s0, norm_g.reshape(1, GDN_DV))


def _lru_gates(xb, wa_ref, ba_ref, wi_ref, bi_ref, lam_ref):
    c = -LRU_C * _softplus(-lam_ref[...])
    sls = [slice(kb * LRU_BW, (kb + 1) * LRU_BW) for kb in range(LRU_BLOCKS)]
    xks = [xb[:, sl] for sl in sls]
    ra = [_dot(xks[kb], wa_ref[kb]) for kb in range(LRU_BLOCKS)]
    ri = [_dot(xks[kb], wi_ref[kb]) for kb in range(LRU_BLOCKS)]
    a_parts, b_parts = [], []
    for kb, sl in enumerate(sls):
        r = _sigmoid(ra[kb] + ba_ref[:, sl])
        ig = _sigmoid(ri[kb] + bi_ref[:, sl])
        log_a = c[:, sl] * r
        a_parts.append(jnp.exp(log_a))
        b_parts.append(jnp.sqrt(1.0 - jnp.exp(2.0 * log_a)) * (ig * xks[kb]))
    return jnp.concatenate(a_parts, axis=-1), jnp.concatenate(b_parts, axis=-1)


def _lru_prompt_kernel(lx_ref, ly_ref, buf_ref, h0_ref, cw_ref, cb_ref, wa_ref, ba_ref, wi_ref, bi_ref,
                       lam_ref, ob_ref, ht_ref, xs_ref, a_ref, b_ref, hs_ref, h_ref, *, tl):
    i = pl.program_id(1)

    @pl.when(i == 0)
    def _():
        h_ref[...] = h0_ref[0]
        xs_ref[0:SUBLANES, :] = buf_ref[0]

    @pl.when(i > 0)
    def _():
        xs_ref[0:SUBLANES, :] = xs_ref[tl:tl + SUBLANES, :]

    xs_ref[SUBLANES:SUBLANES + tl, :] = lx_ref[0]
    cw = cw_ref[...]
    halo0 = SUBLANES - (CONV_W - 1)
    xb = cw[CONV_W - 1:CONV_W, :] * xs_ref[SUBLANES:SUBLANES + tl, :] + cb_ref[...]
    for j in range(CONV_W - 1):
        xb = xb + cw[j:j + 1, :] * xs_ref[halo0 + j:halo0 + j + tl, :]
    a, b = _lru_gates(xb, wa_ref, ba_ref, wi_ref, bi_ref, lam_ref)
    a_ref[...] = a
    b_ref[...] = b

    def body(t, h):
        h = a_ref[pl.ds(t, 1), :] * h + b_ref[pl.ds(t, 1), :]
        hs_ref[pl.ds(t, 1), :] = h
        return h

    h = lax.fori_loop(0, tl, body, h_ref[...], unroll=8)
    h_ref[...] = h
    ob_ref[0] = (hs_ref[...] * jax.nn.gelu(ly_ref[0])).astype(ob_ref.dtype)

    @pl.when(i == pl.num_programs(1) - 1)
    def _():
        ht_ref[0] = h


def _lru_prompt(ya3, buf8, h0, cw, cb, wa, ba, wi, bi, lam, *, n_seq, seq_rows, tl):
    nt = seq_rows // tl
    w = LRU_WIDTH

    def spec(col_block):
        return pl.BlockSpec((1, tl, w), lambda b, i: (0, b * nt + i, col_block))

    vec = pl.BlockSpec((1, w), lambda b, i: (0, 0))
    blk = pl.BlockSpec((LRU_BLOCKS, LRU_BW, LRU_BW), lambda b, i: (0, 0, 0))
    est = 8 * tl * w * 4 + 4 * (tl + SUBLANES) * w * 4 + 4 * LRU_BLOCKS * LRU_BW * LRU_BW * 2
    return pl.pallas_call(
        functools.partial(_lru_prompt_kernel, tl=tl),
        grid=(n_seq, nt),
        in_specs=[
            spec(A_COL_LX // w), spec(A_COL_LY // w),
            pl.BlockSpec((1, SUBLANES, w), lambda b, i: (b, 0, 0)),
            pl.BlockSpec((1, 1, w), lambda b, i: (b, 0, 0)),
            pl.BlockSpec((CONV_W, w), lambda b, i: (0, 0)), vec, blk, vec, blk, vec, vec,
        ],
        out_specs=[
            pl.BlockSpec((1, tl, w), lambda b, i: (b, i, 0)),
            pl.BlockSpec((1, 1, w), lambda b, i: (b, 0, 0)),
        ],
        out_shape=[
            jax.ShapeDtypeStruct((n_seq, seq_rows, w), BF16),
            jax.ShapeDtypeStruct((n_seq, 1, w), F32),
        ],
        scratch_shapes=[
            pltpu.VMEM((tl + SUBLANES, w), F32), pltpu.VMEM((tl, w), F32), pltpu.VMEM((tl, w), F32),
            pltpu.VMEM((tl, w), F32), pltpu.VMEM((1, w), F32),
        ],
        compiler_params=_params(("parallel", "arbitrary"), est),
        name="lru_prompt",
    )(ya3, ya3, buf8, h0, cw, cb.reshape(1, w), wa, ba.reshape(1, w), wi, bi.reshape(1, w),
      lam.reshape(1, w))


def _lru_sample_kernel(lx_ref, ly_ref, buf_ref, h0_ref, cw_ref, cb_ref, wa_ref, ba_ref, wi_ref, bi_ref,
                       lam_ref, ob_ref, ht_ref, *, steps):
    cw = cw_ref[...]
    n_buf = CONV_W - 1

    def tap(t):
        return buf_ref[t] if t < n_buf else lx_ref[t - n_buf]

    h = h0_ref[...]
    for t in range(steps):
        xb = cb_ref[...] + cw[0:1, :] * tap(t)
        for j in range(1, CONV_W):
            xb = xb + cw[j:j + 1, :] * tap(t + j)
        a, b = _lru_gates(xb, wa_ref, ba_ref, wi_ref, bi_ref, lam_ref)
        h = a * h + b
        ob_ref[t] = (h * jax.nn.gelu(ly_ref[t])).astype(ob_ref.dtype)
    ht_ref[...] = h


def _lru_sample(lx_tm, ly_tm, buf_tm, h0, cw, cb, wa, ba, wi, bi, lam):
    steps, n, w = lx_tm.shape
    est = 16 * n * w * 4
    return pl.pallas_call(
        functools.partial(_lru_sample_kernel, steps=steps),
        out_shape=[jax.ShapeDtypeStruct((steps, n, w), BF16), jax.ShapeDtypeStruct((n, w), F32)],
        compiler_params=_params(None, est),
        name="lru_sample",
    )(lx_tm, ly_tm, buf_tm, h0, cw, cb.reshape(1, w), wa, ba.reshape(1, w), wi, bi.reshape(1, w),
      lam.reshape(1, w))


def _rope_lanes(x, cos, sin, lane):
    half = QK_ROPE // 2
    rot = jnp.where(lane % QK_ROPE < half, pltpu.roll(x, LANES - half, 1), pltpu.roll(x, half, 1))
    return x * cos + rot * sin


def _mla_prep_kernel(cq_ref, ckv_ref, kpe_ref, cos_ref, sin_ref, qg_ref, kg_ref,
                     cqn_ref, ckv_out_ref, ckv_bf_ref, kpe_out_ref, kpe_bf_ref):
    cqn_ref[...] = _rms(cq_ref[...], qg_ref[...]).astype(cqn_ref.dtype)
    ckv = _rms(ckv_ref[...], kg_ref[...])
    ckv_out_ref[...] = ckv
    ckv_bf_ref[...] = ckv.astype(ckv_bf_ref.dtype)
    x = kpe_ref[...]
    lane = lax.broadcasted_iota(jnp.int32, x.shape, 1)
    kpe = _rope_lanes(x, cos_ref[...], sin_ref[...], lane)
    kpe_out_ref[...] = kpe
    kpe_bf_ref[...] = kpe.astype(kpe_bf_ref.dtype)


def _mla_prep(yc, cos, sin, q_g, kv_g, *, tm):
    m = yc.shape[0]
    row = lambda width, col_block: pl.BlockSpec((tm, width), lambda i: (i, col_block))
    vec = pl.BlockSpec((1, Q_LORA), lambda i: (0, 0))
    est = 2 * tm * (3 * Q_LORA + 4 * LANES) * 4 * 3
    return pl.pallas_call(
        _mla_prep_kernel,
        grid=(m // tm,),
        in_specs=[row(Q_LORA, 0), row(KV_LORA, 1), row(LANES, C_COL_KPE // LANES), row(LANES, 0),
                  row(LANES, 0), vec, vec],
        out_specs=[row(Q_LORA, 0), row(KV_LORA, 0), row(KV_LORA, 0), row(LANES, 0), row(LANES, 0)],
        out_shape=[
            jax.ShapeDtypeStruct((m, Q_LORA), BF16), jax.ShapeDtypeStruct((m, KV_LORA), F32),
            jax.ShapeDtypeStruct((m, KV_LORA), BF16), jax.ShapeDtypeStruct((m, LANES), F32),
            jax.ShapeDtypeStruct((m, LANES), BF16),
        ],
        compiler_params=_params(("parallel",), est),
        name="mla_prep",
    )(yc, yc, yc, cos, sin, q_g.reshape(1, Q_LORA), kv_g.reshape(1, KV_LORA))


def _mm_qrope_kernel(a_ref, w_ref, cos_ref, sin_ref, o_ref, *, heads):
    acc = _dot(a_ref[...], w_ref[...])
    cos = cos_ref[...]
    sin = sin_ref[...]
    lane = lax.broadcasted_iota(jnp.int32, cos.shape, 1)
    for h in range(heads):
        c0 = h * Q_HEAD_COLS
        o_ref[:, c0:c0 + QK_NOPE] = (acc[:, c0:c0 + QK_NOPE] * MLA_SCALE).astype(o_ref.dtype)
        pe = _rope_lanes(acc[:, c0 + QK_NOPE:c0 + Q_HEAD_COLS], cos, sin, lane)
        o_ref[:, c0 + QK_NOPE:c0 + Q_HEAD_COLS] = (pe * MLA_SCALE).astype(o_ref.dtype)


def _mm_qrope(a, w, cos, sin, *, tm, tn):
    m, k = a.shape
    n = w.shape[1]
    est = 2 * (tm * k * 2 + k * tn * 2 + tm * tn * 2 + 2 * tm * LANES * 4) + 2 * tm * tn * 4
    return pl.pallas_call(
        functools.partial(_mm_qrope_kernel, heads=tn // Q_HEAD_COLS),
        grid=(n // tn, m // tm),
        in_specs=[
            pl.BlockSpec((tm, k), lambda j, i: (i, 0)), pl.BlockSpec((k, tn), lambda j, i: (0, j)),
            pl.BlockSpec((tm, LANES), lambda j, i: (i, 0)), pl.BlockSpec((tm, LANES), lambda j, i: (i, 0)),
        ],
        out_specs=pl.BlockSpec((tm, tn), lambda j, i: (i, j)),
        out_shape=jax.ShapeDtypeStruct((m, n), BF16),
        compiler_params=_params(("parallel", "parallel"), est),
        name="mm_qrope",
    )(a, w, cos, sin)


def _flash_kernel(q_ref, kv_ref, kpe_ref, o_ref, m_ref, l_ref, acc_ref, *, tq, heads):
    qi = pl.program_id(2)
    m_ref[...] = jnp.full(m_ref.shape, -jnp.inf, F32)
    l_ref[...] = jnp.zeros(l_ref.shape, F32)
    acc_ref[...] = jnp.zeros(acc_ref.shape, F32)
    reps = tq // LANES

    def tile(j, diagonal):
        k0 = pl.multiple_of(j * tq, tq)
        kp = kpe_ref[pl.ds(k0, tq), :]
        hd = range(heads)
        cols = [hh * Q_HEAD_COLS for hh in hd]
        ks = [jnp.concatenate([kv_ref[pl.ds(k0, tq), c0:c0 + QK_NOPE], kp], axis=-1) for c0 in cols]
        ss = [lax.dot_general(q_ref[:, c0:c0 + Q_HEAD_COLS], ks[hh], NT, preferred_element_type=F32)
              for hh, c0 in enumerate(cols)]
        if diagonal:
            r = lax.broadcasted_iota(jnp.int32, (tq, tq), 0)
            c = lax.broadcasted_iota(jnp.int32, (tq, tq), 1)
            ss = [jnp.where(c <= r, s, -jnp.inf) for s in ss]
        m_old = [m_ref[hh] for hh in hd]
        m_new = [jnp.maximum(m_old[hh], jnp.max(ss[hh], axis=-1, keepdims=True)) for hh in hd]
        ps = [jnp.exp(ss[hh] - jnp.concatenate([m_new[hh]] * reps, axis=-1)) for hh in hd]
        corr = [jnp.exp(m_old[hh] - m_new[hh]) for hh in hd]
        pv = [lax.dot_general(ps[hh].astype(BF16), kv_ref[pl.ds(k0, tq), c0 + QK_NOPE:c0 + Q_HEAD_COLS], NN,
                              preferred_element_type=F32) for hh, c0 in enumerate(cols)]
        for hh in hd:
            l_ref[hh] = l_ref[hh] * corr[hh] + jnp.sum(ps[hh], axis=-1, keepdims=True)
            acc_ref[hh] = acc_ref[hh] * corr[hh] + pv[hh]
            m_ref[hh] = m_new[hh]

    def body(j, carry):
        tile(j, False)
        return carry

    lax.fori_loop(0, qi, body, 0)
    tile(qi, True)
    for hh in range(heads):
        o_ref[:, hh * V_HEAD:(hh + 1) * V_HEAD] = (acc_ref[hh] / l_ref[hh]).astype(o_ref.dtype)


def _flash(q, kv, kpe_bf, *, n_seq, seq_len, tq, heads):
    nq = seq_len // tq
    qw = heads * Q_HEAD_COLS
    est = 2 * (tq * qw * 2 + seq_len * qw * 2 + seq_len * LANES * 2 + tq * heads * V_HEAD * 2)
    est += 6 * heads * tq * tq * 4
    return pl.pallas_call(
        functools.partial(_flash_kernel, tq=tq, heads=heads),
        grid=(n_seq, MLA_HEADS // heads, nq),
        in_specs=[
            pl.BlockSpec((tq, qw), lambda b, h, i: (b * nq + i, h)),
            pl.BlockSpec((seq_len, qw), lambda b, h, i: (b, h)),
            pl.BlockSpec((seq_len, LANES), lambda b, h, i: (b, 0)),
        ],
        out_specs=pl.BlockSpec((tq, heads * V_HEAD), lambda b, h, i: (b * nq + i, h)),
        out_shape=jax.ShapeDtypeStruct((n_seq * seq_len, MLA_HEADS * V_HEAD), BF16),
        scratch_shapes=[pltpu.VMEM((heads, tq, LANES), F32), pltpu.VMEM((heads, tq, LANES), F32),
                        pltpu.VMEM((heads, tq, V_HEAD), F32)],
        compiler_params=_params(("parallel", "parallel", "arbitrary"), est),
        name="flash_prompt",
    )(q, kv, kpe_bf)


def _head_mm_kernel(a_ref, w_ref, o_ref):
    o_ref[...] = _dot(a_ref[...], w_ref[0]).astype(o_ref.dtype)


def _head_mm(a, w, *, a_width, a_col_stride, out_dtype, name):
    m = a.shape[0]
    heads, k, n = w.shape
    est = 2 * (m * a_width * 2 + k * n * 2 + m * n * 4) + m * n * 4
    return pl.pallas_call(
        _head_mm_kernel,
        grid=(heads,),
        in_specs=[
            pl.BlockSpec((m, a_width), lambda h: (0, h * a_col_stride)),
            pl.BlockSpec((1, k, n), lambda h: (h, 0, 0)),
        ],
        out_specs=pl.BlockSpec((m, n), lambda h: (0, h)),
        out_shape=jax.ShapeDtypeStruct((m, heads * n), out_dtype),
        compiler_params=_params(("parallel",), est),
        name=name,
    )(a, w)


def _paged_kernel(pt_ref, qlat_ref, qpe_ref, ckn_ref, kpn_ref, *rest, pages, seqs, t_new):
    n_in = seqs * pages
    ck_refs = rest[:n_in]
    kp_refs = rest[n_in:2 * n_in]
    o_ref, m_ref, l_ref, acc_ref = rest[2 * n_in:]
    step = pl.program_id(1)

    @pl.when(step == 0)
    def _():
        m_ref[...] = jnp.full(m_ref.shape, -jnp.inf, F32)
        l_ref[...] = jnp.zeros(l_ref.shape, F32)
        acc_ref[...] = jnp.zeros(acc_ref.shape, F32)

    def update(qs, scores, keys):
        m_old = [m_ref[q] for q in qs]
        m_new = [jnp.maximum(m_old[i], jnp.max(scores[i], axis=-1, keepdims=True)) for i in range(len(qs))]
        p = [jnp.exp(scores[i] - m_new[i]) for i in range(len(qs))]
        corr = [jnp.exp(m_old[i] - m_new[i]) for i in range(len(qs))]
        pv = [lax.dot_general(p[i].astype(BF16), keys[i], NN, preferred_element_type=F32) for i in range(len(qs))]
        for i, q in enumerate(qs):
            l_ref[q] = l_ref[q] * corr[i] + jnp.sum(p[i], axis=-1, keepdims=True)
            acc_ref[q] = acc_ref[q] * corr[i] + pv[i]
            m_ref[q] = m_new[i]

    qs = list(range(seqs))
    ql = [qlat_ref[q] for q in qs]
    qp = [qpe_ref[q][:, :QK_ROPE] for q in qs]
    ck = [jnp.concatenate([r[0, 0].astype(BF16) for r in ck_refs[q * pages:(q + 1) * pages]], axis=0) for q in qs]
    kpt = [jnp.concatenate([r[0, 0].astype(BF16) for r in kp_refs[q * pages:(q + 1) * pages]], axis=1) for q in qs]
    s = [lax.dot_general(ql[q], ck[q], NT, preferred_element_type=F32)
         + lax.dot_general(qp[q], kpt[q], NN, preferred_element_type=F32) for q in qs]
    update(qs, s, ck)

    @pl.when(step == pl.num_programs(1) - 1)
    def _():
        rows = ql[0].shape[0]
        pad = PAGE_SIZE - ckn_ref.shape[1]
        ckn = [jnp.concatenate([ckn_ref[q], jnp.zeros((pad, KV_LORA), F32)], axis=0).astype(BF16) for q in qs]
        t_row = lax.broadcasted_iota(jnp.int32, (rows, PAGE_SIZE), 0) // MLA_HEADS
        j_col = lax.broadcasted_iota(jnp.int32, (rows, PAGE_SIZE), 1)
        valid = j_col <= jnp.minimum(t_row, t_new - 1)
        sn = [jnp.where(valid, lax.dot_general(ql[q], ckn[q], NT, preferred_element_type=F32)
                        + lax.dot_general(qp[q], kpn_ref[q].astype(BF16), NN, preferred_element_type=F32),
                        -jnp.inf) for q in qs]
        update(qs, sn, ckn)
        for q in qs:
            o_ref[q] = (acc_ref[q] / l_ref[q]).astype(o_ref.dtype)


def _paged(page_table, qlat, qpe, ckv_new8, kpe_new_t, cache_ckv, cache_kpe_t, *, layer, t_new):
    n_seq, rows, _ = qlat.shape
    n_pages = page_table.shape[1]
    pages, seqs = PAGES_PER_STEP, SEQS_PER_STEP
    steps = n_pages // pages
    pt = page_table.reshape(-1).astype(jnp.int32)
    t_pad = ckv_new8.shape[1]

    def page_spec(shape, q, k):
        return pl.BlockSpec((1, 1) + shape,
                            lambda b, s, pt_ref: (layer, pt_ref[(b * seqs + q) * n_pages + s * pages + k], 0, 0))

    per_seq = lambda d1, d2: pl.BlockSpec((seqs, d1, d2), lambda b, s, pt_ref: (b, 0, 0))
    in_specs = [per_seq(rows, KV_LORA), per_seq(rows, LANES), per_seq(t_pad, KV_LORA), per_seq(QK_ROPE, PAGE_SIZE)]
    in_specs += [page_spec((PAGE_SIZE, KV_LORA), q, k) for q in range(seqs) for k in range(pages)]
    in_specs += [page_spec((QK_ROPE, PAGE_SIZE), q, k) for q in range(seqs) for k in range(pages)]
    n_in = seqs * pages
    est = 2 * n_in * PAGE_SIZE * (KV_LORA + QK_ROPE) * 4 + 3 * n_in * PAGE_SIZE * (KV_LORA + QK_ROPE) * 2
    est += 8 * seqs * rows * (KV_LORA + pages * PAGE_SIZE) * 4
    grid_spec = pltpu.PrefetchScalarGridSpec(
        num_scalar_prefetch=1,
        grid=(n_seq // seqs, steps),
        in_specs=in_specs,
        out_specs=per_seq(rows, KV_LORA),
        scratch_shapes=[pltpu.VMEM((seqs, rows, 1), F32), pltpu.VMEM((seqs, rows, 1), F32),
                        pltpu.VMEM((seqs, rows, KV_LORA), F32)],
    )
    return pl.pallas_call(
        functools.partial(_paged_kernel, pages=pages, seqs=seqs, t_new=t_new),
        grid_spec=grid_spec,
        out_shape=jax.ShapeDtypeStruct((n_seq, rows, KV_LORA), BF16),
        compiler_params=_params(("parallel", "arbitrary"), est),
        name="paged_attn",
    )(pt, qlat, qpe, ckv_new8, kpe_new_t, *([cache_ckv] * n_in), *([cache_kpe_t] * n_in))


def _pad_rows(x, rows_before, rows_total):
    pad = [(0, 0)] * x.ndim
    pad[1] = (rows_before, rows_total - rows_before - x.shape[1])
    return jnp.pad(x, pad)


def _conv_state(buf, x):
    keep = CONV_W - 1
    if x.shape[1] >= keep:
        return x[:, x.shape[1] - keep:]
    return jnp.concatenate([buf.astype(x.dtype), x], axis=1)[:, -keep:]


def _rope_tables(positions):
    half = QK_ROPE // 2
    inv = ROPE_BASE ** (-jnp.arange(half, dtype=F32) / half)
    ang = positions.astype(F32)[:, None] * inv[None, :]
    cos, sin = jnp.cos(ang), jnp.sin(ang)
    zeros = jnp.zeros((positions.shape[0], LANES - QK_ROPE), F32)
    return (jnp.concatenate([cos, cos, zeros], axis=-1), jnp.concatenate([-sin, sin, zeros], axis=-1))


def kernel(x_prompt, x_sample, p_prompt, p_sample, state_gdn, state_gdn_conv, state_lru, state_lru_conv, cache_ckv, cache_kpe, page_table, w_in_a, conv_qkv_w, gdn_a_log, gdn_dt_bias, gdn_norm_g, conv_lru_w, conv_lru_b, lru_wa, lru_ba, lru_wi, lru_bi, lru_lambda, w_out_a, w_in_c, q_norm_g, kv_norm_g, w_uq, w_ukv, w_o_c, norm_mix_g, norm_ffn_g, w_up, w_down, w_ple, w_ple_gate, norm_final_g):
    bp, seq, d = x_prompt.shape
    bs, ts, _ = x_sample.shape
    mp, ms = bp * seq, bs * ts
    m = mp + ms
    depth = norm_mix_g.shape[0]
    past = page_table.shape[1] * PAGE_SIZE
    tm = 512

    h = (x_prompt.reshape(mp, d), x_sample.reshape(ms, d))
    pos = jnp.concatenate([jnp.tile(jnp.arange(seq), bp), jnp.tile(past + jnp.arange(ts), bs)])
    cos, sin = _rope_tables(pos)
    hn = _rmsnorm(h, norm_mix_g[0], tm=tm, out_dtype=BF16)

    gdn_p, gdn_s, gconv_p, gconv_s, lru_p, lru_s, lconv_p, lconv_s = [], [], [], [], [], [], [], []
    ckv_p, ckv_s, kpe_p, kpe_s = [], [], [], []

    for i in range(depth):
        j = i // 2
        if i % 2 == 0:
            wa = w_in_a[j]
            cut_b = GDN_QKV + GDN_WIDTH
            cut_lx = cut_b + 2 * GDN_HEADS
            w_a = jnp.concatenate([
                wa[:, :cut_b], wa[:, cut_lx:], wa[:, cut_b:cut_lx],
                jnp.zeros((d, LANES - 2 * GDN_HEADS), F32)], axis=1).astype(BF16)
            ya = _mm(hn, w_a, tm=tm, tn=A_COLS // 7, out_dtype=F32, name="in_proj_a")
            ya_p = ya.reshape(1, m, A_COLS)
            ya_s = ya[mp:].reshape(bs, ts, A_COLS)

            zero_buf = jnp.zeros((bp, SUBLANES, GDN_QKV), F32)
            parts_p = _gdn_intra(ya_p, zero_buf, conv_qkv_w[j], gdn_a_log[j], gdn_dt_bias[j], flat=True,
                                 n_groups=bp, nt=seq // GDN_CHUNK, bb=1, tl=GDN_CHUNK, l_real=GDN_CHUNK)
            o_gp, s_gp = _gdn_state(*parts_p, jnp.zeros((bp,) + state_gdn.shape[2:], F32), gdn_norm_g[j],
                                    nb=bp, nseg=1, tl=GDN_CHUNK)
            o_lp, h_lp = _lru_prompt(ya_p, jnp.zeros((bp, SUBLANES, LRU_WIDTH), F32),
                                     jnp.zeros((bp, 1, LRU_WIDTH), F32), conv_lru_w[j], conv_lru_b[j],
                                     lru_wa[j].astype(BF16), lru_ba[j].reshape(-1), lru_wi[j].astype(BF16),
                                     lru_bi[j].reshape(-1), lru_lambda[j], n_seq=bp, seq_rows=seq, tl=256)

            t_pad = SUBLANES
            ya_s8 = _pad_rows(ya_s, 0, t_pad)
            buf_g8 = _pad_rows(state_gdn_conv[j], SUBLANES - (CONV_W - 1), SUBLANES)
            seg = GDN_ROWS // t_pad
            parts_s = _gdn_intra(ya_s8, buf_g8, conv_qkv_w[j], gdn_a_log[j], gdn_dt_bias[j], flat=False,
                                 n_groups=bs // seg, nt=1, bb=seg, tl=t_pad, l_real=ts)
            o_gs, s_gs = _gdn_state(*parts_s, state_gdn[j], gdn_norm_g[j], nb=1, nseg=seg, tl=t_pad)
            o_gs = o_gs.reshape(bs, t_pad, GDN_WIDTH)
            lx_tm = ya_s[:, :, A_COL_LX:A_COL_LY].transpose(1, 0, 2)
            ly_tm = ya_s[:, :, A_COL_LY:A_COL_BA].transpose(1, 0, 2)
            o_ls_tm, h_ls = _lru_sample(lx_tm, ly_tm, state_lru_conv[j].transpose(1, 0, 2), state_lru[j],
                                        conv_lru_w[j], conv_lru_b[j], lru_wa[j].astype(BF16),
                                        lru_ba[j].reshape(-1), lru_wi[j].astype(BF16), lru_bi[j].reshape(-1),
                                        lru_lambda[j])
            mix = [(o_gp.reshape(mp, GDN_WIDTH), o_gs[:, :ts].reshape(ms, GDN_WIDTH)),
                   (o_lp.reshape(mp, LRU_WIDTH), o_ls_tm.transpose(1, 0, 2).reshape(ms, LRU_WIDTH))]
            h, hn = _mm(mix, w_out_a, layer=j, single_buffer_w=True, tm=tm // 2, tn=d, out_dtype=F32,
                        epilogue="residual_norm", residual=h, norm_g=norm_ffn_g[i], name="out_proj_a")

            keep = CONV_W - 1
            ya_pp = jnp.stack([ya[(b + 1) * seq - keep:(b + 1) * seq] for b in range(bp)])
            gdn_p.append(s_gp)
            gdn_s.append(s_gs)
            gconv_p.append(ya_pp[:, :, :GDN_QKV])
            gconv_s.append(_conv_state(state_gdn_conv[j], ya_s[:, :, :GDN_QKV]))
            lru_p.append(h_lp.reshape(bp, LRU_WIDTH))
            lru_s.append(h_ls)
            lconv_p.append(ya_pp[:, :, A_COL_LX:A_COL_LY])
            lconv_s.append(_conv_state(state_lru_conv[j], ya_s[:, :, A_COL_LX:A_COL_LY]))
        else:
            w_c = jnp.concatenate([w_in_c[j], jnp.zeros((d, LANES - QK_ROPE), F32)], axis=1).astype(BF16)
            yc = _mm(hn, w_c, tm=tm, tn=C_COLS, out_dtype=F32, name="in_proj_c")
            cqn, ckv, ckv_bf, kpe, kpe_bf = _mla_prep(yc, cos, sin, q_norm_g[j], kv_norm_g[j], tm=tm)
            w_q = w_uq[j].reshape(Q_LORA, MLA_HEADS, QK_NOPE + QK_ROPE)
            w_q = jnp.pad(w_q, ((0, 0), (0, 0), (0, Q_HEAD_COLS - QK_NOPE - QK_ROPE)))
            q = _mm_qrope(cqn, w_q.reshape(Q_LORA, MLA_HEADS * Q_HEAD_COLS).astype(BF16), cos, sin,
                          tm=tm, tn=MLA_HEADS * Q_HEAD_COLS)
            w_r = w_ukv[j].reshape(KV_LORA, MLA_HEADS, QK_NOPE + V_HEAD)

            kv = _mm(ckv_bf[:mp], w_ukv, layer=j, tm=tm, tn=w_ukv.shape[-1], out_dtype=BF16, name="kv_up")
            o_p = _flash(q, kv, kpe_bf, n_seq=bp, seq_len=seq, tq=512, heads=2)

            q_s = q[mp:]
            w_uk_t = w_r[:, :, :QK_NOPE].transpose(1, 2, 0).astype(BF16)
            w_uv = w_r[:, :, QK_NOPE:].transpose(1, 0, 2).astype(BF16)
            qlat = _head_mm(q_s, w_uk_t, a_width=QK_NOPE, a_col_stride=Q_HEAD_COLS // QK_NOPE,
                            out_dtype=BF16, name="q_latent")
            qlat = qlat.reshape(bs, ts * MLA_HEADS, KV_LORA)
            qpe = q_s.reshape(ms, MLA_HEADS, Q_HEAD_COLS)[:, :, QK_NOPE:].reshape(bs, ts * MLA_HEADS, LANES)
            ckv_new8 = _pad_rows(ckv[mp:].reshape(bs, ts, KV_LORA), 0, SUBLANES)
            kpe_new_t = kpe[mp:, :QK_ROPE].reshape(bs, ts, QK_ROPE).transpose(0, 2, 1)
            kpe_new_t = jnp.pad(kpe_new_t, ((0, 0), (0, 0), (0, PAGE_SIZE - ts)))
            lat = _paged(page_table, qlat, qpe, ckv_new8, kpe_new_t, cache_ckv,
                         cache_kpe.transpose(0, 1, 3, 2), layer=j, t_new=ts)
            o_s = _head_mm(lat.reshape(ms, MLA_HEADS * KV_LORA), w_uv, a_width=KV_LORA, a_col_stride=1,
                           out_dtype=BF16, name="v_up")
            h, hn = _mm([(o_p, o_s)], w_o_c, layer=j, single_buffer_w=True, tm=tm, tn=d,
                        out_dtype=F32, epilogue="residual_norm", residual=h, norm_g=norm_ffn_g[i],
                        name="out_proj_c")
            ckv_p.append(ckv[:mp].reshape(bp, seq, KV_LORA))
            ckv_s.append(ckv[mp:].reshape(bs, ts, KV_LORA))
            kpe_p.append(kpe[:mp, :QK_ROPE].reshape(bp, seq, QK_ROPE))
            kpe_s.append(kpe[mp:, :QK_ROPE].reshape(bs, ts, QK_ROPE))

        ff = _mm(hn, w_up, layer=i, tm=tm, tn=2048, out_dtype=BF16, epilogue="relu2", name="ffn_up")
        h = _mm(ff, w_down, layer=i, single_buffer_w=True, tm=tm, tn=512, out_dtype=F32, epilogue="residual",
                residual=h, name="ffn_down")
        p = jnp.concatenate([p_prompt[i].reshape(mp, -1), p_sample[i].reshape(ms, -1)], axis=0)
        if i + 1 < depth:
            h, hn = _ple(h, p, w_ple, w_ple_gate, norm_mix_g[i + 1], layer=i, tm=256, norm_dtype=BF16)
        else:
            y_p, y_s = _ple(h, p, w_ple, w_ple_gate, norm_final_g, layer=i, tm=256, norm_dtype=F32, split_rows=mp)

    return (y_p.reshape(bp, seq, d), y_s.reshape(bs, ts, d),
            jnp.stack(gdn_p), jnp.stack(gdn_s), jnp.stack(gconv_p), jnp.stack(gconv_s),
            jnp.stack(lru_p), jnp.stack(lru_s), jnp.stack(lconv_p), jnp.stack(lconv_s),
            jnp.stack(ckv_p), jnp.stack(ckv_s), jnp.stack(kpe_p), jnp.stack(kpe_s))
```

```python
import functools
import math

import jax
import jax.numpy as jnp
from jax import lax
from jax.experimental import pallas as pl
from jax.experimental.pallas import tpu as pltpu

F32 = jnp.float32
BF16 = jnp.bfloat16

D_MODEL = 2048
CONV_W = 4
GDN_HEADS = 8
GDN_DK = 128
GDN_DV = 128
GDN_QK = GDN_HEADS * GDN_DK
GDN_WIDTH = GDN_HEADS * GDN_DV
GDN_QKV = 2 * GDN_QK + GDN_WIDTH
GDN_CHUNK = 64
LRU_WIDTH = D_MODEL // 2
LRU_BLOCKS = 8
LRU_BW = LRU_WIDTH // LRU_BLOCKS
LRU_C = 8.0
MLA_HEADS = 16
Q_LORA = 512
KV_LORA = 512
QK_NOPE = 128
QK_ROPE = 64
V_HEAD = 128
MLA_SCALE = (QK_NOPE + QK_ROPE) ** -0.5
ROPE_BASE = 10000.0
PAGE_SIZE = 128
EPS = 1e-6

LANES = 128
SUBLANES = 8
VMEM_LIMIT_BYTES = 56 * 1024 * 1024

A_COL_Z = GDN_QKV
A_COL_LX = A_COL_Z + GDN_WIDTH
A_COL_LY = A_COL_LX + LRU_WIDTH
A_COL_BA = A_COL_LY + LRU_WIDTH
A_COLS = A_COL_BA + LANES
C_COL_KPE = Q_LORA + KV_LORA
C_COLS = C_COL_KPE + LANES
Q_HEAD_COLS = 2 * LANES
PAGES_PER_STEP = 8
SEQS_PER_STEP = 4
GDN_ROWS = 64

NN = (((1,), (0,)), ((), ()))
NT = (((1,), (1,)), ((), ()))
TN = (((0,), (0,)), ((), ()))


def _params(semantics, est_bytes):
    limit = int(min(max(2 * est_bytes, 32 * 1024 * 1024), VMEM_LIMIT_BYTES))
    return pltpu.CompilerParams(dimension_semantics=semantics, vmem_limit_bytes=limit)


def _vmem_params(semantics, need_bytes):
    limit = int(min(max(need_bytes + 4 * 1024 * 1024, 32 * 1024 * 1024), VMEM_LIMIT_BYTES))
    return pltpu.CompilerParams(dimension_semantics=semantics, vmem_limit_bytes=limit)


def _nbytes(shape, dtype):
    return math.prod(shape) * jnp.dtype(dtype).itemsize


def _dot(a, b, dims=NN):
    return lax.dot_general(a.astype(BF16), b.astype(BF16), dims, preferred_element_type=F32)


def _dot_hi(a, b, dims=NN):
    return lax.dot_general(a, b, dims, precision=lax.Precision.HIGHEST, preferred_element_type=F32)


def _sigmoid(x):
    return jax.nn.sigmoid(x)


def _softplus(x):
    return jnp.maximum(x, 0.0) + jnp.log1p(jnp.exp(-jnp.abs(x)))


def _rms(x, g):
    return x * lax.rsqrt(jnp.mean(x * x, axis=-1, keepdims=True) + EPS) * g


def _pick(is_prompt, prompt_ref, sample_ref):
    return jnp.where(is_prompt, prompt_ref[...], sample_ref[...])


def _pair_specs(pair, tm, n_prompt_tiles, index):
    prompt, sample = pair
    width = prompt.shape[1]
    return [
        pl.BlockSpec((tm, width), lambda *g: (jnp.minimum(index(*g), n_prompt_tiles - 1), 0)),
        pl.BlockSpec((tm, width), lambda *g: (jnp.maximum(index(*g) - n_prompt_tiles, 0), 0)),
    ]


def _rmsnorm_kernel(xp_ref, xs_ref, g_ref, o_ref, *, n_prompt_tiles):
    x = _pick(pl.program_id(0) < n_prompt_tiles, xp_ref, xs_ref)
    o_ref[...] = _rms(x, g_ref[...]).astype(o_ref.dtype)


def _rmsnorm(x_pair, g, *, tm, out_dtype):
    d = x_pair[0].shape[1]
    m = x_pair[0].shape[0] + x_pair[1].shape[0]
    npt = x_pair[0].shape[0] // tm
    est = 2 * (2 * _nbytes((tm, d), F32) + _nbytes((tm, d), out_dtype)) + 2 * _nbytes((tm, d), F32)
    return pl.pallas_call(
        functools.partial(_rmsnorm_kernel, n_prompt_tiles=npt),
        grid=(m // tm,),
        in_specs=_pair_specs(x_pair, tm, npt, lambda i: i) + [pl.BlockSpec((1, d), lambda i: (0, 0))],
        out_specs=pl.BlockSpec((tm, d), lambda i: (i, 0)),
        out_shape=jax.ShapeDtypeStruct((m, d), out_dtype),
        compiler_params=_params(("arbitrary",), est),
        name="rmsnorm",
    )(*x_pair, g.reshape(1, d))


def _mm_kernel(*refs, a_dual, res_dual, epilogue, cast_w, n_prompt_tiles):
    n_a = sum(2 if dual else 1 for dual in a_dual)
    a_refs, w_ref, refs = list(refs[:n_a]), refs[n_a], list(refs[n_a + 1:])
    is_prompt = pl.program_id(1) < n_prompt_tiles

    def take(src, dual):
        if dual:
            return _pick(is_prompt, src.pop(0), src.pop(0))
        return src.pop(0)[...]

    if cast_w:
        wbf_ref = refs.pop()

        @pl.when(pl.program_id(1) == 0)
        def _():
            wbf_ref[...] = w_ref[0].astype(BF16)

        w = wbf_ref[...]
    else:
        w = w_ref[...]
    cols = [take(a_refs, dual) for dual in a_dual]
    a = cols[0] if len(cols) == 1 else jnp.concatenate(cols, axis=1)
    acc = _dot(a, w)
    if epilogue == "relu2":
        acc = jnp.square(jnp.maximum(acc, 0.0))
    elif epilogue in ("residual", "residual_norm"):
        acc = acc + take(refs, res_dual)
    if epilogue == "residual_norm":
        g_ref, o_ref, n_ref = refs
        n_ref[...] = _rms(acc, g_ref[...]).astype(n_ref.dtype)
    else:
        (o_ref,) = refs
    o_ref[...] = acc.astype(o_ref.dtype)


def _mm(a, w, *, tm, tn, out_dtype, epilogue=None, residual=None, norm_g=None, layer=None,
        single_buffer_w=False, name):
    parts = a if isinstance(a, list) else [a]
    rows = lambda part: sum(x.shape[0] for x in part) if isinstance(part, tuple) else part.shape[0]
    width = lambda part: part[0].shape[1] if isinstance(part, tuple) else part.shape[1]
    a_dtype = parts[0][0].dtype if isinstance(parts[0], tuple) else parts[0].dtype
    m, k = rows(parts[0]), sum(width(part) for part in parts)
    pairs = [x for x in parts + [residual] if isinstance(x, tuple)]
    npt = pairs[0][0].shape[0] // tm if pairs else m // tm
    n = w.shape[-1]
    cast_w = layer is not None
    mode = dict(pipeline_mode=pl.Buffered(1)) if single_buffer_w else {}
    w_bufs = 1 if single_buffer_w else 2
    if cast_w:
        w_spec = pl.BlockSpec((1, k, tn), lambda j, i: (layer, 0, j), **mode)
        w_bytes = w_bufs * _nbytes((k, tn), F32) + _nbytes((k, tn), BF16)
        scratch = [pltpu.VMEM((k, tn), BF16)]
    else:
        w_spec = pl.BlockSpec((k, tn), lambda j, i: (0, j), **mode)
        w_bytes = w_bufs * _nbytes((k, tn), w.dtype)
        scratch = []
    tile = pl.BlockSpec((tm, tn), lambda j, i: (i, j))
    in_specs, args = [], []

    def add(part, spec):
        if isinstance(part, tuple):
            in_specs.extend(_pair_specs(part, tm, npt, lambda j, i: i))
            args.extend(part)
        else:
            in_specs.append(spec)
            args.append(part)

    for part in parts:
        add(part, pl.BlockSpec((tm, width(part)), lambda j, i: (i, 0)))
    in_specs.append(w_spec)
    args.append(w)
    out_specs, out_shape = tile, jax.ShapeDtypeStruct((m, n), out_dtype)
    est = w_bytes + 4 * _nbytes((tm, k), a_dtype) + 2 * _nbytes((tm, tn), out_dtype) + 2 * _nbytes((tm, tn), F32)
    if epilogue in ("residual", "residual_norm"):
        assert not isinstance(residual, tuple) or tn == n
        add(residual, tile)
        est += 4 * _nbytes((tm, tn), F32)
    if epilogue == "residual_norm":
        assert tn == n
        in_specs.append(pl.BlockSpec((1, n), lambda j, i: (0, 0)))
        args.append(norm_g.reshape(1, n))
        out_specs = [tile, tile]
        out_shape = [out_shape, jax.ShapeDtypeStruct((m, n), BF16)]
        est += 2 * _nbytes((tm, tn), BF16) + _nbytes((tm, tn), F32)
    return pl.pallas_call(
        functools.partial(_mm_kernel, a_dual=tuple(isinstance(part, tuple) for part in parts),
                          res_dual=isinstance(residual, tuple), epilogue=epilogue, cast_w=cast_w,
                          n_prompt_tiles=npt),
        grid=(n // tn, m // tm),
        in_specs=in_specs,
        out_specs=out_specs,
        out_shape=out_shape,
        scratch_shapes=scratch,
        compiler_params=_vmem_params(("parallel", "arbitrary"), est),
        name=name,
    )(*args)


def _ple_kernel(h_ref, p_ref, wp_ref, wg_ref, g_ref, *rest, n_prompt_tiles):
    *outs, wp_bf_ref, wg_bf_ref = rest
    i = pl.program_id(0)

    @pl.when(i == 0)
    def _():
        wp_bf_ref[...] = wp_ref[0].astype(BF16)
        wg_bf_ref[...] = wg_ref[0].astype(BF16)

    h = h_ref[...]
    gate = _sigmoid(_dot(h, wg_bf_ref[...]))
    h2 = h + _dot(p_ref[...], wp_bf_ref[...]) * gate
    if n_prompt_tiles is None:
        h_out_ref, n_out_ref = outs
        h_out_ref[...] = h2
        n_out_ref[...] = _rms(h2, g_ref[...]).astype(n_out_ref.dtype)
    else:
        yp_ref, ys_ref = outs
        y = _rms(h2, g_ref[...]).astype(yp_ref.dtype)

        @pl.when(i < n_prompt_tiles)
        def _():
            yp_ref[...] = y

        @pl.when(i >= n_prompt_tiles)
        def _():
            ys_ref[...] = y


def _ple(h, p, wp, wg, g, *, layer, tm, norm_dtype, split_rows=None):
    m, d = h.shape
    pd = p.shape[1]
    est = (_nbytes((pd, d), F32) + _nbytes((d, d), F32) + _nbytes((pd, d), BF16) + _nbytes((d, d), BF16)
           + 2 * (2 * _nbytes((tm, d), F32) + _nbytes((tm, pd), F32) + _nbytes((tm, d), norm_dtype))
           + 3 * _nbytes((tm, d), F32))
    once = dict(pipeline_mode=pl.Buffered(1))
    row = pl.BlockSpec((tm, d), lambda i: (i, 0))
    if split_rows is None:
        npt = None
        out_specs = [row, row]
        out_shape = [jax.ShapeDtypeStruct((m, d), F32), jax.ShapeDtypeStruct((m, d), norm_dtype)]
    else:
        npt = split_rows // tm
        out_specs = [pl.BlockSpec((tm, d), lambda i: (jnp.minimum(i, npt - 1), 0)),
                     pl.BlockSpec((tm, d), lambda i: (jnp.maximum(i - npt, 0), 0))]
        out_shape = [jax.ShapeDtypeStruct((split_rows, d), norm_dtype),
                     jax.ShapeDtypeStruct((m - split_rows, d), norm_dtype)]
    return pl.pallas_call(
        functools.partial(_ple_kernel, n_prompt_tiles=npt),
        grid=(m // tm,),
        in_specs=[
            row,
            pl.BlockSpec((tm, pd), lambda i: (i, 0)),
            pl.BlockSpec((1, pd, d), lambda i: (layer, 0, 0), **once),
            pl.BlockSpec((1, d, d), lambda i: (layer, 0, 0), **once),
            pl.BlockSpec((1, d), lambda i: (0, 0)),
        ],
        out_specs=out_specs,
        out_shape=out_shape,
        scratch_shapes=[pltpu.VMEM((pd, d), BF16), pltpu.VMEM((d, d), BF16)],
        compiler_params=_vmem_params(("arbitrary",), est),
        name="ple",
    )(h, p, wp, wg, g.reshape(1, d))


def _gdn_intra_kernel(qkv_ref, prev_ref, ba_ref, z_ref, buf_ref, cw_ref, alog_ref, dtb_ref,
                      u_ref, w_ref, qg_ref, kd_ref, qk_ref, el_ref, zs_ref, xs_ref, *, bb, tl, l_real):
    rows = bb * tl
    i = pl.program_id(1)
    xs_ref[:, 0:SUBLANES, :] = jnp.where(i == 0, buf_ref[...], prev_ref[...])
    x_in = qkv_ref[...]
    xs_ref[:, SUBLANES:SUBLANES + tl, :] = x_in
    cw = cw_ref[...]
    halo0 = SUBLANES - (CONV_W - 1)
    y = cw[CONV_W - 1:CONV_W, :] * x_in
    for j in range(CONV_W - 1):
        y = y + cw[j:j + 1, :] * xs_ref[:, halo0 + j:halo0 + j + tl, :]
    y = (y * _sigmoid(y)).reshape(rows, GDN_QKV)
    ba = ba_ref[...].reshape(rows, LANES)
    z = z_ref[...].reshape(rows, GDN_WIDTH)
    zs_ref[0] = (z * _sigmoid(z)).astype(zs_ref.dtype)

    shift = int(math.log2(tl))
    ri = lax.broadcasted_iota(jnp.int32, (rows, rows), 0)
    ci = lax.broadcasted_iota(jnp.int32, (rows, rows), 1)
    if bb == 1:
        lower = ri >= ci
        strict = ri > ci
    else:
        same = (ri >> shift) == (ci >> shift)
        lower = same & (ri >= ci)
        strict = same & (ri > ci)
    tril = lower.astype(F32)
    seg_last = (ci == ((ri >> shift) << shift) + (tl - 1)).astype(F32)
    lane = lax.broadcasted_iota(jnp.int32, (rows, LANES), 1)
    beta_all = _sigmoid(ba)
    g_all = -jnp.exp(alog_ref[...]) * _softplus(ba + dtb_ref[...])
    g_all = jnp.where(lane >= GDN_HEADS, g_all, 0.0)
    if l_real < tl:
        r1 = lax.broadcasted_iota(jnp.int32, (rows, 1), 0)
        rowmask = ((r1 & (tl - 1)) < l_real).astype(F32)
        y = y * rowmask
        beta_all = beta_all * rowmask
        g_all = g_all * rowmask
    gc_all = _dot_hi(tril, g_all)
    gl_all = _dot_hi(seg_last, gc_all)
    el_ref[0] = jnp.exp(gl_all)
    eg_all = jnp.exp(gc_all)
    kdf_all = jnp.exp(gl_all - gc_all)
    gct = jnp.concatenate([gc_all, jnp.zeros((LANES - rows, LANES), F32)], axis=0).T
    n_factors = max(1, math.ceil(math.log2(l_real)))

    heads = range(GDN_HEADS)
    lanes = [GDN_HEADS + h for h in heads]
    hsl = [slice(h * GDN_DV, (h + 1) * GDN_DV) for h in heads]
    qs = [y[:, h * GDN_DK:(h + 1) * GDN_DK] for h in heads]
    ks = [y[:, GDN_QK + h * GDN_DK:GDN_QK + (h + 1) * GDN_DK] for h in heads]
    vs = [y[:, 2 * GDN_QK + h * GDN_DV:2 * GDN_QK + (h + 1) * GDN_DV] for h in heads]
    qs = [q * lax.rsqrt(jnp.sum(q * q, axis=-1, keepdims=True) + EPS) * (GDN_DK ** -0.5) for q in qs]
    ks = [k * lax.rsqrt(jnp.sum(k * k, axis=-1, keepdims=True) + EPS) for k in ks]
    decays = [jnp.where(lower, jnp.exp(gc_all[:, lh:lh + 1] - gct[lh:lh + 1, 0:rows]), 0.0) for lh in lanes]
    kbs = [ks[h] * beta_all[:, h:h + 1] for h in heads]
    ps = [-jnp.where(strict, _dot(kbs[h], ks[h], NT) * decays[h], 0.0) for h in heads]
    xs = [jnp.concatenate([vs[h] * beta_all[:, h:h + 1], kbs[h] * eg_all[:, lanes[h]:lanes[h] + 1]], axis=-1)
          for h in heads]
    for h in heads:
        qk = jnp.where(lower, _dot(qs[h], ks[h], NT) * decays[h], 0.0)
        qk_ref[h, 0] = qk.astype(qk_ref.dtype)
        qg_ref[0, :, hsl[h]] = (qs[h] * eg_all[:, lanes[h]:lanes[h] + 1]).astype(qg_ref.dtype)
        kd_ref[0, :, hsl[h]] = (ks[h] * kdf_all[:, lanes[h]:lanes[h] + 1]).astype(kd_ref.dtype)
    for f in range(n_factors):
        xs = [xs[h] + _dot(ps[h], xs[h]) for h in heads]
        if f + 1 < n_factors:
            ps = [_dot(ps[h], ps[h]) for h in heads]
    for h in heads:
        u_ref[0, :, hsl[h]] = xs[h][:, :GDN_DV]
        w_ref[0, :, hsl[h]] = xs[h][:, GDN_DV:].astype(w_ref.dtype)


def _gdn_intra(ya3, buf8, conv_w, a_log, dt_bias, *, flat, n_groups, nt, bb, tl, l_real):
    rows = bb * tl
    assert rows == GDN_ROWS and (flat or nt == 1)

    def spec(width, col_block):
        if flat:
            return pl.BlockSpec((1, rows, width), lambda g, i: (0, g * nt + i, col_block))
        return pl.BlockSpec((bb, tl, width), lambda g, i: (g, 0, col_block))

    if flat:
        prev = pl.BlockSpec((1, SUBLANES, GDN_QKV),
                            lambda g, i: (0, jnp.maximum((g * nt + i) * (rows // SUBLANES) - 1, 0), 0))
    else:
        prev = spec(GDN_QKV, 0)
    lane_pad = jnp.zeros((1, LANES), F32)
    alog = lane_pad.at[0, GDN_HEADS:2 * GDN_HEADS].set(a_log)
    dtb = lane_pad.at[0, GDN_HEADS:2 * GDN_HEADS].set(dt_bias)
    total = nt * rows
    out = lambda width: pl.BlockSpec((1, rows, width), lambda g, i: (g, i, 0))
    shape = lambda width, dt: jax.ShapeDtypeStruct((n_groups, total, width), dt)
    est = 4 * rows * (2 * GDN_QKV + 6 * GDN_WIDTH) * 4
    return pl.pallas_call(
        functools.partial(_gdn_intra_kernel, bb=bb, tl=tl, l_real=l_real),
        grid=(n_groups, nt),
        in_specs=[
            spec(GDN_QKV, 0), prev, spec(LANES, A_COL_BA // LANES), spec(GDN_WIDTH, A_COL_Z // GDN_WIDTH),
            pl.BlockSpec((bb, SUBLANES, GDN_QKV), lambda g, i: (g, 0, 0)),
            pl.BlockSpec((CONV_W, GDN_QKV), lambda g, i: (0, 0)),
            pl.BlockSpec((1, LANES), lambda g, i: (0, 0)),
            pl.BlockSpec((1, LANES), lambda g, i: (0, 0)),
        ],
        out_specs=[
            out(GDN_WIDTH), out(GDN_WIDTH), out(GDN_WIDTH), out(GDN_WIDTH),
            pl.BlockSpec((GDN_HEADS, 1, rows, rows), lambda g, i: (0, g, i, 0)),
            out(LANES), out(GDN_WIDTH),
        ],
        out_shape=[
            shape(GDN_WIDTH, F32), shape(GDN_WIDTH, BF16), shape(GDN_WIDTH, BF16), shape(GDN_WIDTH, BF16),
            jax.ShapeDtypeStruct((GDN_HEADS, n_groups, total, rows), BF16),
            shape(LANES, F32), shape(GDN_WIDTH, BF16),
        ],
        scratch_shapes=[pltpu.VMEM((bb, tl + SUBLANES, GDN_QKV), F32)],
        compiler_params=_params(("parallel", "parallel"), est),
        name="gdn_intra",
    )(ya3, ya3, ya3, ya3, buf8, conv_w, alog, dtb)


def _gdn_state_kernel(u_ref, w_ref, qg_ref, kd_ref, qk_ref, el_ref, zs_ref, s0_ref, ng_ref,
                      o_ref, sout_ref, s_ref, *, nb, nseg, tl):
    i = pl.program_id(1)

    @pl.when(i == 0)
    def _():
        s_ref[...] = s0_ref[...]

    rows = nseg * tl
    seg_of_row = lax.broadcasted_iota(jnp.int32, (rows, 1), 0) >> int(math.log2(tl))
    pairs = [(blk, h) for blk in range(nb) for h in range(GDN_HEADS)]
    hsl = lambda h: slice(h * GDN_DV, (h + 1) * GDN_DV)
    segs = [slice(s * tl, (s + 1) * tl) for s in range(nseg)]
    cat = lambda parts: parts[0] if len(parts) == 1 else jnp.concatenate(parts, axis=0)
    inter = {}
    for blk, h in pairs:
        w = w_ref[blk, :, hsl(h)].astype(F32)
        qg = qg_ref[blk, :, hsl(h)].astype(F32)
        inter[blk, h] = [_dot(jnp.concatenate([w[rs], qg[rs]], axis=0), s_ref[blk * nseg + s, h])
                         for s, rs in enumerate(segs)]
    v_new = {}
    for blk, h in pairs:
        u = u_ref[blk, :, hsl(h)]
        v_new[blk, h] = cat([u[rs] - inter[blk, h][s][:tl] for s, rs in enumerate(segs)])
    for blk, h in pairs:
        o = cat([r[tl:] for r in inter[blk, h]]) + _dot(qk_ref[h, blk], v_new[blk, h])
        on = _rms(o, ng_ref[...]) * zs_ref[blk, :, hsl(h)].astype(F32)
        o_ref[blk, :, hsl(h)] = on.astype(o_ref.dtype)
    for blk, h in pairs:
        kd = kd_ref[blk, :, hsl(h)].astype(F32)
        lh = GDN_HEADS + h
        for s in range(nseg):
            e = el_ref[blk, s * tl:s * tl + 1, lh:lh + 1]
            kds = kd if nseg == 1 else jnp.where(seg_of_row == s, kd, 0.0)
            idx = blk * nseg + s
            s_ref[idx, h] = s_ref[idx, h] * e + _dot(kds, v_new[blk, h], TN)

    @pl.when(i == pl.num_programs(1) - 1)
    def _():
        sout_ref[...] = s_ref[...]


def _gdn_state(u, w, qg, kd, qk, el, zs, s0, norm_g, *, nb, nseg, tl):
    n_groups, total, _ = u.shape
    rows = nseg * tl
    assert rows == GDN_ROWS
    blk = lambda width: pl.BlockSpec((nb, rows, width), lambda g, i: (g, i, 0))
    st = pl.BlockSpec((nb * nseg, GDN_HEADS, GDN_DK, GDN_DV), lambda g, i: (g, 0, 0, 0))
    est = 2 * nb * rows * GDN_WIDTH * 16 + 5 * nb * nseg * GDN_HEADS * GDN_DK * GDN_DV * 4
    return pl.pallas_call(
        functools.partial(_gdn_state_kernel, nb=nb, nseg=nseg, tl=tl),
        grid=(n_groups // nb, total // rows),
        in_specs=[
            blk(GDN_WIDTH), blk(GDN_WIDTH), blk(GDN_WIDTH), blk(GDN_WIDTH),
            pl.BlockSpec((GDN_HEADS, nb, rows, rows), lambda g, i: (0, g, i, 0)),
            blk(LANES), blk(GDN_WIDTH), st,
            pl.BlockSpec((1, GDN_DV), lambda g, i: (0, 0)),
        ],
        out_specs=[blk(GDN_WIDTH), st],
        out_shape=[
            jax.ShapeDtypeStruct((n_groups, total, GDN_WIDTH), BF16),
            jax.ShapeDtypeStruct(s0.shape, F32),
        ],
        scratch_shapes=[pltpu.VMEM((nb * nseg, GDN_HEADS, GDN_DK, GDN_DV), F32)],
        compiler_params=_params(("parallel", "arbitrary"), est),
        name="gdn_state",
    )(u, w, qg, kd, qk, el, zs, s0, norm_g.reshape(1, GDN_DV))


def _lru_gates(xb, wa_ref, ba_ref, wi_ref, bi_ref, lam_ref):
    c = -LRU_C * _softplus(-lam_ref[...])
    sls = [slice(kb * LRU_BW, (kb + 1) * LRU_BW) for kb in range(LRU_BLOCKS)]
    xks = [xb[:, sl] for sl in sls]
    ra = [_dot(xks[kb], wa_ref[kb]) for kb in range(LRU_BLOCKS)]
    ri = [_dot(xks[kb], wi_ref[kb]) for kb in range(LRU_BLOCKS)]
    a_parts, b_parts = [], []
    for kb, sl in enumerate(sls):
        r = _sigmoid(ra[kb] + ba_ref[:, sl])
        ig = _sigmoid(ri[kb] + bi_ref[:, sl])
        log_a = c[:, sl] * r
        a_parts.append(jnp.exp(log_a))
        b_parts.append(jnp.sqrt(1.0 - jnp.exp(2.0 * log_a)) * (ig * xks[kb]))
    return jnp.concatenate(a_parts, axis=-1), jnp.concatenate(b_parts, axis=-1)


def _lru_prompt_kernel(lx_ref, ly_ref, buf_ref, h0_ref, cw_ref, cb_ref, wa_ref, ba_ref, wi_ref, bi_ref,
                       lam_ref, ob_ref, ht_ref, xs_ref, a_ref, b_ref, hs_ref, h_ref, *, tl):
    i = pl.program_id(1)

    @pl.when(i == 0)
    def _():
        h_ref[...] = h0_ref[0]
        xs_ref[0:SUBLANES, :] = buf_ref[0]

    @pl.when(i > 0)
    def _():
        xs_ref[0:SUBLANES, :] = xs_ref[tl:tl + SUBLANES, :]

    xs_ref[SUBLANES:SUBLANES + tl, :] = lx_ref[0]
    cw = cw_ref[...]
    halo0 = SUBLANES - (CONV_W - 1)
    xb = cw[CONV_W - 1:CONV_W, :] * xs_ref[SUBLANES:SUBLANES + tl, :] + cb_ref[...]
    for j in range(CONV_W - 1):
        xb = xb + cw[j:j + 1, :] * xs_ref[halo0 + j:halo0 + j + tl, :]
    a, b = _lru_gates(xb, wa_ref, ba_ref, wi_ref, bi_ref, lam_ref)
    a_ref[...] = a
    b_ref[...] = b

    def body(t, h):
        h = a_ref[pl.ds(t, 1), :] * h + b_ref[pl.ds(t, 1), :]
        hs_ref[pl.ds(t, 1), :] = h
        return h

    h = lax.fori_loop(0, tl, body, h_ref[...], unroll=8)
    h_ref[...] = h
    ob_ref[0] = (hs_ref[...] * jax.nn.gelu(ly_ref[0])).astype(ob_ref.dtype)

    @pl.when(i == pl.num_programs(1) - 1)
    def _():
        ht_ref[0] = h


def _lru_prompt(ya3, buf8, h0, cw, cb, wa, ba, wi, bi, lam, *, n_seq, seq_rows, tl):
    nt = seq_rows // tl
    w = LRU_WIDTH

    def spec(col_block):
        return pl.BlockSpec((1, tl, w), lambda b, i: (0, b * nt + i, col_block))

    vec = pl.BlockSpec((1, w), lambda b, i: (0, 0))
    blk = pl.BlockSpec((LRU_BLOCKS, LRU_BW, LRU_BW), lambda b, i: (0, 0, 0))
    est = 8 * tl * w * 4 + 4 * (tl + SUBLANES) * w * 4 + 4 * LRU_BLOCKS * LRU_BW * LRU_BW * 2
    return pl.pallas_call(
        functools.partial(_lru_prompt_kernel, tl=tl),
        grid=(n_seq, nt),
        in_specs=[
            spec(A_COL_LX // w), spec(A_COL_LY // w),
            pl.BlockSpec((1, SUBLANES, w), lambda b, i: (b, 0, 0)),
            pl.BlockSpec((1, 1, w), lambda b, i: (b, 0, 0)),
            pl.BlockSpec((CONV_W, w), lambda b, i: (0, 0)), vec, blk, vec, blk, vec, vec,
        ],
        out_specs=[
            pl.BlockSpec((1, tl, w), lambda b, i: (b, i, 0)),
            pl.BlockSpec((1, 1, w), lambda b, i: (b, 0, 0)),
        ],
        out_shape=[
            jax.ShapeDtypeStruct((n_seq, seq_rows, w), BF16),
            jax.ShapeDtypeStruct((n_seq, 1, w), F32),
        ],
        scratch_shapes=[
            pltpu.VMEM((tl + SUBLANES, w), F32), pltpu.VMEM((tl, w), F32), pltpu.VMEM((tl, w), F32),
            pltpu.VMEM((tl, w), F32), pltpu.VMEM((1, w), F32),
        ],
        compiler_params=_params(("parallel", "arbitrary"), est),
        name="lru_prompt",
    )(ya3, ya3, buf8, h0, cw, cb.reshape(1, w), wa, ba.reshape(1, w), wi, bi.reshape(1, w),
      lam.reshape(1, w))


def _lru_sample_kernel(lx_ref, ly_ref, buf_ref, h0_ref, cw_ref, cb_ref, wa_ref, ba_ref, wi_ref, bi_ref,
                       lam_ref, ob_ref, ht_ref, *, steps):
    cw = cw_ref[...]
    n_buf = CONV_W - 1

    def tap(t):
        return buf_ref[t] if t < n_buf else lx_ref[t - n_buf]

    h = h0_ref[...]
    for t in range(steps):
        xb = cb_ref[...] + cw[0:1, :] * tap(t)
        for j in range(1, CONV_W):
            xb = xb + cw[j:j + 1, :] * tap(t + j)
        a, b = _lru_gates(xb, wa_ref, ba_ref, wi_ref, bi_ref, lam_ref)
        h = a * h + b
        ob_ref[t] = (h * jax.nn.gelu(ly_ref[t])).astype(ob_ref.dtype)
    ht_ref[...] = h


def _lru_sample(lx_tm, ly_tm, buf_tm, h0, cw, cb, wa, ba, wi, bi, lam):
    steps, n, w = lx_tm.shape
    est = 16 * n * w * 4
    return pl.pallas_call(
        functools.partial(_lru_sample_kernel, steps=steps),
        out_shape=[jax.ShapeDtypeStruct((steps, n, w), BF16), jax.ShapeDtypeStruct((n, w), F32)],
        compiler_params=_params(None, est),
        name="lru_sample",
    )(lx_tm, ly_tm, buf_tm, h0, cw, cb.reshape(1, w), wa, ba.reshape(1, w), wi, bi.reshape(1, w),
      lam.reshape(1, w))


def _rope_lanes(x, cos, sin, lane):
    half = QK_ROPE // 2
    rot = jnp.where(lane % QK_ROPE < half, pltpu.roll(x, LANES - half, 1), pltpu.roll(x, half, 1))
    return x * cos + rot * sin


def _mla_prep_kernel(cq_ref, ckv_ref, kpe_ref, cos_ref, sin_ref, qg_ref, kg_ref,
                     cqn_ref, ckv_out_ref, ckv_bf_ref, kpe_out_ref, kpe_bf_ref):
    cqn_ref[...] = _rms(cq_ref[...], qg_ref[...]).astype(cqn_ref.dtype)
    ckv = _rms(ckv_ref[...], kg_ref[...])
    ckv_out_ref[...] = ckv
    ckv_bf_ref[...] = ckv.astype(ckv_bf_ref.dtype)
    x = kpe_ref[...]
    lane = lax.broadcasted_iota(jnp.int32, x.shape, 1)
    kpe = _rope_lanes(x, cos_ref[...], sin_ref[...], lane)
    kpe_out_ref[...] = kpe
    kpe_bf_ref[...] = kpe.astype(kpe_bf_ref.dtype)


def _mla_prep(yc, cos, sin, q_g, kv_g, *, tm):
    m = yc.shape[0]
    row = lambda width, col_block: pl.BlockSpec((tm, width), lambda i: (i, col_block))
    vec = pl.BlockSpec((1, Q_LORA), lambda i: (0, 0))
    est = 2 * tm * (3 * Q_LORA + 4 * LANES) * 4 * 3
    return pl.pallas_call(
        _mla_prep_kernel,
        grid=(m // tm,),
        in_specs=[row(Q_LORA, 0), row(KV_LORA, 1), row(LANES, C_COL_KPE // LANES), row(LANES, 0),
                  row(LANES, 0), vec, vec],
        out_specs=[row(Q_LORA, 0), row(KV_LORA, 0), row(KV_LORA, 0), row(LANES, 0), row(LANES, 0)],
        out_shape=[
            jax.ShapeDtypeStruct((m, Q_LORA), BF16), jax.ShapeDtypeStruct((m, KV_LORA), F32),
            jax.ShapeDtypeStruct((m, KV_LORA), BF16), jax.ShapeDtypeStruct((m, LANES), F32),
            jax.ShapeDtypeStruct((m, LANES), BF16),
        ],
        compiler_params=_params(("parallel",), est),
        name="mla_prep",
    )(yc, yc, yc, cos, sin, q_g.reshape(1, Q_LORA), kv_g.reshape(1, KV_LORA))


def _mm_qrope_kernel(a_ref, w_ref, cos_ref, sin_ref, o_ref, *, heads):
    acc = _dot(a_ref[...], w_ref[...])
    cos = cos_ref[...]
    sin = sin_ref[...]
    lane = lax.broadcasted_iota(jnp.int32, cos.shape, 1)
    for h in range(heads):
        c0 = h * Q_HEAD_COLS
        o_ref[:, c0:c0 + QK_NOPE] = (acc[:, c0:c0 + QK_NOPE] * MLA_SCALE).astype(o_ref.dtype)
        pe = _rope_lanes(acc[:, c0 + QK_NOPE:c0 + Q_HEAD_COLS], cos, sin, lane)
        o_ref[:, c0 + QK_NOPE:c0 + Q_HEAD_COLS] = (pe * MLA_SCALE).astype(o_ref.dtype)


def _mm_qrope(a, w, cos, sin, *, tm, tn):
    m, k = a.shape
    n = w.shape[1]
    est = 2 * (tm * k * 2 + k * tn * 2 + tm * tn * 2 + 2 * tm * LANES * 4) + 2 * tm * tn * 4
    return pl.pallas_call(
        functools.partial(_mm_qrope_kernel, heads=tn // Q_HEAD_COLS),
        grid=(n // tn, m // tm),
        in_specs=[
            pl.BlockSpec((tm, k), lambda j, i: (i, 0)), pl.BlockSpec((k, tn), lambda j, i: (0, j)),
            pl.BlockSpec((tm, LANES), lambda j, i: (i, 0)), pl.BlockSpec((tm, LANES), lambda j, i: (i, 0)),
        ],
        out_specs=pl.BlockSpec((tm, tn), lambda j, i: (i, j)),
        out_shape=jax.ShapeDtypeStruct((m, n), BF16),
        compiler_params=_params(("parallel", "parallel"), est),
        name="mm_qrope",
    )(a, w, cos, sin)


def _flash_kernel(q_ref, kv_ref, kpe_ref, o_ref, m_ref, l_ref, acc_ref, *, tq, heads):
    qi = pl.program_id(2)
    m_ref[...] = jnp.full(m_ref.shape, -jnp.inf, F32)
    l_ref[...] = jnp.zeros(l_ref.shape, F32)
    acc_ref[...] = jnp.zeros(acc_ref.shape, F32)
    reps = tq // LANES

    def tile(j, diagonal):
        k0 = pl.multiple_of(j * tq, tq)
        kp = kpe_ref[pl.ds(k0, tq), :]
        hd = range(heads)
        cols = [hh * Q_HEAD_COLS for hh in hd]
        ks = [jnp.concatenate([kv_ref[pl.ds(k0, tq), c0:c0 + QK_NOPE], kp], axis=-1) for c0 in cols]
        ss = [lax.dot_general(q_ref[:, c0:c0 + Q_HEAD_COLS], ks[hh], NT, preferred_element_type=F32)
              for hh, c0 in enumerate(cols)]
        if diagonal:
            r = lax.broadcasted_iota(jnp.int32, (tq, tq), 0)
            c = lax.broadcasted_iota(jnp.int32, (tq, tq), 1)
            ss = [jnp.where(c <= r, s, -jnp.inf) for s in ss]
        m_old = [m_ref[hh] for hh in hd]
        m_new = [jnp.maximum(m_old[hh], jnp.max(ss[hh], axis=-1, keepdims=True)) for hh in hd]
        ps = [jnp.exp(ss[hh] - jnp.concatenate([m_new[hh]] * reps, axis=-1)) for hh in hd]
        corr = [jnp.exp(m_old[hh] - m_new[hh]) for hh in hd]
        pv = [lax.dot_general(ps[hh].astype(BF16), kv_ref[pl.ds(k0, tq), c0 + QK_NOPE:c0 + Q_HEAD_COLS], NN,
                              preferred_element_type=F32) for hh, c0 in enumerate(cols)]
        for hh in hd:
            l_ref[hh] = l_ref[hh] * corr[hh] + jnp.sum(ps[hh], axis=-1, keepdims=True)
            acc_ref[hh] = acc_ref[hh] * corr[hh] + pv[hh]
            m_ref[hh] = m_new[hh]

    def body(j, carry):
        tile(j, False)
        return carry

    lax.fori_loop(0, qi, body, 0)
    tile(qi, True)
    for hh in range(heads):
        o_ref[:, hh * V_HEAD:(hh + 1) * V_HEAD] = (acc_ref[hh] / l_ref[hh]).astype(o_ref.dtype)


def _flash(q, kv, kpe_bf, *, n_seq, seq_len, tq, heads):
    nq = seq_len // tq
    qw = heads * Q_HEAD_COLS
    est = 2 * (tq * qw * 2 + seq_len * qw * 2 + seq_len * LANES * 2 + tq * heads * V_HEAD * 2)
    est += 6 * heads * tq * tq * 4
    return pl.pallas_call(
        functools.partial(_flash_kernel, tq=tq, heads=heads),
        grid=(n_seq, MLA_HEADS // heads, nq),
        in_specs=[
            pl.BlockSpec((tq, qw), lambda b, h, i: (b * nq + i, h)),
            pl.BlockSpec((seq_len, qw), lambda b, h, i: (b, h)),
            pl.BlockSpec((seq_len, LANES), lambda b, h, i: (b, 0)),
        ],
        out_specs=pl.BlockSpec((tq, heads * V_HEAD), lambda b, h, i: (b * nq + i, h)),
        out_shape=jax.ShapeDtypeStruct((n_seq * seq_len, MLA_HEADS * V_HEAD), BF16),
        scratch_shapes=[pltpu.VMEM((heads, tq, LANES), F32), pltpu.VMEM((heads, tq, LANES), F32),
                        pltpu.VMEM((heads, tq, V_HEAD), F32)],
        compiler_params=_params(("parallel", "parallel", "arbitrary"), est),
        name="flash_prompt",
    )(q, kv, kpe_bf)


def _head_mm_kernel(a_ref, w_ref, o_ref):
    o_ref[...] = _dot(a_ref[...], w_ref[0]).astype(o_ref.dtype)


def _head_mm(a, w, *, a_width, a_col_stride, out_dtype, name):
    m = a.shape[0]
    heads, k, n = w.shape
    est = 2 * (m * a_width * 2 + k * n * 2 + m * n * 4) + m * n * 4
    return pl.pallas_call(
        _head_mm_kernel,
        grid=(heads,),
        in_specs=[
            pl.BlockSpec((m, a_width), lambda h: (0, h * a_col_stride)),
            pl.BlockSpec((1, k, n), lambda h: (h, 0, 0)),
        ],
        out_specs=pl.BlockSpec((m, n), lambda h: (0, h)),
        out_shape=jax.ShapeDtypeStruct((m, heads * n), out_dtype),
        compiler_params=_params(("parallel",), est),
        name=name,
    )(a, w)


def _paged_kernel(pt_ref, qlat_ref, qpe_ref, ckn_ref, kpn_ref, ck_hbm, kp_hbm, o_ref,
                  ck_buf, kp_buf, sems, m_ref, l_ref, acc_ref, *, pages, seqs, t_new, layer, n_pages, n_steps):
    step = pl.program_id(1)
    t = pl.program_id(0) * n_steps + step
    last_t = pl.num_programs(0) * n_steps - 1
    slot = lax.rem(t, 2)
    qs = list(range(seqs))

    def page_copies(tt, sl, q):
        base = ((tt // n_steps) * seqs + q) * n_pages + lax.rem(tt, n_steps) * pages
        out = []
        for k in range(pages):
            page = pt_ref[base + k]
            out.append(pltpu.make_async_copy(ck_hbm.at[layer, page], ck_buf.at[sl, q * pages + k], sems.at[0, sl]))
            out.append(pltpu.make_async_copy(kp_hbm.at[layer, page], kp_buf.at[sl, q * pages + k], sems.at[1, sl]))
        return out

    @pl.when(t == 0)
    def _():
        for q in qs:
            for c in page_copies(t, slot, q):
                c.start()

    @pl.when(step == 0)
    def _():
        m_ref[...] = jnp.full(m_ref.shape, -jnp.inf, F32)
        l_ref[...] = jnp.zeros(l_ref.shape, F32)
        acc_ref[...] = jnp.zeros(acc_ref.shape, F32)

    for q in qs:
        for c in page_copies(t, slot, q):
            c.wait()

    def update(scores, keys):
        m_old = [m_ref[q] for q in qs]
        m_new = [jnp.maximum(m_old[q], jnp.max(scores[q], axis=-1, keepdims=True)) for q in qs]
        p = [jnp.exp(scores[q] - m_new[q]) for q in qs]
        corr = [jnp.exp(m_old[q] - m_new[q]) for q in qs]
        pv = [lax.dot_general(p[q].astype(BF16), keys[q], NN, preferred_element_type=F32) for q in qs]
        for q in qs:
            l_ref[q] = l_ref[q] * corr[q] + jnp.sum(p[q], axis=-1, keepdims=True)
            acc_ref[q] = acc_ref[q] * corr[q] + pv[q]
            m_ref[q] = m_new[q]

    t_next = jnp.minimum(t + 1, last_t)
    ql = [qlat_ref[q] for q in qs]
    qp = [qpe_ref[q][:, :QK_ROPE] for q in qs]
    ck, kpt = [], []
    for q in qs:
        ck.append(jnp.concatenate([ck_buf[slot, q * pages + k].astype(BF16) for k in range(pages)], axis=0))
        kpt.append(jnp.concatenate([kp_buf[slot, q * pages + k].astype(BF16) for k in range(pages)], axis=1))
        for c in page_copies(t_next, 1 - slot, q):
            c.start()
    s = [lax.dot_general(ql[q], ck[q], NT, preferred_element_type=F32)
         + lax.dot_general(qp[q], kpt[q], NN, preferred_element_type=F32) for q in qs]
    update(s, ck)

    @pl.when(step == n_steps - 1)
    def _():
        rows = ql[0].shape[0]
        pad = PAGE_SIZE - ckn_ref.shape[1]
        ckn = [jnp.concatenate([ckn_ref[q], jnp.zeros((pad, KV_LORA), F32)], axis=0).astype(BF16) for q in qs]
        t_row = lax.broadcasted_iota(jnp.int32, (rows, PAGE_SIZE), 0) // MLA_HEADS
        j_col = lax.broadcasted_iota(jnp.int32, (rows, PAGE_SIZE), 1)
        valid = j_col <= jnp.minimum(t_row, t_new - 1)
        sn = [jnp.where(valid, lax.dot_general(ql[q], ckn[q], NT, preferred_element_type=F32)
                        + lax.dot_general(qp[q], kpn_ref[q].astype(BF16), NN, preferred_element_type=F32),
                        -jnp.inf) for q in qs]
        update(sn, ckn)
        for q in qs:
            o_ref[q] = (acc_ref[q] / l_ref[q]).astype(o_ref.dtype)

    @pl.when(t == last_t)
    def _():
        for q in qs:
            for c in page_copies(t_next, 1 - slot, q):
                c.wait()


def _paged(page_table, qlat, qpe, ckv_new8, kpe_new_t, cache_ckv, cache_kpe_t, *, layer, t_new):
    n_seq, rows, _ = qlat.shape
    n_pages = page_table.shape[1]
    pages, seqs = PAGES_PER_STEP, SEQS_PER_STEP
    steps = n_pages // pages
    pt = page_table.reshape(-1).astype(jnp.int32)
    t_pad = ckv_new8.shape[1]
    per_seq = lambda d1, d2: pl.BlockSpec((seqs, d1, d2), lambda b, s, pt_ref: (b, 0, 0))
    n_buf = seqs * pages
    est = (2 * n_buf * PAGE_SIZE * (KV_LORA + QK_ROPE) * 4 + 2 * n_buf * PAGE_SIZE * (KV_LORA + QK_ROPE) * 2
           + 8 * seqs * rows * (KV_LORA + pages * PAGE_SIZE) * 4)
    grid_spec = pltpu.PrefetchScalarGridSpec(
        num_scalar_prefetch=1,
        grid=(n_seq // seqs, steps),
        in_specs=[per_seq(rows, KV_LORA), per_seq(rows, LANES), per_seq(t_pad, KV_LORA),
                  per_seq(QK_ROPE, PAGE_SIZE), pl.BlockSpec(memory_space=pl.ANY),
                  pl.BlockSpec(memory_space=pl.ANY)],
        out_specs=per_seq(rows, KV_LORA),
        scratch_shapes=[pltpu.VMEM((2, n_buf, PAGE_SIZE, KV_LORA), F32),
                        pltpu.VMEM((2, n_buf, QK_ROPE, PAGE_SIZE), F32),
                        pltpu.SemaphoreType.DMA((2, 2)),
                        pltpu.VMEM((seqs, rows, 1), F32), pltpu.VMEM((seqs, rows, 1), F32),
                        pltpu.VMEM((seqs, rows, KV_LORA), F32)],
    )
    return pl.pallas_call(
        functools.partial(_paged_kernel, pages=pages, seqs=seqs, t_new=t_new, layer=layer, n_pages=n_pages,
                          n_steps=steps),
        grid_spec=grid_spec,
        out_shape=jax.ShapeDtypeStruct((n_seq, rows, KV_LORA), BF16),
        compiler_params=_vmem_params(("arbitrary", "arbitrary"), est),
        name="paged_attn",
    )(pt, qlat, qpe, ckv_new8, kpe_new_t, cache_ckv, cache_kpe_t)


def _pad_rows(x, rows_before, rows_total):
    pad = [(0, 0)] * x.ndim
    pad[1] = (rows_before, rows_total - rows_before - x.shape[1])
    return jnp.pad(x, pad)


def _conv_state(buf, x):
    keep = CONV_W - 1
    if x.shape[1] >= keep:
        return x[:, x.shape[1] - keep:]
    return jnp.concatenate([buf.astype(x.dtype), x], axis=1)[:, -keep:]


def _rope_tables(positions):
    half = QK_ROPE // 2
    inv = ROPE_BASE ** (-jnp.arange(half, dtype=F32) / half)
    ang = positions.astype(F32)[:, None] * inv[None, :]
    cos, sin = jnp.cos(ang), jnp.sin(ang)
    zeros = jnp.zeros((positions.shape[0], LANES - QK_ROPE), F32)
    return (jnp.concatenate([cos, cos, zeros], axis=-1), jnp.concatenate([-sin, sin, zeros], axis=-1))


def kernel(x_prompt, x_sample, p_prompt, p_sample, state_gdn, state_gdn_conv, state_lru, state_lru_conv, cache_ckv, cache_kpe, page_table, w_in_a, conv_qkv_w, gdn_a_log, gdn_dt_bias, gdn_norm_g, conv_lru_w, conv_lru_b, lru_wa, lru_ba, lru_wi, lru_bi, lru_lambda, w_out_a, w_in_c, q_norm_g, kv_norm_g, w_uq, w_ukv, w_o_c, norm_mix_g, norm_ffn_g, w_up, w_down, w_ple, w_ple_gate, norm_final_g):
    bp, seq, d = x_prompt.shape
    bs, ts, _ = x_sample.shape
    mp, ms = bp * seq, bs * ts
    m = mp + ms
    depth = norm_mix_g.shape[0]
    past = page_table.shape[1] * PAGE_SIZE
    tm = 512

    h = (x_prompt.reshape(mp, d), x_sample.reshape(ms, d))
    pos = jnp.concatenate([jnp.tile(jnp.arange(seq), bp), jnp.tile(past + jnp.arange(ts), bs)])
    cos, sin = _rope_tables(pos)
    hn = _rmsnorm(h, norm_mix_g[0], tm=tm, out_dtype=BF16)

    gdn_p, gdn_s, gconv_p, gconv_s, lru_p, lru_s, lconv_p, lconv_s = [], [], [], [], [], [], [], []
    ckv_p, ckv_s, kpe_p, kpe_s = [], [], [], []

    for i in range(depth):
        j = i // 2
        if i % 2 == 0:
            wa = w_in_a[j]
            cut_b = GDN_QKV + GDN_WIDTH
            cut_lx = cut_b + 2 * GDN_HEADS
            w_a = jnp.concatenate([
                wa[:, :cut_b], wa[:, cut_lx:], wa[:, cut_b:cut_lx],
                jnp.zeros((d, LANES - 2 * GDN_HEADS), F32)], axis=1).astype(BF16)
            ya = _mm(hn, w_a, single_buffer_w=True, tm=tm, tn=A_COLS, out_dtype=F32, name="in_proj_a")
            ya_p = ya.reshape(1, m, A_COLS)
            ya_s = ya[mp:].reshape(bs, ts, A_COLS)

            zero_buf = jnp.zeros((bp, SUBLANES, GDN_QKV), F32)
            parts_p = _gdn_intra(ya_p, zero_buf, conv_qkv_w[j], gdn_a_log[j], gdn_dt_bias[j], flat=True,
                                 n_groups=bp, nt=seq // GDN_CHUNK, bb=1, tl=GDN_CHUNK, l_real=GDN_CHUNK)
            o_gp, s_gp = _gdn_state(*parts_p, jnp.zeros((bp,) + state_gdn.shape[2:], F32), gdn_norm_g[j],
                                    nb=bp, nseg=1, tl=GDN_CHUNK)
            o_lp, h_lp = _lru_prompt(ya_p, jnp.zeros((bp, SUBLANES, LRU_WIDTH), F32),
                                     jnp.zeros((bp, 1, LRU_WIDTH), F32), conv_lru_w[j], conv_lru_b[j],
                                     lru_wa[j].astype(BF16), lru_ba[j].reshape(-1), lru_wi[j].astype(BF16),
                                     lru_bi[j].reshape(-1), lru_lambda[j], n_seq=bp, seq_rows=seq, tl=256)

            t_pad = SUBLANES
            ya_s8 = _pad_rows(ya_s, 0, t_pad)
            buf_g8 = _pad_rows(state_gdn_conv[j], SUBLANES - (CONV_W - 1), SUBLANES)
            seg = GDN_ROWS // t_pad
            parts_s = _gdn_intra(ya_s8, buf_g8, conv_qkv_w[j], gdn_a_log[j], gdn_dt_bias[j], flat=False,
                                 n_groups=bs // seg, nt=1, bb=seg, tl=t_pad, l_real=ts)
            o_gs, s_gs = _gdn_state(*parts_s, state_gdn[j], gdn_norm_g[j], nb=1, nseg=seg, tl=t_pad)
            o_gs = o_gs.reshape(bs, t_pad, GDN_WIDTH)
            lx_tm = ya_s[:, :, A_COL_LX:A_COL_LY].transpose(1, 0, 2)
            ly_tm = ya_s[:, :, A_COL_LY:A_COL_BA].transpose(1, 0, 2)
            o_ls_tm, h_ls = _lru_sample(lx_tm, ly_tm, state_lru_conv[j].transpose(1, 0, 2), state_lru[j],
                                        conv_lru_w[j], conv_lru_b[j], lru_wa[j].astype(BF16),
                                        lru_ba[j].reshape(-1), lru_wi[j].astype(BF16), lru_bi[j].reshape(-1),
                                        lru_lambda[j])
            mix = [(o_gp.reshape(mp, GDN_WIDTH), o_gs[:, :ts].reshape(ms, GDN_WIDTH)),
                   (o_lp.reshape(mp, LRU_WIDTH), o_ls_tm.transpose(1, 0, 2).reshape(ms, LRU_WIDTH))]
            h, hn = _mm(mix, w_out_a, layer=j, single_buffer_w=True, tm=tm // 2, tn=d, out_dtype=F32,
                        epilogue="residual_norm", residual=h, norm_g=norm_ffn_g[i], name="out_proj_a")

            keep = CONV_W - 1
            ya_pp = jnp.stack([ya[(b + 1) * seq - keep:(b + 1) * seq] for b in range(bp)])
            gdn_p.append(s_gp)
            gdn_s.append(s_gs)
            gconv_p.append(ya_pp[:, :, :GDN_QKV])
            gconv_s.append(_conv_state(state_gdn_conv[j], ya_s[:, :, :GDN_QKV]))
            lru_p.append(h_lp.reshape(bp, LRU_WIDTH))
            lru_s.append(h_ls)
            lconv_p.append(ya_pp[:, :, A_COL_LX:A_COL_LY])
            lconv_s.append(_conv_state(state_lru_conv[j], ya_s[:, :, A_COL_LX:A_COL_LY]))
        else:
            w_c = jnp.concatenate([w_in_c[j], jnp.zeros((d, LANES - QK_ROPE), F32)], axis=1).astype(BF16)
            yc = _mm(hn, w_c, tm=tm, tn=C_COLS, out_dtype=F32, name="in_proj_c")
            cqn, ckv, ckv_bf, kpe, kpe_bf = _mla_prep(yc, cos, sin, q_norm_g[j], kv_norm_g[j], tm=tm)
            w_q = w_uq[j].reshape(Q_LORA, MLA_HEADS, QK_NOPE + QK_ROPE)
            w_q = jnp.pad(w_q, ((0, 0), (0, 0), (0, Q_HEAD_COLS - QK_NOPE - QK_ROPE)))
            q = _mm_qrope(cqn, w_q.reshape(Q_LORA, MLA_HEADS * Q_HEAD_COLS).astype(BF16), cos, sin,
                          tm=tm, tn=MLA_HEADS * Q_HEAD_COLS)
            w_r = w_ukv[j].reshape(KV_LORA, MLA_HEADS, QK_NOPE + V_HEAD)

            kv = _mm(ckv_bf[:mp], w_ukv, layer=j, tm=tm, tn=w_ukv.shape[-1], out_dtype=BF16, name="kv_up")
            o_p = _flash(q, kv, kpe_bf, n_seq=bp, seq_len=seq, tq=512, heads=2)

            q_s = q[mp:]
            w_uk_t = w_r[:, :, :QK_NOPE].transpose(1, 2, 0).astype(BF16)
            w_uv = w_r[:, :, QK_NOPE:].transpose(1, 0, 2).astype(BF16)
            qlat = _head_mm(q_s, w_uk_t, a_width=QK_NOPE, a_col_stride=Q_HEAD_COLS // QK_NOPE,
                            out_dtype=BF16, name="q_latent")
            qlat = qlat.reshape(bs, ts * MLA_HEADS, KV_LORA)
            qpe = q_s.reshape(ms, MLA_HEADS, Q_HEAD_COLS)[:, :, QK_NOPE:].reshape(bs, ts * MLA_HEADS, LANES)
            ckv_new8 = _pad_rows(ckv[mp:].reshape(bs, ts, KV_LORA), 0, SUBLANES)
            kpe_new_t = kpe[mp:, :QK_ROPE].reshape(bs, ts, QK_ROPE).transpose(0, 2, 1)
            kpe_new_t = jnp.pad(kpe_new_t, ((0, 0), (0, 0), (0, PAGE_SIZE - ts)))
            lat = _paged(page_table, qlat, qpe, ckv_new8, kpe_new_t, cache_ckv,
                         cache_kpe.transpose(0, 1, 3, 2), layer=j, t_new=ts)
            o_s = _head_mm(lat.reshape(ms, MLA_HEADS * KV_LORA), w_uv, a_width=KV_LORA, a_col_stride=1,
                           out_dtype=BF16, name="v_up")
            h, hn = _mm([(o_p, o_s)], w_o_c, layer=j, single_buffer_w=True, tm=tm, tn=d,
                        out_dtype=F32, epilogue="residual_norm", residual=h, norm_g=norm_ffn_g[i],
                        name="out_proj_c")
            ckv_p.append(ckv[:mp].reshape(bp, seq, KV_LORA))
            ckv_s.append(ckv[mp:].reshape(bs, ts, KV_LORA))
            kpe_p.append(kpe[:mp, :QK_ROPE].reshape(bp, seq, QK_ROPE))
            kpe_s.append(kpe[mp:, :QK_ROPE].reshape(bs, ts, QK_ROPE))

        ff = _mm(hn, w_up, layer=i, tm=tm, tn=2048, out_dtype=BF16, epilogue="relu2", name="ffn_up")
        h = _mm(ff, w_down, layer=i, single_buffer_w=True, tm=tm, tn=512, out_dtype=F32, epilogue="residual",
                residual=h, name="ffn_down")
        p = jnp.concatenate([p_prompt[i].reshape(mp, -1), p_sample[i].reshape(ms, -1)], axis=0)
        if i + 1 < depth:
            h, hn = _ple(h, p, w_ple, w_ple_gate, norm_mix_g[i + 1], layer=i, tm=256, norm_dtype=BF16)
        else:
            y_p, y_s = _ple(h, p, w_ple, w_ple_gate, norm_final_g, layer=i, tm=256, norm_dtype=F32, split_rows=mp)

    return (y_p.reshape(bp, seq, d), y_s.reshape(bs, ts, d),
            jnp.stack(gdn_p), jnp.stack(gdn_s), jnp.stack(gconv_p), jnp.stack(gconv_s),
            jnp.stack(lru_p), jnp.stack(lru_s), jnp.stack(lconv_p), jnp.stack(lconv_s),
            jnp.stack(ckv_p), jnp.stack(ckv_s), jnp.stack(kpe_p), jnp.stack(kpe_s))
```

```python
import functools
import math

import jax
import jax.numpy as jnp
from jax import lax
from jax.experimental import pallas as pl
from jax.experimental.pallas import tpu as pltpu

F32 = jnp.float32
BF16 = jnp.bfloat16

D_MODEL = 2048
CONV_W = 4
GDN_HEADS = 8
GDN_DK = 128
GDN_DV = 128
GDN_QK = GDN_HEADS * GDN_DK
GDN_WIDTH = GDN_HEADS * GDN_DV
GDN_QKV = 2 * GDN_QK + GDN_WIDTH
GDN_CHUNK = 64
LRU_WIDTH = D_MODEL // 2
LRU_BLOCKS = 8
LRU_BW = LRU_WIDTH // LRU_BLOCKS
LRU_C = 8.0
MLA_HEADS = 16
Q_LORA = 512
KV_LORA = 512
QK_NOPE = 128
QK_ROPE = 64
V_HEAD = 128
MLA_SCALE = (QK_NOPE + QK_ROPE) ** -0.5
ROPE_BASE = 10000.0
PAGE_SIZE = 128
EPS = 1e-6

LANES = 128
SUBLANES = 8
VMEM_LIMIT_BYTES = 56 * 1024 * 1024

A_COL_Z = GDN_QKV
A_COL_LX = A_COL_Z + GDN_WIDTH
A_COL_LY = A_COL_LX + LRU_WIDTH
A_COL_BA = A_COL_LY + LRU_WIDTH
A_COLS = A_COL_BA + LANES
C_COL_KPE = Q_LORA + KV_LORA
C_COLS = C_COL_KPE + LANES
Q_HEAD_COLS = 2 * LANES
PAGES_PER_STEP = 8
SEQS_PER_STEP = 4
GDN_ROWS = 64

NN = (((1,), (0,)), ((), ()))
NT = (((1,), (1,)), ((), ()))
TN = (((0,), (0,)), ((), ()))


def _params(semantics, est_bytes):
    limit = int(min(max(2 * est_bytes, 32 * 1024 * 1024), VMEM_LIMIT_BYTES))
    return pltpu.CompilerParams(dimension_semantics=semantics, vmem_limit_bytes=limit)


def _vmem_params(semantics, need_bytes):
    limit = int(min(max(need_bytes + 4 * 1024 * 1024, 32 * 1024 * 1024), VMEM_LIMIT_BYTES))
    return pltpu.CompilerParams(dimension_semantics=semantics, vmem_limit_bytes=limit)


def _nbytes(shape, dtype):
    return math.prod(shape) * jnp.dtype(dtype).itemsize


def _dot(a, b, dims=NN):
    return lax.dot_general(a.astype(BF16), b.astype(BF16), dims, preferred_element_type=F32)


def _dot_hi(a, b, dims=NN):
    return lax.dot_general(a, b, dims, precision=lax.Precision.HIGHEST, preferred_element_type=F32)


def _sigmoid(x):
    return jax.nn.sigmoid(x)


def _softplus(x):
    return jnp.maximum(x, 0.0) + jnp.log1p(jnp.exp(-jnp.abs(x)))


def _rms(x, g):
    return x * lax.rsqrt(jnp.mean(x * x, axis=-1, keepdims=True) + EPS) * g


def _pick(is_prompt, prompt_ref, sample_ref):
    return jnp.where(is_prompt, prompt_ref[...], sample_ref[...])


def _pair_specs(pair, tm, n_prompt_tiles, index):
    prompt, sample = pair
    width = prompt.shape[1]
    return [
        pl.BlockSpec((tm, width), lambda *g: (jnp.minimum(index(*g), n_prompt_tiles - 1), 0)),
        pl.BlockSpec((tm, width), lambda *g: (jnp.maximum(index(*g) - n_prompt_tiles, 0), 0)),
    ]


def _rmsnorm_kernel(xp_ref, xs_ref, g_ref, o_ref, *, n_prompt_tiles):
    x = _pick(pl.program_id(0) < n_prompt_tiles, xp_ref, xs_ref)
    o_ref[...] = _rms(x, g_ref[...]).astype(o_ref.dtype)


def _rmsnorm(x_pair, g, *, tm, out_dtype):
    d = x_pair[0].shape[1]
    m = x_pair[0].shape[0] + x_pair[1].shape[0]
    npt = x_pair[0].shape[0] // tm
    est = 2 * (2 * _nbytes((tm, d), F32) + _nbytes((tm, d), out_dtype)) + 2 * _nbytes((tm, d), F32)
    return pl.pallas_call(
        functools.partial(_rmsnorm_kernel, n_prompt_tiles=npt),
        grid=(m // tm,),
        in_specs=_pair_specs(x_pair, tm, npt, lambda i: i) + [pl.BlockSpec((1, d), lambda i: (0, 0))],
        out_specs=pl.BlockSpec((tm, d), lambda i: (i, 0)),
        out_shape=jax.ShapeDtypeStruct((m, d), out_dtype),
        compiler_params=_params(("arbitrary",), est),
        name="rmsnorm",
    )(*x_pair, g.reshape(1, d))


def _mm_kernel(*refs, a_dual, res_dual, epilogue, cast_w, n_prompt_tiles):
    n_a = sum(2 if dual else 1 for dual in a_dual)
    a_refs, w_ref, refs = list(refs[:n_a]), refs[n_a], list(refs[n_a + 1:])
    is_prompt = pl.program_id(1) < n_prompt_tiles

    def take(src, dual):
        if dual:
            return _pick(is_prompt, src.pop(0), src.pop(0))
        return src.pop(0)[...]

    if cast_w:
        wbf_ref = refs.pop()

        @pl.when(pl.program_id(1) == 0)
        def _():
            wbf_ref[...] = w_ref[0].astype(BF16)

        w = wbf_ref[...]
    else:
        w = w_ref[...]
    cols = [take(a_refs, dual) for dual in a_dual]
    a = cols[0] if len(cols) == 1 else jnp.concatenate(cols, axis=1)
    acc = _dot(a, w)
    if epilogue == "relu2":
        acc = jnp.square(jnp.maximum(acc, 0.0))
    elif epilogue in ("residual", "residual_norm"):
        acc = acc + take(refs, res_dual)
    if epilogue == "residual_norm":
        g_ref, o_ref, n_ref = refs
        n_ref[...] = _rms(acc, g_ref[...]).astype(n_ref.dtype)
    else:
        (o_ref,) = refs
    o_ref[...] = acc.astype(o_ref.dtype)


def _mm(a, w, *, tm, tn, out_dtype, epilogue=None, residual=None, norm_g=None, layer=None,
        single_buffer_w=False, name):
    parts = a if isinstance(a, list) else [a]
    rows = lambda part: sum(x.shape[0] for x in part) if isinstance(part, tuple) else part.shape[0]
    width = lambda part: part[0].shape[1] if isinstance(part, tuple) else part.shape[1]
    a_dtype = parts[0][0].dtype if isinstance(parts[0], tuple) else parts[0].dtype
    m, k = rows(parts[0]), sum(width(part) for part in parts)
    pairs = [x for x in parts + [residual] if isinstance(x, tuple)]
    npt = pairs[0][0].shape[0] // tm if pairs else m // tm
    n = w.shape[-1]
    cast_w = layer is not None
    mode = dict(pipeline_mode=pl.Buffered(1)) if single_buffer_w else {}
    w_bufs = 1 if single_buffer_w else 2
    if cast_w:
        w_spec = pl.BlockSpec((1, k, tn), lambda j, i: (layer, 0, j), **mode)
        w_bytes = w_bufs * _nbytes((k, tn), F32) + _nbytes((k, tn), BF16)
        scratch = [pltpu.VMEM((k, tn), BF16)]
    else:
        w_spec = pl.BlockSpec((k, tn), lambda j, i: (0, j), **mode)
        w_bytes = w_bufs * _nbytes((k, tn), w.dtype)
        scratch = []
    tile = pl.BlockSpec((tm, tn), lambda j, i: (i, j))
    in_specs, args = [], []

    def add(part, spec):
        if isinstance(part, tuple):
            in_specs.extend(_pair_specs(part, tm, npt, lambda j, i: i))
            args.extend(part)
        else:
            in_specs.append(spec)
            args.append(part)

    for part in parts:
        add(part, pl.BlockSpec((tm, width(part)), lambda j, i: (i, 0)))
    in_specs.append(w_spec)
    args.append(w)
    out_specs, out_shape = tile, jax.ShapeDtypeStruct((m, n), out_dtype)
    est = w_bytes + 4 * _nbytes((tm, k), a_dtype) + 2 * _nbytes((tm, tn), out_dtype) + 2 * _nbytes((tm, tn), F32)
    if epilogue in ("residual", "residual_norm"):
        assert not isinstance(residual, tuple) or tn == n
        add(residual, tile)
        est += 4 * _nbytes((tm, tn), F32)
    if epilogue == "residual_norm":
        assert tn == n
        in_specs.append(pl.BlockSpec((1, n), lambda j, i: (0, 0)))
        args.append(norm_g.reshape(1, n))
        out_specs = [tile, tile]
        out_shape = [out_shape, jax.ShapeDtypeStruct((m, n), BF16)]
        est += 2 * _nbytes((tm, tn), BF16) + _nbytes((tm, tn), F32)
    return pl.pallas_call(
        functools.partial(_mm_kernel, a_dual=tuple(isinstance(part, tuple) for part in parts),
                          res_dual=isinstance(residual, tuple), epilogue=epilogue, cast_w=cast_w,
                          n_prompt_tiles=npt),
        grid=(n // tn, m // tm),
        in_specs=in_specs,
        out_specs=out_specs,
        out_shape=out_shape,
        scratch_shapes=scratch,
        compiler_params=_vmem_params(("parallel", "arbitrary"), est),
        name=name,
    )(*args)


def _ple_kernel(h_ref, p_ref, wp_ref, wg_ref, g_ref, *rest, n_prompt_tiles):
    *outs, wp_bf_ref, wg_bf_ref = rest
    i = pl.program_id(0)

    @pl.when(i == 0)
    def _():
        wp_bf_ref[...] = wp_ref[0].astype(BF16)
        wg_bf_ref[...] = wg_ref[0].astype(BF16)

    h = h_ref[...]
    gate = _sigmoid(_dot(h, wg_bf_ref[...]))
    h2 = h + _dot(p_ref[...], wp_bf_ref[...]) * gate
    if n_prompt_tiles is None:
        h_out_ref, n_out_ref = outs
        h_out_ref[...] = h2
        n_out_ref[...] = _rms(h2, g_ref[...]).astype(n_out_ref.dtype)
    else:
        yp_ref, ys_ref = outs
        y = _rms(h2, g_ref[...]).astype(yp_ref.dtype)

        @pl.when(i < n_prompt_tiles)
        def _():
            yp_ref[...] = y

        @pl.when(i >= n_prompt_tiles)
        def _():
            ys_ref[...] = y


def _ple(h, p, wp, wg, g, *, layer, tm, norm_dtype, split_rows=None):
    m, d = h.shape
    pd = p.shape[1]
    est = (_nbytes((pd, d), F32) + _nbytes((d, d), F32) + _nbytes((pd, d), BF16) + _nbytes((d, d), BF16)
           + 2 * (2 * _nbytes((tm, d), F32) + _nbytes((tm, pd), F32) + _nbytes((tm, d), norm_dtype))
           + 3 * _nbytes((tm, d), F32))
    once = dict(pipeline_mode=pl.Buffered(1))
    row = pl.BlockSpec((tm, d), lambda i: (i, 0))
    if split_rows is None:
        npt = None
        out_specs = [row, row]
        out_shape = [jax.ShapeDtypeStruct((m, d), F32), jax.ShapeDtypeStruct((m, d), norm_dtype)]
    else:
        npt = split_rows // tm
        out_specs = [pl.BlockSpec((tm, d), lambda i: (jnp.minimum(i, npt - 1), 0)),
                     pl.BlockSpec((tm, d), lambda i: (jnp.maximum(i - npt, 0), 0))]
        out_shape = [jax.ShapeDtypeStruct((split_rows, d), norm_dtype),
                     jax.ShapeDtypeStruct((m - split_rows, d), norm_dtype)]
    return pl.pallas_call(
        functools.partial(_ple_kernel, n_prompt_tiles=npt),
        grid=(m // tm,),
        in_specs=[
            row,
            pl.BlockSpec((tm, pd), lambda i: (i, 0)),
            pl.BlockSpec((1, pd, d), lambda i: (layer, 0, 0), **once),
            pl.BlockSpec((1, d, d), lambda i: (layer, 0, 0), **once),
            pl.BlockSpec((1, d), lambda i: (0, 0)),
        ],
        out_specs=out_specs,
        out_shape=out_shape,
        scratch_shapes=[pltpu.VMEM((pd, d), BF16), pltpu.VMEM((d, d), BF16)],
        compiler_params=_vmem_params(("arbitrary",), est),
        name="ple",
    )(h, p, wp, wg, g.reshape(1, d))


def _gdn_intra_kernel(qkv_ref, prev_ref, ba_ref, z_ref, buf_ref, cw_ref, alog_ref, dtb_ref,
                      u_ref, w_ref, qg_ref, kd_ref, qk_ref, el_ref, zs_ref, xs_ref, *, bb, tl, l_real):
    rows = bb * tl
    i = pl.program_id(1)
    xs_ref[:, 0:SUBLANES, :] = jnp.where(i == 0, buf_ref[...], prev_ref[...])
    x_in = qkv_ref[...]
    xs_ref[:, SUBLANES:SUBLANES + tl, :] = x_in
    cw = cw_ref[...]
    halo0 = SUBLANES - (CONV_W - 1)
    y = cw[CONV_W - 1:CONV_W, :] * x_in
    for j in range(CONV_W - 1):
        y = y + cw[j:j + 1, :] * xs_ref[:, halo0 + j:halo0 + j + tl, :]
    y = (y * _sigmoid(y)).reshape(rows, GDN_QKV)
    ba = ba_ref[...].reshape(rows, LANES)
    z = z_ref[...].reshape(rows, GDN_WIDTH)
    zs_ref[0] = (z * _sigmoid(z)).astype(zs_ref.dtype)

    shift = int(math.log2(tl))
    ri = lax.broadcasted_iota(jnp.int32, (rows, rows), 0)
    ci = lax.broadcasted_iota(jnp.int32, (rows, rows), 1)
    if bb == 1:
        lower = ri >= ci
        strict = ri > ci
    else:
        same = (ri >> shift) == (ci >> shift)
        lower = same & (ri >= ci)
        strict = same & (ri > ci)
    tril = lower.astype(F32)
    seg_last = (ci == ((ri >> shift) << shift) + (tl - 1)).astype(F32)
    lane = lax.broadcasted_iota(jnp.int32, (rows, LANES), 1)
    beta_all = _sigmoid(ba)
    g_all = -jnp.exp(alog_ref[...]) * _softplus(ba + dtb_ref[...])
    g_all = jnp.where(lane >= GDN_HEADS, g_all, 0.0)
    if l_real < tl:
        r1 = lax.broadcasted_iota(jnp.int32, (rows, 1), 0)
        rowmask = ((r1 & (tl - 1)) < l_real).astype(F32)
        y = y * rowmask
        beta_all = beta_all * rowmask
        g_all = g_all * rowmask
    gc_all = _dot_hi(tril, g_all)
    gl_all = _dot_hi(seg_last, gc_all)
    el_ref[0] = jnp.exp(gl_all)
    eg_all = jnp.exp(gc_all)
    kdf_all = jnp.exp(gl_all - gc_all)
    gct = jnp.concatenate([gc_all, jnp.zeros((LANES - rows, LANES), F32)], axis=0).T
    n_factors = max(1, math.ceil(math.log2(l_real)))

    heads = range(GDN_HEADS)
    lanes = [GDN_HEADS + h for h in heads]
    hsl = [slice(h * GDN_DV, (h + 1) * GDN_DV) for h in heads]
    qs = [y[:, h * GDN_DK:(h + 1) * GDN_DK] for h in heads]
    ks = [y[:, GDN_QK + h * GDN_DK:GDN_QK + (h + 1) * GDN_DK] for h in heads]
    vs = [y[:, 2 * GDN_QK + h * GDN_DV:2 * GDN_QK + (h + 1) * GDN_DV] for h in heads]
    qs = [q * lax.rsqrt(jnp.sum(q * q, axis=-1, keepdims=True) + EPS) * (GDN_DK ** -0.5) for q in qs]
    ks = [k * lax.rsqrt(jnp.sum(k * k, axis=-1, keepdims=True) + EPS) for k in ks]
    decays = [jnp.where(lower, jnp.exp(gc_all[:, lh:lh + 1] - gct[lh:lh + 1, 0:rows]), 0.0) for lh in lanes]
    kbs = [ks[h] * beta_all[:, h:h + 1] for h in heads]
    ps = [-jnp.where(strict, _dot(kbs[h], ks[h], NT) * decays[h], 0.0) for h in heads]
    xs = [jnp.concatenate([vs[h] * beta_all[:, h:h + 1], kbs[h] * eg_all[:, lanes[h]:lanes[h] + 1]], axis=-1)
          for h in heads]
    for h in heads:
        qk = jnp.where(lower, _dot(qs[h], ks[h], NT) * decays[h], 0.0)
        qk_ref[h, 0] = qk.astype(qk_ref.dtype)
        qg_ref[0, :, hsl[h]] = (qs[h] * eg_all[:, lanes[h]:lanes[h] + 1]).astype(qg_ref.dtype)
        kd_ref[0, :, hsl[h]] = (ks[h] * kdf_all[:, lanes[h]:lanes[h] + 1]).astype(kd_ref.dtype)
    for f in range(n_factors):
        xs = [xs[h] + _dot(ps[h], xs[h]) for h in heads]
        if f + 1 < n_factors:
            ps = [_dot(ps[h], ps[h]) for h in heads]
    for h in heads:
        u_ref[0, :, hsl[h]] = xs[h][:, :GDN_DV]
        w_ref[0, :, hsl[h]] = xs[h][:, GDN_DV:].astype(w_ref.dtype)


def _gdn_intra(ya3, buf8, conv_w, a_log, dt_bias, *, flat, n_groups, nt, bb, tl, l_real):
    rows = bb * tl
    assert rows == GDN_ROWS and (flat or nt == 1)

    def spec(width, col_block):
        if flat:
            return pl.BlockSpec((1, rows, width), lambda g, i: (0, g * nt + i, col_block))
        return pl.BlockSpec((bb, tl, width), lambda g, i: (g, 0, col_block))

    if flat:
        prev = pl.BlockSpec((1, SUBLANES, GDN_QKV),
                            lambda g, i: (0, jnp.maximum((g * nt + i) * (rows // SUBLANES) - 1, 0), 0))
    else:
        prev = spec(GDN_QKV, 0)
    lane_pad = jnp.zeros((1, LANES), F32)
    alog = lane_pad.at[0, GDN_HEADS:2 * GDN_HEADS].set(a_log)
    dtb = lane_pad.at[0, GDN_HEADS:2 * GDN_HEADS].set(dt_bias)
    total = nt * rows
    out = lambda width: pl.BlockSpec((1, rows, width), lambda g, i: (g, i, 0))
    shape = lambda width, dt: jax.ShapeDtypeStruct((n_groups, total, width), dt)
    est = 4 * rows * (2 * GDN_QKV + 6 * GDN_WIDTH) * 4
    return pl.pallas_call(
        functools.partial(_gdn_intra_kernel, bb=bb, tl=tl, l_real=l_real),
        grid=(n_groups, nt),
        in_specs=[
            spec(GDN_QKV, 0), prev, spec(LANES, A_COL_BA // LANES), spec(GDN_WIDTH, A_COL_Z // GDN_WIDTH),
            pl.BlockSpec((bb, SUBLANES, GDN_QKV), lambda g, i: (g, 0, 0)),
            pl.BlockSpec((CONV_W, GDN_QKV), lambda g, i: (0, 0)),
            pl.BlockSpec((1, LANES), lambda g, i: (0, 0)),
            pl.BlockSpec((1, LANES), lambda g, i: (0, 0)),
        ],
        out_specs=[
            out(GDN_WIDTH), out(GDN_WIDTH), out(GDN_WIDTH), out(GDN_WIDTH),
            pl.BlockSpec((GDN_HEADS, 1, rows, rows), lambda g, i: (0, g, i, 0)),
            out(LANES), out(GDN_WIDTH),
        ],
        out_shape=[
            shape(GDN_WIDTH, F32), shape(GDN_WIDTH, BF16), shape(GDN_WIDTH, BF16), shape(GDN_WIDTH, BF16),
            jax.ShapeDtypeStruct((GDN_HEADS, n_groups, total, rows), BF16),
            shape(LANES, F32), shape(GDN_WIDTH, BF16),
        ],
        scratch_shapes=[pltpu.VMEM((bb, tl + SUBLANES, GDN_QKV), F32)],
        compiler_params=_params(("parallel", "parallel"), est),
        name="gdn_intra",
    )(ya3, ya3, ya3, ya3, buf8, conv_w, alog, dtb)


def _gdn_state_kernel(u_ref, w_ref, qg_ref, kd_ref, qk_ref, el_ref, zs_ref, s0_ref, ng_ref,
                      o_ref, sout_ref, s_ref, *, nb, nseg, tl):
    i = pl.program_id(1)

    @pl.when(i == 0)
    def _():
        s_ref[...] = s0_ref[...]

    rows = nseg * tl
    seg_of_row = lax.broadcasted_iota(jnp.int32, (rows, 1), 0) >> int(math.log2(tl))
    pairs = [(blk, h) for blk in range(nb) for h in range(GDN_HEADS)]
    hsl = lambda h: slice(h * GDN_DV, (h + 1) * GDN_DV)
    segs = [slice(s * tl, (s + 1) * tl) for s in range(nseg)]
    cat = lambda parts: parts[0] if len(parts) == 1 else jnp.concatenate(parts, axis=0)
    inter = {}
    for blk, h in pairs:
        w = w_ref[blk, :, hsl(h)].astype(F32)
        qg = qg_ref[blk, :, hsl(h)].astype(F32)
        inter[blk, h] = [_dot(jnp.concatenate([w[rs], qg[rs]], axis=0), s_ref[blk * nseg + s, h])
                         for s, rs in enumerate(segs)]
    v_new = {}
    for blk, h in pairs:
        u = u_ref[blk, :, hsl(h)]
        v_new[blk, h] = cat([u[rs] - inter[blk, h][s][:tl] for s, rs in enumerate(segs)])
    for blk, h in pairs:
        o = cat([r[tl:] for r in inter[blk, h]]) + _dot(qk_ref[h, blk], v_new[blk, h])
        on = _rms(o, ng_ref[...]) * zs_ref[blk, :, hsl(h)].astype(F32)
        o_ref[blk, :, hsl(h)] = on.astype(o_ref.dtype)
    for blk, h in pairs:
        kd = kd_ref[blk, :, hsl(h)].astype(F32)
        lh = GDN_HEADS + h
        for s in range(nseg):
            e = el_ref[blk, s * tl:s * tl + 1, lh:lh + 1]
            kds = kd if nseg == 1 else jnp.where(seg_of_row == s, kd, 0.0)
            idx = blk * nseg + s
            s_ref[idx, h] = s_ref[idx, h] * e + _dot(kds, v_new[blk, h], TN)

    @pl.when(i == pl.num_programs(1) - 1)
    def _():
        sout_ref[...] = s_ref[...]


def _gdn_state(u, w, qg, kd, qk, el, zs, s0, norm_g, *, nb, nseg, tl):
    n_groups, total, _ = u.shape
    rows = nseg * tl
    assert rows == GDN_ROWS
    blk = lambda width: pl.BlockSpec((nb, rows, width), lambda g, i: (g, i, 0))
    st = pl.BlockSpec((nb * nseg, GDN_HEADS, GDN_DK, GDN_DV), lambda g, i: (g, 0, 0, 0))
    est = 2 * nb * rows * GDN_WIDTH * 16 + 5 * nb * nseg * GDN_HEADS * GDN_DK * GDN_DV * 4
    return pl.pallas_call(
        functools.partial(_gdn_state_kernel, nb=nb, nseg=nseg, tl=tl),
        grid=(n_groups // nb, total // rows),
        in_specs=[
            blk(GDN_WIDTH), blk(GDN_WIDTH), blk(GDN_WIDTH), blk(GDN_WIDTH),
            pl.BlockSpec((GDN_HEADS, nb, rows, rows), lambda g, i: (0, g, i, 0)),
            blk(LANES), blk(GDN_WIDTH), st,
            pl.BlockSpec((1, GDN_DV), lambda g, i: (0, 0)),
        ],
        out_specs=[blk(GDN_WIDTH), st],
        out_shape=[
            jax.ShapeDtypeStruct((n_groups, total, GDN_WIDTH), BF16),
            jax.ShapeDtypeStruct(s0.shape, F32),
        ],
        scratch_shapes=[pltpu.VMEM((nb * nseg, GDN_HEADS, GDN_DK, GDN_DV), F32)],
        compiler_params=_params(("parallel", "arbitrary"), est),
        name="gdn_state",
    )(u, w, qg, kd, qk, el, zs, s0, norm_g.reshape(1, GDN_DV))


def _lru_gates(xb, wa_ref, ba_ref, wi_ref, bi_ref, lam_ref):
    c = -LRU_C * _softplus(-lam_ref[...])
    sls = [slice(kb * LRU_BW, (kb + 1) * LRU_BW) for kb in range(LRU_BLOCKS)]
    xks = [xb[:, sl] for sl in sls]
    ra = [_dot(xks[kb], wa_ref[kb]) for kb in range(LRU_BLOCKS)]
    ri = [_dot(xks[kb], wi_ref[kb]) for kb in range(LRU_BLOCKS)]
    a_parts, b_parts = [], []
    for kb, sl in enumerate(sls):
        r = _sigmoid(ra[kb] + ba_ref[:, sl])
        ig = _sigmoid(ri[kb] + bi_ref[:, sl])
        log_a = c[:, sl] * r
        a_parts.append(jnp.exp(log_a))
        b_parts.append(jnp.sqrt(1.0 - jnp.exp(2.0 * log_a)) * (ig * xks[kb]))
    return jnp.concatenate(a_parts, axis=-1), jnp.concatenate(b_parts, axis=-1)


def _lru_prompt_kernel(lx_ref, ly_ref, buf_ref, h0_ref, cw_ref, cb_ref, wa_ref, ba_ref, wi_ref, bi_ref,
                       lam_ref, ob_ref, ht_ref, xs_ref, a_ref, b_ref, hs_ref, h_ref, *, tl):
    i = pl.program_id(1)

    @pl.when(i == 0)
    def _():
        h_ref[...] = h0_ref[0]
        xs_ref[0:SUBLANES, :] = buf_ref[0]

    @pl.when(i > 0)
    def _():
        xs_ref[0:SUBLANES, :] = xs_ref[tl:tl + SUBLANES, :]

    xs_ref[SUBLANES:SUBLANES + tl, :] = lx_ref[0]
    cw = cw_ref[...]
    halo0 = SUBLANES - (CONV_W - 1)
    xb = cw[CONV_W - 1:CONV_W, :] * xs_ref[SUBLANES:SUBLANES + tl, :] + cb_ref[...]
    for j in range(CONV_W - 1):
        xb = xb + cw[j:j + 1, :] * xs_ref[halo0 + j:halo0 + j + tl, :]
    a, b = _lru_gates(xb, wa_ref, ba_ref, wi_ref, bi_ref, lam_ref)
    a_ref[...] = a
    b_ref[...] = b

    def body(t, h):
        h = a_ref[pl.ds(t, 1), :] * h + b_ref[pl.ds(t, 1), :]
        hs_ref[pl.ds(t, 1), :] = h
        return h

    h = lax.fori_loop(0, tl, body, h_ref[...], unroll=8)
    h_ref[...] = h
    ob_ref[0] = (hs_ref[...] * jax.nn.gelu(ly_ref[0])).astype(ob_ref.dtype)

    @pl.when(i == pl.num_programs(1) - 1)
    def _():
        ht_ref[0] = h


def _lru_prompt(ya3, buf8, h0, cw, cb, wa, ba, wi, bi, lam, *, n_seq, seq_rows, tl):
    nt = seq_rows // tl
    w = LRU_WIDTH

    def spec(col_block):
        return pl.BlockSpec((1, tl, w), lambda b, i: (0, b * nt + i, col_block))

    vec = pl.BlockSpec((1, w), lambda b, i: (0, 0))
    blk = pl.BlockSpec((LRU_BLOCKS, LRU_BW, LRU_BW), lambda b, i: (0, 0, 0))
    est = 8 * tl * w * 4 + 4 * (tl + SUBLANES) * w * 4 + 4 * LRU_BLOCKS * LRU_BW * LRU_BW * 2
    return pl.pallas_call(
        functools.partial(_lru_prompt_kernel, tl=tl),
        grid=(n_seq, nt),
        in_specs=[
            spec(A_COL_LX // w), spec(A_COL_LY // w),
            pl.BlockSpec((1, SUBLANES, w), lambda b, i: (b, 0, 0)),
            pl.BlockSpec((1, 1, w), lambda b, i: (b, 0, 0)),
            pl.BlockSpec((CONV_W, w), lambda b, i: (0, 0)), vec, blk, vec, blk, vec, vec,
        ],
        out_specs=[
            pl.BlockSpec((1, tl, w), lambda b, i: (b, i, 0)),
            pl.BlockSpec((1, 1, w), lambda b, i: (b, 0, 0)),
        ],
        out_shape=[
            jax.ShapeDtypeStruct((n_seq, seq_rows, w), BF16),
            jax.ShapeDtypeStruct((n_seq, 1, w), F32),
        ],
        scratch_shapes=[
            pltpu.VMEM((tl + SUBLANES, w), F32), pltpu.VMEM((tl, w), F32), pltpu.VMEM((tl, w), F32),
            pltpu.VMEM((tl, w), F32), pltpu.VMEM((1, w), F32),
        ],
        compiler_params=_params(("parallel", "arbitrary"), est),
        name="lru_prompt",
    )(ya3, ya3, buf8, h0, cw, cb.reshape(1, w), wa, ba.reshape(1, w), wi, bi.reshape(1, w),
      lam.reshape(1, w))


def _lru_sample_kernel(lx_ref, ly_ref, buf_ref, h0_ref, cw_ref, cb_ref, wa_ref, ba_ref, wi_ref, bi_ref,
                       lam_ref, ob_ref, ht_ref, *, steps):
    cw = cw_ref[...]
    n_buf = CONV_W - 1

    def tap(t):
        return buf_ref[t] if t < n_buf else lx_ref[t - n_buf]

    h = h0_ref[...]
    for t in range(steps):
        xb = cb_ref[...] + cw[0:1, :] * tap(t)
        for j in range(1, CONV_W):
            xb = xb + cw[j:j + 1, :] * tap(t + j)
        a, b = _lru_gates(xb, wa_ref, ba_ref, wi_ref, bi_ref, lam_ref)
        h = a * h + b
        ob_ref[t] = (h * jax.nn.gelu(ly_ref[t])).astype(ob_ref.dtype)
    ht_ref[...] = h


def _lru_sample(lx_tm, ly_tm, buf_tm, h0, cw, cb, wa, ba, wi, bi, lam):
    steps, n, w = lx_tm.shape
    est = 16 * n * w * 4
    return pl.pallas_call(
        functools.partial(_lru_sample_kernel, steps=steps),
        out_shape=[jax.ShapeDtypeStruct((steps, n, w), BF16), jax.ShapeDtypeStruct((n, w), F32)],
        compiler_params=_params(None, est),
        name="lru_sample",
    )(lx_tm, ly_tm, buf_tm, h0, cw, cb.reshape(1, w), wa, ba.reshape(1, w), wi, bi.reshape(1, w),
      lam.reshape(1, w))


def _rope_lanes(x, cos, sin, lane):
    half = QK_ROPE // 2
    rot = jnp.where(lane % QK_ROPE < half, pltpu.roll(x, LANES - half, 1), pltpu.roll(x, half, 1))
    return x * cos + rot * sin


def _mla_prep_kernel(cq_ref, ckv_ref, kpe_ref, cos_ref, sin_ref, qg_ref, kg_ref,
                     cqn_ref, ckv_out_ref, ckv_bf_ref, kpe_out_ref, kpe_bf_ref):
    cqn_ref[...] = _rms(cq_ref[...], qg_ref[...]).astype(cqn_ref.dtype)
    ckv = _rms(ckv_ref[...], kg_ref[...])
    ckv_out_ref[...] = ckv
    ckv_bf_ref[...] = ckv.astype(ckv_bf_ref.dtype)
    x = kpe_ref[...]
    lane = lax.broadcasted_iota(jnp.int32, x.shape, 1)
    kpe = _rope_lanes(x, cos_ref[...], sin_ref[...], lane)
    kpe_out_ref[...] = kpe
    kpe_bf_ref[...] = kpe.astype(kpe_bf_ref.dtype)


def _mla_prep(yc, cos, sin, q_g, kv_g, *, tm):
    m = yc.shape[0]
    row = lambda width, col_block: pl.BlockSpec((tm, width), lambda i: (i, col_block))
    vec = pl.BlockSpec((1, Q_LORA), lambda i: (0, 0))
    est = 2 * tm * (3 * Q_LORA + 4 * LANES) * 4 * 3
    return pl.pallas_call(
        _mla_prep_kernel,
        grid=(m // tm,),
        in_specs=[row(Q_LORA, 0), row(KV_LORA, 1), row(LANES, C_COL_KPE // LANES), row(LANES, 0),
                  row(LANES, 0), vec, vec],
        out_specs=[row(Q_LORA, 0), row(KV_LORA, 0), row(KV_LORA, 0), row(LANES, 0), row(LANES, 0)],
        out_shape=[
            jax.ShapeDtypeStruct((m, Q_LORA), BF16), jax.ShapeDtypeStruct((m, KV_LORA), F32),
            jax.ShapeDtypeStruct((m, KV_LORA), BF16), jax.ShapeDtypeStruct((m, LANES), F32),
            jax.ShapeDtypeStruct((m, LANES), BF16),
        ],
        compiler_params=_params(("parallel",), est),
        name="mla_prep",
    )(yc, yc, yc, cos, sin, q_g.reshape(1, Q_LORA), kv_g.reshape(1, KV_LORA))


def _mm_qrope_kernel(a_ref, w_ref, cos_ref, sin_ref, o_ref, *, heads):
    acc = _dot(a_ref[...], w_ref[...])
    cos = cos_ref[...]
    sin = sin_ref[...]
    lane = lax.broadcasted_iota(jnp.int32, cos.shape, 1)
    for h in range(heads):
        c0 = h * Q_HEAD_COLS
        o_ref[:, c0:c0 + QK_NOPE] = (acc[:, c0:c0 + QK_NOPE] * MLA_SCALE).astype(o_ref.dtype)
        pe = _rope_lanes(acc[:, c0 + QK_NOPE:c0 + Q_HEAD_COLS], cos, sin, lane)
        o_ref[:, c0 + QK_NOPE:c0 + Q_HEAD_COLS] = (pe * MLA_SCALE).astype(o_ref.dtype)


def _mm_qrope(a, w, cos, sin, *, tm, tn):
    m, k = a.shape
    n = w.shape[1]
    est = 2 * (tm * k * 2 + k * tn * 2 + tm * tn * 2 + 2 * tm * LANES * 4) + 2 * tm * tn * 4
    return pl.pallas_call(
        functools.partial(_mm_qrope_kernel, heads=tn // Q_HEAD_COLS),
        grid=(n // tn, m // tm),
        in_specs=[
            pl.BlockSpec((tm, k), lambda j, i: (i, 0)), pl.BlockSpec((k, tn), lambda j, i: (0, j)),
            pl.BlockSpec((tm, LANES), lambda j, i: (i, 0)), pl.BlockSpec((tm, LANES), lambda j, i: (i, 0)),
        ],
        out_specs=pl.BlockSpec((tm, tn), lambda j, i: (i, j)),
        out_shape=jax.ShapeDtypeStruct((m, n), BF16),
        compiler_params=_params(("parallel", "parallel"), est),
        name="mm_qrope",
    )(a, w, cos, sin)


def _flash_kernel(q_ref, kv_ref, kpe_ref, o_ref, m_ref, l_ref, acc_ref, *, tq, heads):
    qi = pl.program_id(2)
    m_ref[...] = jnp.full(m_ref.shape, -jnp.inf, F32)
    l_ref[...] = jnp.zeros(l_ref.shape, F32)
    acc_ref[...] = jnp.zeros(acc_ref.shape, F32)
    reps = tq // LANES

    def tile(j, diagonal):
        k0 = pl.multiple_of(j * tq, tq)
        kp = kpe_ref[pl.ds(k0, tq), :]
        hd = range(heads)
        cols = [hh * Q_HEAD_COLS for hh in hd]
        ks = [jnp.concatenate([kv_ref[pl.ds(k0, tq), c0:c0 + QK_NOPE], kp], axis=-1) for c0 in cols]
        ss = [lax.dot_general(q_ref[:, c0:c0 + Q_HEAD_COLS], ks[hh], NT, preferred_element_type=F32)
              for hh, c0 in enumerate(cols)]
        if diagonal:
            r = lax.broadcasted_iota(jnp.int32, (tq, tq), 0)
            c = lax.broadcasted_iota(jnp.int32, (tq, tq), 1)
            ss = [jnp.where(c <= r, s, -jnp.inf) for s in ss]
        m_old = [m_ref[hh] for hh in hd]
        m_new = [jnp.maximum(m_old[hh], jnp.max(ss[hh], axis=-1, keepdims=True)) for hh in hd]
        ps = [jnp.exp(ss[hh] - jnp.concatenate([m_new[hh]] * reps, axis=-1)) for hh in hd]
        corr = [jnp.exp(m_old[hh] - m_new[hh]) for hh in hd]
        pv = [lax.dot_general(ps[hh].astype(BF16), kv_ref[pl.ds(k0, tq), c0 + QK_NOPE:c0 + Q_HEAD_COLS], NN,
                              preferred_element_type=F32) for hh, c0 in enumerate(cols)]
        for hh in hd:
            l_ref[hh] = l_ref[hh] * corr[hh] + jnp.sum(ps[hh], axis=-1, keepdims=True)
            acc_ref[hh] = acc_ref[hh] * corr[hh] + pv[hh]
            m_ref[hh] = m_new[hh]

    def body(j, carry):
        tile(j, False)
        return carry

    lax.fori_loop(0, qi, body, 0)
    tile(qi, True)
    for hh in range(heads):
        o_ref[:, hh * V_HEAD:(hh + 1) * V_HEAD] = (acc_ref[hh] / l_ref[hh]).astype(o_ref.dtype)


def _flash(q, kv, kpe_bf, *, n_seq, seq_len, tq, heads):
    nq = seq_len // tq
    qw = heads * Q_HEAD_COLS
    est = 2 * (tq * qw * 2 + seq_len * qw * 2 + seq_len * LANES * 2 + tq * heads * V_HEAD * 2)
    est += 6 * heads * tq * tq * 4
    return pl.pallas_call(
        functools.partial(_flash_kernel, tq=tq, heads=heads),
        grid=(n_seq, MLA_HEADS // heads, nq),
        in_specs=[
            pl.BlockSpec((tq, qw), lambda b, h, i: (b * nq + i, h)),
            pl.BlockSpec((seq_len, qw), lambda b, h, i: (b, h)),
            pl.BlockSpec((seq_len, LANES), lambda b, h, i: (b, 0)),
        ],
        out_specs=pl.BlockSpec((tq, heads * V_HEAD), lambda b, h, i: (b * nq + i, h)),
        out_shape=jax.ShapeDtypeStruct((n_seq * seq_len, MLA_HEADS * V_HEAD), BF16),
        scratch_shapes=[pltpu.VMEM((heads, tq, LANES), F32), pltpu.VMEM((heads, tq, LANES), F32),
                        pltpu.VMEM((heads, tq, V_HEAD), F32)],
        compiler_params=_params(("parallel", "parallel", "arbitrary"), est),
        name="flash_prompt",
    )(q, kv, kpe_bf)


def _head_mm_kernel(a_ref, w_ref, o_ref):
    o_ref[...] = _dot(a_ref[...], w_ref[0]).astype(o_ref.dtype)


def _head_mm(a, w, *, a_width, a_col_stride, out_dtype, name):
    m = a.shape[0]
    heads, k, n = w.shape
    est = 2 * (m * a_width * 2 + k * n * 2 + m * n * 4) + m * n * 4
    return pl.pallas_call(
        _head_mm_kernel,
        grid=(heads,),
        in_specs=[
            pl.BlockSpec((m, a_width), lambda h: (0, h * a_col_stride)),
            pl.BlockSpec((1, k, n), lambda h: (h, 0, 0)),
        ],
        out_specs=pl.BlockSpec((m, n), lambda h: (0, h)),
        out_shape=jax.ShapeDtypeStruct((m, heads * n), out_dtype),
        compiler_params=_params(("parallel",), est),
        name=name,
    )(a, w)


def _paged_kernel(pt_ref, qlat_ref, qpe_ref, ckn_ref, kpn_ref, ck_hbm, kp_hbm, o_ref,
                  ck_buf, kp_buf, sems, m_ref, l_ref, acc_ref, *, pages, seqs, t_new, layer, n_pages, n_steps):
    step = pl.program_id(1)
    t = pl.program_id(0) * n_steps + step
    last_t = pl.num_programs(0) * n_steps - 1
    slot = lax.rem(t, 2)
    qs = list(range(seqs))

    def page_copies(tt, sl, q):
        base = ((tt // n_steps) * seqs + q) * n_pages + lax.rem(tt, n_steps) * pages
        out = []
        for k in range(pages):
            page = pt_ref[base + k]
            out.append(pltpu.make_async_copy(ck_hbm.at[layer, page], ck_buf.at[sl, q * pages + k], sems.at[0, sl]))
            out.append(pltpu.make_async_copy(kp_hbm.at[layer, page], kp_buf.at[sl, q * pages + k], sems.at[1, sl]))
        return out

    def start_all(copies):
        for n, c in enumerate(copies):
            c.start(priority=(n // 2) % 2)

    @pl.when(t == 0)
    def _():
        for q in qs:
            start_all(page_copies(t, slot, q))

    @pl.when(step == 0)
    def _():
        m_ref[...] = jnp.full(m_ref.shape, -jnp.inf, F32)
        l_ref[...] = jnp.zeros(l_ref.shape, F32)
        acc_ref[...] = jnp.zeros(acc_ref.shape, F32)

    for q in qs:
        for c in page_copies(t, slot, q):
            c.wait()

    def update(scores, keys):
        m_old = [m_ref[q] for q in qs]
        m_new = [jnp.maximum(m_old[q], jnp.max(scores[q], axis=-1, keepdims=True)) for q in qs]
        p = [jnp.exp(scores[q] - m_new[q]) for q in qs]
        corr = [jnp.exp(m_old[q] - m_new[q]) for q in qs]
        pv = [lax.dot_general(p[q].astype(BF16), keys[q], NN, preferred_element_type=F32) for q in qs]
        for q in qs:
            l_ref[q] = l_ref[q] * corr[q] + jnp.sum(p[q], axis=-1, keepdims=True)
            acc_ref[q] = acc_ref[q] * corr[q] + pv[q]
            m_ref[q] = m_new[q]

    t_next = jnp.minimum(t + 1, last_t)
    ql = [qlat_ref[q] for q in qs]
    qp = [qpe_ref[q][:, :QK_ROPE] for q in qs]
    ck, kpt = [], []
    for q in qs:
        ck.append(jnp.concatenate([ck_buf[slot, q * pages + k].astype(BF16) for k in range(pages)], axis=0))
        kpt.append(jnp.concatenate([kp_buf[slot, q * pages + k].astype(BF16) for k in range(pages)], axis=1))
        start_all(page_copies(t_next, 1 - slot, q))
    s = [lax.dot_general(ql[q], ck[q], NT, preferred_element_type=F32)
         + lax.dot_general(qp[q], kpt[q], NN, preferred_element_type=F32) for q in qs]
    update(s, ck)

    @pl.when(step == n_steps - 1)
    def _():
        rows = ql[0].shape[0]
        pad = PAGE_SIZE - ckn_ref.shape[1]
        ckn = [jnp.concatenate([ckn_ref[q], jnp.zeros((pad, KV_LORA), F32)], axis=0).astype(BF16) for q in qs]
        t_row = lax.broadcasted_iota(jnp.int32, (rows, PAGE_SIZE), 0) // MLA_HEADS
        j_col = lax.broadcasted_iota(jnp.int32, (rows, PAGE_SIZE), 1)
        valid = j_col <= jnp.minimum(t_row, t_new - 1)
        sn = [jnp.where(valid, lax.dot_general(ql[q], ckn[q], NT, preferred_element_type=F32)
                        + lax.dot_general(qp[q], kpn_ref[q].astype(BF16), NN, preferred_element_type=F32),
                        -jnp.inf) for q in qs]
        update(sn, ckn)
        for q in qs:
            o_ref[q] = (acc_ref[q] / l_ref[q]).astype(o_ref.dtype)

    @pl.when(t == last_t)
    def _():
        for q in qs:
            for c in page_copies(t_next, 1 - slot, q):
                c.wait()


def _paged(page_table, qlat, qpe, ckv_new8, kpe_new_t, cache_ckv, cache_kpe_t, *, layer, t_new):
    n_seq, rows, _ = qlat.shape
    n_pages = page_table.shape[1]
    pages, seqs = PAGES_PER_STEP, SEQS_PER_STEP
    steps = n_pages // pages
    pt = page_table.reshape(-1).astype(jnp.int32)
    t_pad = ckv_new8.shape[1]
    per_seq = lambda d1, d2: pl.BlockSpec((seqs, d1, d2), lambda b, s, pt_ref: (b, 0, 0))
    n_buf = seqs * pages
    est = (2 * n_buf * PAGE_SIZE * (KV_LORA + QK_ROPE) * 4 + 2 * n_buf * PAGE_SIZE * (KV_LORA + QK_ROPE) * 2
           + 8 * seqs * rows * (KV_LORA + pages * PAGE_SIZE) * 4)
    grid_spec = pltpu.PrefetchScalarGridSpec(
        num_scalar_prefetch=1,
        grid=(n_seq // seqs, steps),
        in_specs=[per_seq(rows, KV_LORA), per_seq(rows, LANES), per_seq(t_pad, KV_LORA),
                  per_seq(QK_ROPE, PAGE_SIZE), pl.BlockSpec(memory_space=pl.ANY),
                  pl.BlockSpec(memory_space=pl.ANY)],
        out_specs=per_seq(rows, KV_LORA),
        scratch_shapes=[pltpu.VMEM((2, n_buf, PAGE_SIZE, KV_LORA), F32),
                        pltpu.VMEM((2, n_buf, QK_ROPE, PAGE_SIZE), F32),
                        pltpu.SemaphoreType.DMA((2, 2)),
                        pltpu.VMEM((seqs, rows, 1), F32), pltpu.VMEM((seqs, rows, 1), F32),
                        pltpu.VMEM((seqs, rows, KV_LORA), F32)],
    )
    return pl.pallas_call(
        functools.partial(_paged_kernel, pages=pages, seqs=seqs, t_new=t_new, layer=layer, n_pages=n_pages,
                          n_steps=steps),
        grid_spec=grid_spec,
        out_shape=jax.ShapeDtypeStruct((n_seq, rows, KV_LORA), BF16),
        compiler_params=_vmem_params(("arbitrary", "arbitrary"), est),
        name="paged_attn",
    )(pt, qlat, qpe, ckv_new8, kpe_new_t, cache_ckv, cache_kpe_t)


def _pad_rows(x, rows_before, rows_total):
    pad = [(0, 0)] * x.ndim
    pad[1] = (rows_before, rows_total - rows_before - x.shape[1])
    return jnp.pad(x, pad)


def _conv_state(buf, x):
    keep = CONV_W - 1
    if x.shape[1] >= keep:
        return x[:, x.shape[1] - keep:]
    return jnp.concatenate([buf.astype(x.dtype), x], axis=1)[:, -keep:]


def _rope_tables(positions):
    half = QK_ROPE // 2
    inv = ROPE_BASE ** (-jnp.arange(half, dtype=F32) / half)
    ang = positions.astype(F32)[:, None] * inv[None, :]
    cos, sin = jnp.cos(ang), jnp.sin(ang)
    zeros = jnp.zeros((positions.shape[0], LANES - QK_ROPE), F32)
    return (jnp.concatenate([cos, cos, zeros], axis=-1), jnp.concatenate([-sin, sin, zeros], axis=-1))


def kernel(x_prompt, x_sample, p_prompt, p_sample, state_gdn, state_gdn_conv, state_lru, state_lru_conv, cache_ckv, cache_kpe, page_table, w_in_a, conv_qkv_w, gdn_a_log, gdn_dt_bias, gdn_norm_g, conv_lru_w, conv_lru_b, lru_wa, lru_ba, lru_wi, lru_bi, lru_lambda, w_out_a, w_in_c, q_norm_g, kv_norm_g, w_uq, w_ukv, w_o_c, norm_mix_g, norm_ffn_g, w_up, w_down, w_ple, w_ple_gate, norm_final_g):
    bp, seq, d = x_prompt.shape
    bs, ts, _ = x_sample.shape
    mp, ms = bp * seq, bs * ts
    m = mp + ms
    depth = norm_mix_g.shape[0]
    past = page_table.shape[1] * PAGE_SIZE
    tm = 512

    h = (x_prompt.reshape(mp, d), x_sample.reshape(ms, d))
    pos = jnp.concatenate([jnp.tile(jnp.arange(seq), bp), jnp.tile(past + jnp.arange(ts), bs)])
    cos, sin = _rope_tables(pos)
    hn = _rmsnorm(h, norm_mix_g[0], tm=tm, out_dtype=BF16)

    gdn_p, gdn_s, gconv_p, gconv_s, lru_p, lru_s, lconv_p, lconv_s = [], [], [], [], [], [], [], []
    ckv_p, ckv_s, kpe_p, kpe_s = [], [], [], []

    for i in range(depth):
        j = i // 2
        if i % 2 == 0:
            wa = w_in_a[j]
            cut_b = GDN_QKV + GDN_WIDTH
            cut_lx = cut_b + 2 * GDN_HEADS
            w_a = jnp.concatenate([
                wa[:, :cut_b], wa[:, cut_lx:], wa[:, cut_b:cut_lx],
                jnp.zeros((d, LANES - 2 * GDN_HEADS), F32)], axis=1).astype(BF16)
            ya = _mm(hn, w_a, single_buffer_w=True, tm=tm, tn=A_COLS, out_dtype=F32, name="in_proj_a")
            ya_p = ya.reshape(1, m, A_COLS)
            ya_s = ya[mp:].reshape(bs, ts, A_COLS)

            zero_buf = jnp.zeros((bp, SUBLANES, GDN_QKV), F32)
            parts_p = _gdn_intra(ya_p, zero_buf, conv_qkv_w[j], gdn_a_log[j], gdn_dt_bias[j], flat=True,
                                 n_groups=bp, nt=seq // GDN_CHUNK, bb=1, tl=GDN_CHUNK, l_real=GDN_CHUNK)
            o_gp, s_gp = _gdn_state(*parts_p, jnp.zeros((bp,) + state_gdn.shape[2:], F32), gdn_norm_g[j],
                                    nb=bp, nseg=1, tl=GDN_CHUNK)
            o_lp, h_lp = _lru_prompt(ya_p, jnp.zeros((bp, SUBLANES, LRU_WIDTH), F32),
                                     jnp.zeros((bp, 1, LRU_WIDTH), F32), conv_lru_w[j], conv_lru_b[j],
                                     lru_wa[j].astype(BF16), lru_ba[j].reshape(-1), lru_wi[j].astype(BF16),
                                     lru_bi[j].reshape(-1), lru_lambda[j], n_seq=bp, seq_rows=seq, tl=256)

            t_pad = SUBLANES
            ya_s8 = _pad_rows(ya_s, 0, t_pad)
            buf_g8 = _pad_rows(state_gdn_conv[j], SUBLANES - (CONV_W - 1), SUBLANES)
            seg = GDN_ROWS // t_pad
            parts_s = _gdn_intra(ya_s8, buf_g8, conv_qkv_w[j], gdn_a_log[j], gdn_dt_bias[j], flat=False,
                                 n_groups=bs // seg, nt=1, bb=seg, tl=t_pad, l_real=ts)
            o_gs, s_gs = _gdn_state(*parts_s, state_gdn[j], gdn_norm_g[j], nb=1, nseg=seg, tl=t_pad)
            o_gs = o_gs.reshape(bs, t_pad, GDN_WIDTH)
            lx_tm = ya_s[:, :, A_COL_LX:A_COL_LY].transpose(1, 0, 2)
            ly_tm = ya_s[:, :, A_COL_LY:A_COL_BA].transpose(1, 0, 2)
            o_ls_tm, h_ls = _lru_sample(lx_tm, ly_tm, state_lru_conv[j].transpose(1, 0, 2), state_lru[j],
                                        conv_lru_w[j], conv_lru_b[j], lru_wa[j].astype(BF16),
                                        lru_ba[j].reshape(-1), lru_wi[j].astype(BF16), lru_bi[j].reshape(-1),
                                        lru_lambda[j])
            mix = [(o_gp.reshape(mp, GDN_WIDTH), o_gs[:, :ts].reshape(ms, GDN_WIDTH)),
                   (o_lp.reshape(mp, LRU_WIDTH), o_ls_tm.transpose(1, 0, 2).reshape(ms, LRU_WIDTH))]
            h, hn = _mm(mix, w_out_a, layer=j, single_buffer_w=True, tm=tm // 2, tn=d, out_dtype=F32,
                        epilogue="residual_norm", residual=h, norm_g=norm_ffn_g[i], name="out_proj_a")

            keep = CONV_W - 1
            ya_pp = jnp.stack([ya[(b + 1) * seq - keep:(b + 1) * seq] for b in range(bp)])
            gdn_p.append(s_gp)
            gdn_s.append(s_gs)
            gconv_p.append(ya_pp[:, :, :GDN_QKV])
            gconv_s.append(_conv_state(state_gdn_conv[j], ya_s[:, :, :GDN_QKV]))
            lru_p.append(h_lp.reshape(bp, LRU_WIDTH))
            lru_s.append(h_ls)
            lconv_p.append(ya_pp[:, :, A_COL_LX:A_COL_LY])
            lconv_s.append(_conv_state(state_lru_conv[j], ya_s[:, :, A_COL_LX:A_COL_LY]))
        else:
            w_c = jnp.concatenate([w_in_c[j], jnp.zeros((d, LANES - QK_ROPE), F32)], axis=1).astype(BF16)
            yc = _mm(hn, w_c, tm=tm, tn=C_COLS, out_dtype=F32, name="in_proj_c")
            cqn, ckv, ckv_bf, kpe, kpe_bf = _mla_prep(yc, cos, sin, q_norm_g[j], kv_norm_g[j], tm=tm)
            w_q = w_uq[j].reshape(Q_LORA, MLA_HEADS, QK_NOPE + QK_ROPE)
            w_q = jnp.pad(w_q, ((0, 0), (0, 0), (0, Q_HEAD_COLS - QK_NOPE - QK_ROPE)))
            q = _mm_qrope(cqn, w_q.reshape(Q_LORA, MLA_HEADS * Q_HEAD_COLS).astype(BF16), cos, sin,
                          tm=tm, tn=MLA_HEADS * Q_HEAD_COLS)
            w_r = w_ukv[j].reshape(KV_LORA, MLA_HEADS, QK_NOPE + V_HEAD)

            kv = _mm(ckv_bf[:mp], w_ukv, layer=j, tm=tm, tn=w_ukv.shape[-1], out_dtype=BF16, name="kv_up")
            o_p = _flash(q, kv, kpe_bf, n_seq=bp, seq_len=seq, tq=512, heads=2)

            q_s = q[mp:]
            w_uk_t = w_r[:, :, :QK_NOPE].transpose(1, 2, 0).astype(BF16)
            w_uv = w_r[:, :, QK_NOPE:].transpose(1, 0, 2).astype(BF16)
            qlat = _head_mm(q_s, w_uk_t, a_width=QK_NOPE, a_col_stride=Q_HEAD_COLS // QK_NOPE,
                            out_dtype=BF16, name="q_latent")
            qlat = qlat.reshape(bs, ts * MLA_HEADS, KV_LORA)
            qpe = q_s.reshape(ms, MLA_HEADS, Q_HEAD_COLS)[:, :, QK_NOPE:].reshape(bs, ts * MLA_HEADS, LANES)
            ckv_new8 = _pad_rows(ckv[mp:].reshape(bs, ts, KV_LORA), 0, SUBLANES)
            kpe_new_t = kpe[mp:, :QK_ROPE].reshape(bs, ts, QK_ROPE).transpose(0, 2, 1)
            kpe_new_t = jnp.pad(kpe_new_t, ((0, 0), (0, 0), (0, PAGE_SIZE - ts)))
            lat = _paged(page_table, qlat, qpe, ckv_new8, kpe_new_t, cache_ckv,
                         cache_kpe.transpose(0, 1, 3, 2), layer=j, t_new=ts)
            o_s = _head_mm(lat.reshape(ms, MLA_HEADS * KV_LORA), w_uv, a_width=KV_LORA, a_col_stride=1,
                           out_dtype=BF16, name="v_up")
            h, hn = _mm([(o_p, o_s)], w_o_c, layer=j, single_buffer_w=True, tm=tm, tn=d,
                        out_dtype=F32, epilogue="residual_norm", residual=h, norm_g=norm_ffn_g[i],
                        name="out_proj_c")
            ckv_p.append(ckv[:mp].reshape(bp, seq, KV_LORA))
            ckv_s.append(ckv[mp:].reshape(bs, ts, KV_LORA))
            kpe_p.append(kpe[:mp, :QK_ROPE].reshape(bp, seq, QK_ROPE))
            kpe_s.append(kpe[mp:, :QK_ROPE].reshape(bs, ts, QK_ROPE))

        ff = _mm(hn, w_up, layer=i, tm=tm, tn=2048, out_dtype=BF16, epilogue="relu2", name="ffn_up")
        h = _mm(ff, w_down, layer=i, single_buffer_w=True, tm=tm, tn=512, out_dtype=F32, epilogue="residual",
                residual=h, name="ffn_down")
        p = jnp.concatenate([p_prompt[i].reshape(mp, -1), p_sample[i].reshape(ms, -1)], axis=0)
        if i + 1 < depth:
            h, hn = _ple(h, p, w_ple, w_ple_gate, norm_mix_g[i + 1], layer=i, tm=256, norm_dtype=BF16)
        else:
            y_p, y_s = _ple(h, p, w_ple, w_ple_gate, norm_final_g, layer=i, tm=256, norm_dtype=F32, split_rows=mp)

    return (y_p.reshape(bp, seq, d), y_s.reshape(bs, ts, d),
            jnp.stack(gdn_p), jnp.stack(gdn_s), jnp.stack(gconv_p), jnp.stack(gconv_s),
            jnp.stack(lru_p), jnp.stack(lru_s), jnp.stack(lconv_p), jnp.stack(lconv_s),
            jnp.stack(ckv_p), jnp.stack(ckv_s), jnp.stack(kpe_p), jnp.stack(kpe_s))
```

```python
import functools
import math

import jax
import jax.numpy as jnp
from jax import lax
from jax.experimental import pallas as pl
from jax.experimental.pallas import tpu as pltpu

F32 = jnp.float32
BF16 = jnp.bfloat16

D_MODEL = 2048
CONV_W = 4
GDN_HEADS = 8
GDN_DK = 128
GDN_DV = 128
GDN_QK = GDN_HEADS * GDN_DK
GDN_WIDTH = GDN_HEADS * GDN_DV
GDN_QKV = 2 * GDN_QK + GDN_WIDTH
GDN_CHUNK = 64
LRU_WIDTH = D_MODEL // 2
LRU_BLOCKS = 8
LRU_BW = LRU_WIDTH // LRU_BLOCKS
LRU_C = 8.0
MLA_HEADS = 16
Q_LORA = 512
KV_LORA = 512
QK_NOPE = 128
QK_ROPE = 64
V_HEAD = 128
MLA_SCALE = (QK_NOPE + QK_ROPE) ** -0.5
ROPE_BASE = 10000.0
PAGE_SIZE = 128
EPS = 1e-6

LANES = 128
SUBLANES = 8
VMEM_LIMIT_BYTES = 56 * 1024 * 1024

A_COL_Z = GDN_QKV
A_COL_LX = A_COL_Z + GDN_WIDTH
A_COL_LY = A_COL_LX + LRU_WIDTH
A_COL_BA = A_COL_LY + LRU_WIDTH
A_COLS = A_COL_BA + LANES
C_COL_KPE = Q_LORA + KV_LORA
C_COLS = C_COL_KPE + LANES
Q_HEAD_COLS = 2 * LANES
PAGES_PER_STEP = 4
PAGE_SLOTS = 3
SEQS_PER_STEP = 4
GDN_ROWS = 64

NN = (((1,), (0,)), ((), ()))
NT = (((1,), (1,)), ((), ()))
TN = (((0,), (0,)), ((), ()))


def _params(semantics, est_bytes):
    limit = int(min(max(2 * est_bytes, 32 * 1024 * 1024), VMEM_LIMIT_BYTES))
    return pltpu.CompilerParams(dimension_semantics=semantics, vmem_limit_bytes=limit)


def _vmem_params(semantics, need_bytes):
    limit = int(min(max(need_bytes + 4 * 1024 * 1024, 32 * 1024 * 1024), VMEM_LIMIT_BYTES))
    return pltpu.CompilerParams(dimension_semantics=semantics, vmem_limit_bytes=limit)


def _nbytes(shape, dtype):
    return math.prod(shape) * jnp.dtype(dtype).itemsize


def _dot(a, b, dims=NN):
    return lax.dot_general(a.astype(BF16), b.astype(BF16), dims, preferred_element_type=F32)


def _dot_hi(a, b, dims=NN):
    return lax.dot_general(a, b, dims, precision=lax.Precision.HIGHEST, preferred_element_type=F32)


def _sigmoid(x):
    return jax.nn.sigmoid(x)


def _softplus(x):
    return jnp.maximum(x, 0.0) + jnp.log1p(jnp.exp(-jnp.abs(x)))


def _rms(x, g):
    return x * lax.rsqrt(jnp.mean(x * x, axis=-1, keepdims=True) + EPS) * g


def _pick(is_prompt, prompt_ref, sample_ref):
    return jnp.where(is_prompt, prompt_ref[...], sample_ref[...])


def _pair_specs(pair, tm, n_prompt_tiles, index):
    prompt, sample = pair
    width = prompt.shape[1]
    return [
        pl.BlockSpec((tm, width), lambda *g: (jnp.minimum(index(*g), n_prompt_tiles - 1), 0)),
        pl.BlockSpec((tm, width), lambda *g: (jnp.maximum(index(*g) - n_prompt_tiles, 0), 0)),
    ]


def _rmsnorm_kernel(xp_ref, xs_ref, g_ref, o_ref, *, n_prompt_tiles):
    x = _pick(pl.program_id(0) < n_prompt_tiles, xp_ref, xs_ref)
    o_ref[...] = _rms(x, g_ref[...]).astype(o_ref.dtype)


def _rmsnorm(x_pair, g, *, tm, out_dtype):
    d = x_pair[0].shape[1]
    m = x_pair[0].shape[0] + x_pair[1].shape[0]
    npt = x_pair[0].shape[0] // tm
    est = 2 * (2 * _nbytes((tm, d), F32) + _nbytes((tm, d), out_dtype)) + 2 * _nbytes((tm, d), F32)
    return pl.pallas_call(
        functools.partial(_rmsnorm_kernel, n_prompt_tiles=npt),
        grid=(m // tm,),
        in_specs=_pair_specs(x_pair, tm, npt, lambda i: i) + [pl.BlockSpec((1, d), lambda i: (0, 0))],
        out_specs=pl.BlockSpec((tm, d), lambda i: (i, 0)),
        out_shape=jax.ShapeDtypeStruct((m, d), out_dtype),
        compiler_params=_params(("arbitrary",), est),
        name="rmsnorm",
    )(*x_pair, g.reshape(1, d))


def _mm_kernel(*refs, a_dual, res_dual, epilogue, cast_w, n_prompt_tiles):
    n_a = sum(2 if dual else 1 for dual in a_dual)
    a_refs, w_ref, refs = list(refs[:n_a]), refs[n_a], list(refs[n_a + 1:])
    is_prompt = pl.program_id(1) < n_prompt_tiles

    def take(src, dual):
        if dual:
            return _pick(is_prompt, src.pop(0), src.pop(0))
        return src.pop(0)[...]

    if cast_w:
        wbf_ref = refs.pop()

        @pl.when(pl.program_id(1) == 0)
        def _():
            wbf_ref[...] = w_ref[0].astype(BF16)

        w = wbf_ref[...]
    else:
        w = w_ref[...]
    cols = [take(a_refs, dual) for dual in a_dual]
    a = cols[0] if len(cols) == 1 else jnp.concatenate(cols, axis=1)
    acc = _dot(a, w)
    if epilogue == "relu2":
        acc = jnp.square(jnp.maximum(acc, 0.0))
    elif epilogue in ("residual", "residual_norm"):
        acc = acc + take(refs, res_dual)
    if epilogue == "residual_norm":
        g_ref, o_ref, n_ref = refs
        n_ref[...] = _rms(acc, g_ref[...]).astype(n_ref.dtype)
    else:
        (o_ref,) = refs
    o_ref[...] = acc.astype(o_ref.dtype)


def _mm(a, w, *, tm, tn, out_dtype, epilogue=None, residual=None, norm_g=None, layer=None,
        single_buffer_w=False, name):
    parts = a if isinstance(a, list) else [a]
    rows = lambda part: sum(x.shape[0] for x in part) if isinstance(part, tuple) else part.shape[0]
    width = lambda part: part[0].shape[1] if isinstance(part, tuple) else part.shape[1]
    a_dtype = parts[0][0].dtype if isinstance(parts[0], tuple) else parts[0].dtype
    m, k = rows(parts[0]), sum(width(part) for part in parts)
    pairs = [x for x in parts + [residual] if isinstance(x, tuple)]
    npt = pairs[0][0].shape[0] // tm if pairs else m // tm
    n = w.shape[-1]
    cast_w = layer is not None
    mode = dict(pipeline_mode=pl.Buffered(1)) if single_buffer_w else {}
    w_bufs = 1 if single_buffer_w else 2
    if cast_w:
        w_spec = pl.BlockSpec((1, k, tn), lambda j, i: (layer, 0, j), **mode)
        w_bytes = w_bufs * _nbytes((k, tn), F32) + _nbytes((k, tn), BF16)
        scratch = [pltpu.VMEM((k, tn), BF16)]
    else:
        w_spec = pl.BlockSpec((k, tn), lambda j, i: (0, j), **mode)
        w_bytes = w_bufs * _nbytes((k, tn), w.dtype)
        scratch = []
    tile = pl.BlockSpec((tm, tn), lambda j, i: (i, j))
    in_specs, args = [], []

    def add(part, spec):
        if isinstance(part, tuple):
            in_specs.extend(_pair_specs(part, tm, npt, lambda j, i: i))
            args.extend(part)
        else:
            in_specs.append(spec)
            args.append(part)

    for part in parts:
        add(part, pl.BlockSpec((tm, width(part)), lambda j, i: (i, 0)))
    in_specs.append(w_spec)
    args.append(w)
    out_specs, out_shape = tile, jax.ShapeDtypeStruct((m, n), out_dtype)
    est = w_bytes + 4 * _nbytes((tm, k), a_dtype) + 2 * _nbytes((tm, tn), out_dtype) + 2 * _nbytes((tm, tn), F32)
    if epilogue in ("residual", "residual_norm"):
        assert not isinstance(residual, tuple) or tn == n
        add(residual, tile)
        est += 4 * _nbytes((tm, tn), F32)
    if epilogue == "residual_norm":
        assert tn == n
        in_specs.append(pl.BlockSpec((1, n), lambda j, i: (0, 0)))
        args.append(norm_g.reshape(1, n))
        out_specs = [tile, tile]
        out_shape = [out_shape, jax.ShapeDtypeStruct((m, n), BF16)]
        est += 2 * _nbytes((tm, tn), BF16) + _nbytes((tm, tn), F32)
    return pl.pallas_call(
        functools.partial(_mm_kernel, a_dual=tuple(isinstance(part, tuple) for part in parts),
                          res_dual=isinstance(residual, tuple), epilogue=epilogue, cast_w=cast_w,
                          n_prompt_tiles=npt),
        grid=(n // tn, m // tm),
        in_specs=in_specs,
        out_specs=out_specs,
        out_shape=out_shape,
        scratch_shapes=scratch,
        compiler_params=_vmem_params(("parallel", "arbitrary"), est),
        name=name,
    )(*args)


def _ple_kernel(h_ref, p_ref, wp_ref, wg_ref, g_ref, *rest, n_prompt_tiles):
    *outs, wp_bf_ref, wg_bf_ref = rest
    i = pl.program_id(0)

    @pl.when(i == 0)
    def _():
        wp_bf_ref[...] = wp_ref[0].astype(BF16)
        wg_bf_ref[...] = wg_ref[0].astype(BF16)

    h = h_ref[...]
    gate = _sigmoid(_dot(h, wg_bf_ref[...]))
    h2 = h + _dot(p_ref[...], wp_bf_ref[...]) * gate
    if n_prompt_tiles is None:
        h_out_ref, n_out_ref = outs
        h_out_ref[...] = h2
        n_out_ref[...] = _rms(h2, g_ref[...]).astype(n_out_ref.dtype)
    else:
        yp_ref, ys_ref = outs
        y = _rms(h2, g_ref[...]).astype(yp_ref.dtype)

        @pl.when(i < n_prompt_tiles)
        def _():
            yp_ref[...] = y

        @pl.when(i >= n_prompt_tiles)
        def _():
            ys_ref[...] = y


def _ple(h, p, wp, wg, g, *, layer, tm, norm_dtype, split_rows=None):
    m, d = h.shape
    pd = p.shape[1]
    est = (_nbytes((pd, d), F32) + _nbytes((d, d), F32) + _nbytes((pd, d), BF16) + _nbytes((d, d), BF16)
           + 2 * (2 * _nbytes((tm, d), F32) + _nbytes((tm, pd), F32) + _nbytes((tm, d), norm_dtype))
           + 3 * _nbytes((tm, d), F32))
    once = dict(pipeline_mode=pl.Buffered(1))
    row = pl.BlockSpec((tm, d), lambda i: (i, 0))
    if split_rows is None:
        npt = None
        out_specs = [row, row]
        out_shape = [jax.ShapeDtypeStruct((m, d), F32), jax.ShapeDtypeStruct((m, d), norm_dtype)]
    else:
        npt = split_rows // tm
        out_specs = [pl.BlockSpec((tm, d), lambda i: (jnp.minimum(i, npt - 1), 0)),
                     pl.BlockSpec((tm, d), lambda i: (jnp.maximum(i - npt, 0), 0))]
        out_shape = [jax.ShapeDtypeStruct((split_rows, d), norm_dtype),
                     jax.ShapeDtypeStruct((m - split_rows, d), norm_dtype)]
    return pl.pallas_call(
        functools.partial(_ple_kernel, n_prompt_tiles=npt),
        grid=(m // tm,),
        in_specs=[
            row,
            pl.BlockSpec((tm, pd), lambda i: (i, 0)),
            pl.BlockSpec((1, pd, d), lambda i: (layer, 0, 0), **once),
            pl.BlockSpec((1, d, d), lambda i: (layer, 0, 0), **once),
            pl.BlockSpec((1, d), lambda i: (0, 0)),
        ],
        out_specs=out_specs,
        out_shape=out_shape,
        scratch_shapes=[pltpu.VMEM((pd, d), BF16), pltpu.VMEM((d, d), BF16)],
        compiler_params=_vmem_params(("arbitrary",), est),
        name="ple",
    )(h, p, wp, wg, g.reshape(1, d))


def _gdn_intra_kernel(qkv_ref, prev_ref, ba_ref, z_ref, buf_ref, cw_ref, alog_ref, dtb_ref,
                      u_ref, w_ref, qg_ref, kd_ref, qk_ref, el_ref, zs_ref, xs_ref, *, bb, tl, l_real):
    rows = bb * tl
    i = pl.program_id(1)
    xs_ref[:, 0:SUBLANES, :] = jnp.where(i == 0, buf_ref[...], prev_ref[...])
    x_in = qkv_ref[...]
    xs_ref[:, SUBLANES:SUBLANES + tl, :] = x_in
    cw = cw_ref[...]
    halo0 = SUBLANES - (CONV_W - 1)
    y = cw[CONV_W - 1:CONV_W, :] * x_in
    for j in range(CONV_W - 1):
        y = y + cw[j:j + 1, :] * xs_ref[:, halo0 + j:halo0 + j + tl, :]
    y = (y * _sigmoid(y)).reshape(rows, GDN_QKV)
    ba = ba_ref[...].reshape(rows, LANES)
    z = z_ref[...].reshape(rows, GDN_WIDTH)
    zs_ref[0] = (z * _sigmoid(z)).astype(zs_ref.dtype)

    shift = int(math.log2(tl))
    ri = lax.broadcasted_iota(jnp.int32, (rows, rows), 0)
    ci = lax.broadcasted_iota(jnp.int32, (rows, rows), 1)
    if bb == 1:
        lower = ri >= ci
        strict = ri > ci
    else:
        same = (ri >> shift) == (ci >> shift)
        lower = same & (ri >= ci)
        strict = same & (ri > ci)
    tril = lower.astype(F32)
    seg_last = (ci == ((ri >> shift) << shift) + (tl - 1)).astype(F32)
    lane = lax.broadcasted_iota(jnp.int32, (rows, LANES), 1)
    beta_all = _sigmoid(ba)
    g_all = -jnp.exp(alog_ref[...]) * _softplus(ba + dtb_ref[...])
    g_all = jnp.where(lane >= GDN_HEADS, g_all, 0.0)
    if l_real < tl:
        r1 = lax.broadcasted_iota(jnp.int32, (rows, 1), 0)
        rowmask = ((r1 & (tl - 1)) < l_real).astype(F32)
        y = y * rowmask
        beta_all = beta_all * rowmask
        g_all = g_all * rowmask
    gc_all = _dot_hi(tril, g_all)
    gl_all = _dot_hi(seg_last, gc_all)
    el_ref[0] = jnp.exp(gl_all)
    eg_all = jnp.exp(gc_all)
    kdf_all = jnp.exp(gl_all - gc_all)
    gct = jnp.concatenate([gc_all, jnp.zeros((LANES - rows, LANES), F32)], axis=0).T
    n_factors = max(1, math.ceil(math.log2(l_real)))

    heads = range(GDN_HEADS)
    lanes = [GDN_HEADS + h for h in heads]
    hsl = [slice(h * GDN_DV, (h + 1) * GDN_DV) for h in heads]
    qs = [y[:, h * GDN_DK:(h + 1) * GDN_DK] for h in heads]
    ks = [y[:, GDN_QK + h * GDN_DK:GDN_QK + (h + 1) * GDN_DK] for h in heads]
    vs = [y[:, 2 * GDN_QK + h * GDN_DV:2 * GDN_QK + (h + 1) * GDN_DV] for h in heads]
    qs = [q * lax.rsqrt(jnp.sum(q * q, axis=-1, keepdims=True) + EPS) * (GDN_DK ** -0.5) for q in qs]
    ks = [k * lax.rsqrt(jnp.sum(k * k, axis=-1, keepdims=True) + EPS) for k in ks]
    decays = [jnp.where(lower, jnp.exp(gc_all[:, lh:lh + 1] - gct[lh:lh + 1, 0:rows]), 0.0) for lh in lanes]
    kbs = [ks[h] * beta_all[:, h:h + 1] for h in heads]
    ps = [-jnp.where(strict, _dot(kbs[h], ks[h], NT) * decays[h], 0.0) for h in heads]
    xs = [jnp.concatenate([vs[h] * beta_all[:, h:h + 1], kbs[h] * eg_all[:, lanes[h]:lanes[h] + 1]], axis=-1)
          for h in heads]
    for h in heads:
        qk = jnp.where(lower, _dot(qs[h], ks[h], NT) * decays[h], 0.0)
        qk_ref[h, 0] = qk.astype(qk_ref.dtype)
        qg_ref[0, :, hsl[h]] = (qs[h] * eg_all[:, lanes[h]:lanes[h] + 1]).astype(qg_ref.dtype)
        kd_ref[0, :, hsl[h]] = (ks[h] * kdf_all[:, lanes[h]:lanes[h] + 1]).astype(kd_ref.dtype)
    for f in range(n_factors):
        xs = [xs[h] + _dot(ps[h], xs[h]) for h in heads]
        if f + 1 < n_factors:
            ps = [_dot(ps[h], ps[h]) for h in heads]
    for h in heads:
        u_ref[0, :, hsl[h]] = xs[h][:, :GDN_DV]
        w_ref[0, :, hsl[h]] = xs[h][:, GDN_DV:].astype(w_ref.dtype)


def _gdn_intra(ya3, buf8, conv_w, a_log, dt_bias, *, flat, n_groups, nt, bb, tl, l_real):
    rows = bb * tl
    assert rows == GDN_ROWS and (flat or nt == 1)

    def spec(width, col_block):
        if flat:
            return pl.BlockSpec((1, rows, width), lambda g, i: (0, g * nt + i, col_block))
        return pl.BlockSpec((bb, tl, width), lambda g, i: (g, 0, col_block))

    if flat:
        prev = pl.BlockSpec((1, SUBLANES, GDN_QKV),
                            lambda g, i: (0, jnp.maximum((g * nt + i) * (rows // SUBLANES) - 1, 0), 0))
    else:
        prev = spec(GDN_QKV, 0)
    lane_pad = jnp.zeros((1, LANES), F32)
    alog = lane_pad.at[0, GDN_HEADS:2 * GDN_HEADS].set(a_log)
    dtb = lane_pad.at[0, GDN_HEADS:2 * GDN_HEADS].set(dt_bias)
    total = nt * rows
    out = lambda width: pl.BlockSpec((1, rows, width), lambda g, i: (g, i, 0))
    shape = lambda width, dt: jax.ShapeDtypeStruct((n_groups, total, width), dt)
    est = 4 * rows * (2 * GDN_QKV + 6 * GDN_WIDTH) * 4
    return pl.pallas_call(
        functools.partial(_gdn_intra_kernel, bb=bb, tl=tl, l_real=l_real),
        grid=(n_groups, nt),
        in_specs=[
            spec(GDN_QKV, 0), prev, spec(LANES, A_COL_BA // LANES), spec(GDN_WIDTH, A_COL_Z // GDN_WIDTH),
            pl.BlockSpec((bb, SUBLANES, GDN_QKV), lambda g, i: (g, 0, 0)),
            pl.BlockSpec((CONV_W, GDN_QKV), lambda g, i: (0, 0)),
            pl.BlockSpec((1, LANES), lambda g, i: (0, 0)),
            pl.BlockSpec((1, LANES), lambda g, i: (0, 0)),
        ],
        out_specs=[
            out(GDN_WIDTH), out(GDN_WIDTH), out(GDN_WIDTH), out(GDN_WIDTH),
            pl.BlockSpec((GDN_HEADS, 1, rows, rows), lambda g, i: (0, g, i, 0)),
            out(LANES), out(GDN_WIDTH),
        ],
        out_shape=[
            shape(GDN_WIDTH, F32), shape(GDN_WIDTH, BF16), shape(GDN_WIDTH, BF16), shape(GDN_WIDTH, BF16),
            jax.ShapeDtypeStruct((GDN_HEADS, n_groups, total, rows), BF16),
            shape(LANES, F32), shape(GDN_WIDTH, BF16),
        ],
        scratch_shapes=[pltpu.VMEM((bb, tl + SUBLANES, GDN_QKV), F32)],
        compiler_params=_params(("parallel", "parallel"), est),
        name="gdn_intra",
    )(ya3, ya3, ya3, ya3, buf8, conv_w, alog, dtb)


def _gdn_state_kernel(u_ref, w_ref, qg_ref, kd_ref, qk_ref, el_ref, zs_ref, s0_ref, ng_ref,
                      o_ref, sout_ref, s_ref, *, nb, nseg, tl):
    i = pl.program_id(1)

    @pl.when(i == 0)
    def _():
        s_ref[...] = s0_ref[...]

    rows = nseg * tl
    seg_of_row = lax.broadcasted_iota(jnp.int32, (rows, 1), 0) >> int(math.log2(tl))
    pairs = [(blk, h) for blk in range(nb) for h in range(GDN_HEADS)]
    hsl = lambda h: slice(h * GDN_DV, (h + 1) * GDN_DV)
    segs = [slice(s * tl, (s + 1) * tl) for s in range(nseg)]
    cat = lambda parts: parts[0] if len(parts) == 1 else jnp.concatenate(parts, axis=0)
    inter = {}
    for blk, h in pairs:
        w = w_ref[blk, :, hsl(h)].astype(F32)
        qg = qg_ref[blk, :, hsl(h)].astype(F32)
        inter[blk, h] = [_dot(jnp.concatenate([w[rs], qg[rs]], axis=0), s_ref[blk * nseg + s, h])
                         for s, rs in enumerate(segs)]
    v_new = {}
    for blk, h in pairs:
        u = u_ref[blk, :, hsl(h)]
        v_new[blk, h] = cat([u[rs] - inter[blk, h][s][:tl] for s, rs in enumerate(segs)])
    for blk, h in pairs:
        o = cat([r[tl:] for r in inter[blk, h]]) + _dot(qk_ref[h, blk], v_new[blk, h])
        on = _rms(o, ng_ref[...]) * zs_ref[blk, :, hsl(h)].astype(F32)
        o_ref[blk, :, hsl(h)] = on.astype(o_ref.dtype)
    for blk, h in pairs:
        kd = kd_ref[blk, :, hsl(h)].astype(F32)
        lh = GDN_HEADS + h
        for s in range(nseg):
            e = el_ref[blk, s * tl:s * tl + 1, lh:lh + 1]
            kds = kd if nseg == 1 else jnp.where(seg_of_row == s, kd, 0.0)
            idx = blk * nseg + s
            s_ref[idx, h] = s_ref[idx, h] * e + _dot(kds, v_new[blk, h], TN)

    @pl.when(i == pl.num_programs(1) - 1)
    def _():
        sout_ref[...] = s_ref[...]


def _gdn_state(u, w, qg, kd, qk, el, zs, s0, norm_g, *, nb, nseg, tl):
    n_groups, total, _ = u.shape
    rows = nseg * tl
    assert rows == GDN_ROWS
    blk = lambda width: pl.BlockSpec((nb, rows, width), lambda g, i: (g, i, 0))
    st = pl.BlockSpec((nb * nseg, GDN_HEADS, GDN_DK, GDN_DV), lambda g, i: (g, 0, 0, 0))
    est = 2 * nb * rows * GDN_WIDTH * 16 + 5 * nb * nseg * GDN_HEADS * GDN_DK * GDN_DV * 4
    return pl.pallas_call(
        functools.partial(_gdn_state_kernel, nb=nb, nseg=nseg, tl=tl),
        grid=(n_groups // nb, total // rows),
        in_specs=[
            blk(GDN_WIDTH), blk(GDN_WIDTH), blk(GDN_WIDTH), blk(GDN_WIDTH),
            pl.BlockSpec((GDN_HEADS, nb, rows, rows), lambda g, i: (0, g, i, 0)),
            blk(LANES), blk(GDN_WIDTH), st,
            pl.BlockSpec((1, GDN_DV), lambda g, i: (0, 0)),
        ],
        out_specs=[blk(GDN_WIDTH), st],
        out_shape=[
            jax.ShapeDtypeStruct((n_groups, total, GDN_WIDTH), BF16),
            jax.ShapeDtypeStruct(s0.shape, F32),
        ],
        scratch_shapes=[pltpu.VMEM((nb * nseg, GDN_HEADS, GDN_DK, GDN_DV), F32)],
        compiler_params=_params(("parallel", "arbitrary"), est),
        name="gdn_state",
    )(u, w, qg, kd, qk, el, zs, s0, norm_g.reshape(1, GDN_DV))


def _lru_gates(xb, wa_ref, ba_ref, wi_ref, bi_ref, lam_ref):
    c = -LRU_C * _softplus(-lam_ref[...])
    sls = [slice(kb * LRU_BW, (kb + 1) * LRU_BW) for kb in range(LRU_BLOCKS)]
    xks = [xb[:, sl] for sl in sls]
    ra = [_dot(xks[kb], wa_ref[kb]) for kb in range(LRU_BLOCKS)]
    ri = [_dot(xks[kb], wi_ref[kb]) for kb in range(LRU_BLOCKS)]
    a_parts, b_parts = [], []
    for kb, sl in enumerate(sls):
        r = _sigmoid(ra[kb] + ba_ref[:, sl])
        ig = _sigmoid(ri[kb] + bi_ref[:, sl])
        log_a = c[:, sl] * r
        a_parts.append(jnp.exp(log_a))
        b_parts.append(jnp.sqrt(1.0 - jnp.exp(2.0 * log_a)) * (ig * xks[kb]))
    return jnp.concatenate(a_parts, axis=-1), jnp.concatenate(b_parts, axis=-1)


def _lru_prompt_kernel(lx_ref, ly_ref, buf_ref, h0_ref, cw_ref, cb_ref, wa_ref, ba_ref, wi_ref, bi_ref,
                       lam_ref, ob_ref, ht_ref, xs_ref, a_ref, b_ref, hs_ref, h_ref, *, tl):
    i = pl.program_id(1)

    @pl.when(i == 0)
    def _():
        h_ref[...] = h0_ref[0]
        xs_ref[0:SUBLANES, :] = buf_ref[0]

    @pl.when(i > 0)
    def _():
        xs_ref[0:SUBLANES, :] = xs_ref[tl:tl + SUBLANES, :]

    xs_ref[SUBLANES:SUBLANES + tl, :] = lx_ref[0]
    cw = cw_ref[...]
    halo0 = SUBLANES - (CONV_W - 1)
    xb = cw[CONV_W - 1:CONV_W, :] * xs_ref[SUBLANES:SUBLANES + tl, :] + cb_ref[...]
    for j in range(CONV_W - 1):
        xb = xb + cw[j:j + 1, :] * xs_ref[halo0 + j:halo0 + j + tl, :]
    a, b = _lru_gates(xb, wa_ref, ba_ref, wi_ref, bi_ref, lam_ref)
    a_ref[...] = a
    b_ref[...] = b

    def body(t, h):
        h = a_ref[pl.ds(t, 1), :] * h + b_ref[pl.ds(t, 1), :]
        hs_ref[pl.ds(t, 1), :] = h
        return h

    h = lax.fori_loop(0, tl, body, h_ref[...], unroll=8)
    h_ref[...] = h
    ob_ref[0] = (hs_ref[...] * jax.nn.gelu(ly_ref[0])).astype(ob_ref.dtype)

    @pl.when(i == pl.num_programs(1) - 1)
    def _():
        ht_ref[0] = h


def _lru_prompt(ya3, buf8, h0, cw, cb, wa, ba, wi, bi, lam, *, n_seq, seq_rows, tl):
    nt = seq_rows // tl
    w = LRU_WIDTH

    def spec(col_block):
        return pl.BlockSpec((1, tl, w), lambda b, i: (0, b * nt + i, col_block))

    vec = pl.BlockSpec((1, w), lambda b, i: (0, 0))
    blk = pl.BlockSpec((LRU_BLOCKS, LRU_BW, LRU_BW), lambda b, i: (0, 0, 0))
    est = 8 * tl * w * 4 + 4 * (tl + SUBLANES) * w * 4 + 4 * LRU_BLOCKS * LRU_BW * LRU_BW * 2
    return pl.pallas_call(
        functools.partial(_lru_prompt_kernel, tl=tl),
        grid=(n_seq, nt),
        in_specs=[
            spec(A_COL_LX // w), spec(A_COL_LY // w),
            pl.BlockSpec((1, SUBLANES, w), lambda b, i: (b, 0, 0)),
            pl.BlockSpec((1, 1, w), lambda b, i: (b, 0, 0)),
            pl.BlockSpec((CONV_W, w), lambda b, i: (0, 0)), vec, blk, vec, blk, vec, vec,
        ],
        out_specs=[
            pl.BlockSpec((1, tl, w), lambda b, i: (b, i, 0)),
            pl.BlockSpec((1, 1, w), lambda b, i: (b, 0, 0)),
        ],
        out_shape=[
            jax.ShapeDtypeStruct((n_seq, seq_rows, w), BF16),
            jax.ShapeDtypeStruct((n_seq, 1, w), F32),
        ],
        scratch_shapes=[
            pltpu.VMEM((tl + SUBLANES, w), F32), pltpu.VMEM((tl, w), F32), pltpu.VMEM((tl, w), F32),
            pltpu.VMEM((tl, w), F32), pltpu.VMEM((1, w), F32),
        ],
        compiler_params=_params(("parallel", "arbitrary"), est),
        name="lru_prompt",
    )(ya3, ya3, buf8, h0, cw, cb.reshape(1, w), wa, ba.reshape(1, w), wi, bi.reshape(1, w),
      lam.reshape(1, w))


def _lru_sample_kernel(lx_ref, ly_ref, buf_ref, h0_ref, cw_ref, cb_ref, wa_ref, ba_ref, wi_ref, bi_ref,
                       lam_ref, ob_ref, ht_ref, *, steps):
    cw = cw_ref[...]
    n_buf = CONV_W - 1

    def tap(t):
        return buf_ref[t] if t < n_buf else lx_ref[t - n_buf]

    h = h0_ref[...]
    for t in range(steps):
        xb = cb_ref[...] + cw[0:1, :] * tap(t)
        for j in range(1, CONV_W):
            xb = xb + cw[j:j + 1, :] * tap(t + j)
        a, b = _lru_gates(xb, wa_ref, ba_ref, wi_ref, bi_ref, lam_ref)
        h = a * h + b
        ob_ref[t] = (h * jax.nn.gelu(ly_ref[t])).astype(ob_ref.dtype)
    ht_ref[...] = h


def _lru_sample(lx_tm, ly_tm, buf_tm, h0, cw, cb, wa, ba, wi, bi, lam):
    steps, n, w = lx_tm.shape
    est = 16 * n * w * 4
    return pl.pallas_call(
        functools.partial(_lru_sample_kernel, steps=steps),
        out_shape=[jax.ShapeDtypeStruct((steps, n, w), BF16), jax.ShapeDtypeStruct((n, w), F32)],
        compiler_params=_params(None, est),
        name="lru_sample",
    )(lx_tm, ly_tm, buf_tm, h0, cw, cb.reshape(1, w), wa, ba.reshape(1, w), wi, bi.reshape(1, w),
      lam.reshape(1, w))


def _rope_lanes(x, cos, sin, lane):
    half = QK_ROPE // 2
    rot = jnp.where(lane % QK_ROPE < half, pltpu.roll(x, LANES - half, 1), pltpu.roll(x, half, 1))
    return x * cos + rot * sin


def _mla_prep_kernel(cq_ref, ckv_ref, kpe_ref, cos_ref, sin_ref, qg_ref, kg_ref,
                     cqn_ref, ckv_out_ref, ckv_bf_ref, kpe_out_ref, kpe_bf_ref):
    cqn_ref[...] = _rms(cq_ref[...], qg_ref[...]).astype(cqn_ref.dtype)
    ckv = _rms(ckv_ref[...], kg_ref[...])
    ckv_out_ref[...] = ckv
    ckv_bf_ref[...] = ckv.astype(ckv_bf_ref.dtype)
    x = kpe_ref[...]
    lane = lax.broadcasted_iota(jnp.int32, x.shape, 1)
    kpe = _rope_lanes(x, cos_ref[...], sin_ref[...], lane)
    kpe_out_ref[...] = kpe
    kpe_bf_ref[...] = kpe.astype(kpe_bf_ref.dtype)


def _mla_prep(yc, cos, sin, q_g, kv_g, *, tm):
    m = yc.shape[0]
    row = lambda width, col_block: pl.BlockSpec((tm, width), lambda i: (i, col_block))
    vec = pl.BlockSpec((1, Q_LORA), lambda i: (0, 0))
    est = 2 * tm * (3 * Q_LORA + 4 * LANES) * 4 * 3
    return pl.pallas_call(
        _mla_prep_kernel,
        grid=(m // tm,),
        in_specs=[row(Q_LORA, 0), row(KV_LORA, 1), row(LANES, C_COL_KPE // LANES), row(LANES, 0),
                  row(LANES, 0), vec, vec],
        out_specs=[row(Q_LORA, 0), row(KV_LORA, 0), row(KV_LORA, 0), row(LANES, 0), row(LANES, 0)],
        out_shape=[
            jax.ShapeDtypeStruct((m, Q_LORA), BF16), jax.ShapeDtypeStruct((m, KV_LORA), F32),
            jax.ShapeDtypeStruct((m, KV_LORA), BF16), jax.ShapeDtypeStruct((m, LANES), F32),
            jax.ShapeDtypeStruct((m, LANES), BF16),
        ],
        compiler_params=_params(("parallel",), est),
        name="mla_prep",
    )(yc, yc, yc, cos, sin, q_g.reshape(1, Q_LORA), kv_g.reshape(1, KV_LORA))


def _mm_qrope_kernel(a_ref, w_ref, cos_ref, sin_ref, o_ref, *, heads):
    acc = _dot(a_ref[...], w_ref[...])
    cos = cos_ref[...]
    sin = sin_ref[...]
    lane = lax.broadcasted_iota(jnp.int32, cos.shape, 1)
    for h in range(heads):
        c0 = h * Q_HEAD_COLS
        o_ref[:, c0:c0 + QK_NOPE] = (acc[:, c0:c0 + QK_NOPE] * MLA_SCALE).astype(o_ref.dtype)
        pe = _rope_lanes(acc[:, c0 + QK_NOPE:c0 + Q_HEAD_COLS], cos, sin, lane)
        o_ref[:, c0 + QK_NOPE:c0 + Q_HEAD_COLS] = (pe * MLA_SCALE).astype(o_ref.dtype)


def _mm_qrope(a, w, cos, sin, *, tm, tn):
    m, k = a.shape
    n = w.shape[1]
    est = 2 * (tm * k * 2 + k * tn * 2 + tm * tn * 2 + 2 * tm * LANES * 4) + 2 * tm * tn * 4
    return pl.pallas_call(
        functools.partial(_mm_qrope_kernel, heads=tn // Q_HEAD_COLS),
        grid=(n // tn, m // tm),
        in_specs=[
            pl.BlockSpec((tm, k), lambda j, i: (i, 0)), pl.BlockSpec((k, tn), lambda j, i: (0, j)),
            pl.BlockSpec((tm, LANES), lambda j, i: (i, 0)), pl.BlockSpec((tm, LANES), lambda j, i: (i, 0)),
        ],
        out_specs=pl.BlockSpec((tm, tn), lambda j, i: (i, j)),
        out_shape=jax.ShapeDtypeStruct((m, n), BF16),
        compiler_params=_params(("parallel", "parallel"), est),
        name="mm_qrope",
    )(a, w, cos, sin)


def _flash_kernel(q_ref, kv_ref, kpe_ref, o_ref, m_ref, l_ref, acc_ref, *, tq, heads):
    qi = pl.program_id(2)
    m_ref[...] = jnp.full(m_ref.shape, -jnp.inf, F32)
    l_ref[...] = jnp.zeros(l_ref.shape, F32)
    acc_ref[...] = jnp.zeros(acc_ref.shape, F32)
    reps = tq // LANES

    def tile(j, diagonal):
        k0 = pl.multiple_of(j * tq, tq)
        kp = kpe_ref[pl.ds(k0, tq), :]
        hd = range(heads)
        cols = [hh * Q_HEAD_COLS for hh in hd]
        ks = [jnp.concatenate([kv_ref[pl.ds(k0, tq), c0:c0 + QK_NOPE], kp], axis=-1) for c0 in cols]
        ss = [lax.dot_general(q_ref[:, c0:c0 + Q_HEAD_COLS], ks[hh], NT, preferred_element_type=F32)
              for hh, c0 in enumerate(cols)]
        if diagonal:
            r = lax.broadcasted_iota(jnp.int32, (tq, tq), 0)
            c = lax.broadcasted_iota(jnp.int32, (tq, tq), 1)
            ss = [jnp.where(c <= r, s, -jnp.inf) for s in ss]
        m_old = [m_ref[hh] for hh in hd]
        m_new = [jnp.maximum(m_old[hh], jnp.max(ss[hh], axis=-1, keepdims=True)) for hh in hd]
        ps = [jnp.exp(ss[hh] - jnp.concatenate([m_new[hh]] * reps, axis=-1)) for hh in hd]
        corr = [jnp.exp(m_old[hh] - m_new[hh]) for hh in hd]
        pv = [lax.dot_general(ps[hh].astype(BF16), kv_ref[pl.ds(k0, tq), c0 + QK_NOPE:c0 + Q_HEAD_COLS], NN,
                              preferred_element_type=F32) for hh, c0 in enumerate(cols)]
        for hh in hd:
            l_ref[hh] = l_ref[hh] * corr[hh] + jnp.sum(ps[hh], axis=-1, keepdims=True)
            acc_ref[hh] = acc_ref[hh] * corr[hh] + pv[hh]
            m_ref[hh] = m_new[hh]

    def body(j, carry):
        tile(j, False)
        return carry

    lax.fori_loop(0, qi, body, 0)
    tile(qi, True)
    for hh in range(heads):
        o_ref[:, hh * V_HEAD:(hh + 1) * V_HEAD] = (acc_ref[hh] / l_ref[hh]).astype(o_ref.dtype)


def _flash(q, kv, kpe_bf, *, n_seq, seq_len, tq, heads):
    nq = seq_len // tq
    qw = heads * Q_HEAD_COLS
    est = 2 * (tq * qw * 2 + seq_len * qw * 2 + seq_len * LANES * 2 + tq * heads * V_HEAD * 2)
    est += 6 * heads * tq * tq * 4
    return pl.pallas_call(
        functools.partial(_flash_kernel, tq=tq, heads=heads),
        grid=(n_seq, MLA_HEADS // heads, nq),
        in_specs=[
            pl.BlockSpec((tq, qw), lambda b, h, i: (b * nq + i, h)),
            pl.BlockSpec((seq_len, qw), lambda b, h, i: (b, h)),
            pl.BlockSpec((seq_len, LANES), lambda b, h, i: (b, 0)),
        ],
        out_specs=pl.BlockSpec((tq, heads * V_HEAD), lambda b, h, i: (b * nq + i, h)),
        out_shape=jax.ShapeDtypeStruct((n_seq * seq_len, MLA_HEADS * V_HEAD), BF16),
        scratch_shapes=[pltpu.VMEM((heads, tq, LANES), F32), pltpu.VMEM((heads, tq, LANES), F32),
                        pltpu.VMEM((heads, tq, V_HEAD), F32)],
        compiler_params=_params(("parallel", "parallel", "arbitrary"), est),
        name="flash_prompt",
    )(q, kv, kpe_bf)


def _head_mm_kernel(a_ref, w_ref, o_ref):
    o_ref[...] = _dot(a_ref[...], w_ref[0]).astype(o_ref.dtype)


def _head_mm(a, w, *, a_width, a_col_stride, out_dtype, name):
    m = a.shape[0]
    heads, k, n = w.shape
    est = 2 * (m * a_width * 2 + k * n * 2 + m * n * 4) + m * n * 4
    return pl.pallas_call(
        _head_mm_kernel,
        grid=(heads,),
        in_specs=[
            pl.BlockSpec((m, a_width), lambda h: (0, h * a_col_stride)),
            pl.BlockSpec((1, k, n), lambda h: (h, 0, 0)),
        ],
        out_specs=pl.BlockSpec((m, n), lambda h: (0, h)),
        out_shape=jax.ShapeDtypeStruct((m, heads * n), out_dtype),
        compiler_params=_params(("parallel",), est),
        name=name,
    )(a, w)


def _paged_kernel(pt_ref, qlat_ref, qpe_ref, ckn_ref, kpn_ref, ck_hbm, kp_hbm, o_ref,
                  ck_buf, kp_buf, sems, m_ref, l_ref, acc_ref, *, pages, seqs, t_new, layer, n_pages, n_steps):
    step = pl.program_id(1)
    t = pl.program_id(0) * n_steps + step
    last_t = pl.num_programs(0) * n_steps - 1
    slot = lax.rem(t, PAGE_SLOTS)
    slot1 = lax.rem(t + 1, PAGE_SLOTS)
    slot2 = lax.rem(t + 2, PAGE_SLOTS)
    qs = list(range(seqs))

    def page_copies(tt, sl, q):
        base = ((tt // n_steps) * seqs + q) * n_pages + lax.rem(tt, n_steps) * pages
        out = []
        for k in range(pages):
            page = pt_ref[base + k]
            out.append(pltpu.make_async_copy(ck_hbm.at[layer, page], ck_buf.at[sl, q * pages + k], sems.at[0, sl]))
            out.append(pltpu.make_async_copy(kp_hbm.at[layer, page], kp_buf.at[sl, q * pages + k], sems.at[1, sl]))
        return out

    def start_all(copies):
        for n, c in enumerate(copies):
            c.start(priority=(n // 2) % 2)

    @pl.when(t == 0)
    def _():
        for q in qs:
            start_all(page_copies(t, slot, q))
            start_all(page_copies(jnp.minimum(t + 1, last_t), slot1, q))

    @pl.when(step == 0)
    def _():
        m_ref[...] = jnp.full(m_ref.shape, -jnp.inf, F32)
        l_ref[...] = jnp.zeros(l_ref.shape, F32)
        acc_ref[...] = jnp.zeros(acc_ref.shape, F32)

    for q in qs:
        for c in page_copies(t, slot, q):
            c.wait()

    def update(scores, keys):
        m_old = [m_ref[q] for q in qs]
        m_new = [jnp.maximum(m_old[q], jnp.max(scores[q], axis=-1, keepdims=True)) for q in qs]
        p = [jnp.exp(scores[q] - m_new[q]) for q in qs]
        corr = [jnp.exp(m_old[q] - m_new[q]) for q in qs]
        pv = [lax.dot_general(p[q].astype(BF16), keys[q], NN, preferred_element_type=F32) for q in qs]
        for q in qs:
            l_ref[q] = l_ref[q] * corr[q] + jnp.sum(p[q], axis=-1, keepdims=True)
            acc_ref[q] = acc_ref[q] * corr[q] + pv[q]
            m_ref[q] = m_new[q]

    t_next = jnp.minimum(t + 2, last_t)
    ql = [qlat_ref[q] for q in qs]
    qp = [qpe_ref[q][:, :QK_ROPE] for q in qs]
    ck, kpt = [], []
    for q in qs:
        ck.append(jnp.concatenate([ck_buf[slot, q * pages + k].astype(BF16) for k in range(pages)], axis=0))
        kpt.append(jnp.concatenate([kp_buf[slot, q * pages + k].astype(BF16) for k in range(pages)], axis=1))
        start_all(page_copies(t_next, slot2, q))
    s = [lax.dot_general(ql[q], ck[q], NT, preferred_element_type=F32)
         + lax.dot_general(qp[q], kpt[q], NN, preferred_element_type=F32) for q in qs]
    update(s, ck)

    @pl.when(step == n_steps - 1)
    def _():
        rows = ql[0].shape[0]
        pad = PAGE_SIZE - ckn_ref.shape[1]
        ckn = [jnp.concatenate([ckn_ref[q], jnp.zeros((pad, KV_LORA), F32)], axis=0).astype(BF16) for q in qs]
        t_row = lax.broadcasted_iota(jnp.int32, (rows, PAGE_SIZE), 0) // MLA_HEADS
        j_col = lax.broadcasted_iota(jnp.int32, (rows, PAGE_SIZE), 1)
        valid = j_col <= jnp.minimum(t_row, t_new - 1)
        sn = [jnp.where(valid, lax.dot_general(ql[q], ckn[q], NT, preferred_element_type=F32)
                        + lax.dot_general(qp[q], kpn_ref[q].astype(BF16), NN, preferred_element_type=F32),
                        -jnp.inf) for q in qs]
        update(sn, ckn)
        for q in qs:
            o_ref[q] = (acc_ref[q] / l_ref[q]).astype(o_ref.dtype)

    @pl.when(t == last_t)
    def _():
        for q in qs:
            for c in page_copies(t_next, slot1, q) + page_copies(t_next, slot2, q):
                c.wait()


def _paged(page_table, qlat, qpe, ckv_new8, kpe_new_t, cache_ckv, cache_kpe_t, *, layer, t_new):
    n_seq, rows, _ = qlat.shape
    n_pages = page_table.shape[1]
    pages, seqs = PAGES_PER_STEP, SEQS_PER_STEP
    steps = n_pages // pages
    pt = page_table.reshape(-1).astype(jnp.int32)
    t_pad = ckv_new8.shape[1]
    per_seq = lambda d1, d2: pl.BlockSpec((seqs, d1, d2), lambda b, s, pt_ref: (b, 0, 0))
    n_buf = seqs * pages
    est = (2 * n_buf * PAGE_SIZE * (KV_LORA + QK_ROPE) * 4 + 2 * n_buf * PAGE_SIZE * (KV_LORA + QK_ROPE) * 2
           + 8 * seqs * rows * (KV_LORA + pages * PAGE_SIZE) * 4)
    grid_spec = pltpu.PrefetchScalarGridSpec(
        num_scalar_prefetch=1,
        grid=(n_seq // seqs, steps),
        in_specs=[per_seq(rows, KV_LORA), per_seq(rows, LANES), per_seq(t_pad, KV_LORA),
                  per_seq(QK_ROPE, PAGE_SIZE), pl.BlockSpec(memory_space=pl.ANY),
                  pl.BlockSpec(memory_space=pl.ANY)],
        out_specs=per_seq(rows, KV_LORA),
        scratch_shapes=[pltpu.VMEM((PAGE_SLOTS, n_buf, PAGE_SIZE, KV_LORA), F32),
                        pltpu.VMEM((PAGE_SLOTS, n_buf, QK_ROPE, PAGE_SIZE), F32),
                        pltpu.SemaphoreType.DMA((2, PAGE_SLOTS)),
                        pltpu.VMEM((seqs, rows, 1), F32), pltpu.VMEM((seqs, rows, 1), F32),
                        pltpu.VMEM((seqs, rows, KV_LORA), F32)],
    )
    return pl.pallas_call(
        functools.partial(_paged_kernel, pages=pages, seqs=seqs, t_new=t_new, layer=layer, n_pages=n_pages,
                          n_steps=steps),
        grid_spec=grid_spec,
        out_shape=jax.ShapeDtypeStruct((n_seq, rows, KV_LORA), BF16),
        compiler_params=_vmem_params(("arbitrary", "arbitrary"), est),
        name="paged_attn",
    )(pt, qlat, qpe, ckv_new8, kpe_new_t, cache_ckv, cache_kpe_t)


def _pad_rows(x, rows_before, rows_total):
    pad = [(0, 0)] * x.ndim
    pad[1] = (rows_before, rows_total - rows_before - x.shape[1])
    return jnp.pad(x, pad)


def _conv_state(buf, x):
    keep = CONV_W - 1
    if x.shape[1] >= keep:
        return x[:, x.shape[1] - keep:]
    return jnp.concatenate([buf.astype(x.dtype), x], axis=1)[:, -keep:]


def _rope_tables(positions):
    half = QK_ROPE // 2
    inv = ROPE_BASE ** (-jnp.arange(half, dtype=F32) / half)
    ang = positions.astype(F32)[:, None] * inv[None, :]
    cos, sin = jnp.cos(ang), jnp.sin(ang)
    zeros = jnp.zeros((positions.shape[0], LANES - QK_ROPE), F32)
    return (jnp.concatenate([cos, cos, zeros], axis=-1), jnp.concatenate([-sin, sin, zeros], axis=-1))


def kernel(x_prompt, x_sample, p_prompt, p_sample, state_gdn, state_gdn_conv, state_lru, state_lru_conv, cache_ckv, cache_kpe, page_table, w_in_a, conv_qkv_w, gdn_a_log, gdn_dt_bias, gdn_norm_g, conv_lru_w, conv_lru_b, lru_wa, lru_ba, lru_wi, lru_bi, lru_lambda, w_out_a, w_in_c, q_norm_g, kv_norm_g, w_uq, w_ukv, w_o_c, norm_mix_g, norm_ffn_g, w_up, w_down, w_ple, w_ple_gate, norm_final_g):
    bp, seq, d = x_prompt.shape
    bs, ts, _ = x_sample.shape
    mp, ms = bp * seq, bs * ts
    m = mp + ms
    depth = norm_mix_g.shape[0]
    past = page_table.shape[1] * PAGE_SIZE
    tm = 512

    h = (x_prompt.reshape(mp, d), x_sample.reshape(ms, d))
    pos = jnp.concatenate([jnp.tile(jnp.arange(seq), bp), jnp.tile(past + jnp.arange(ts), bs)])
    cos, sin = _rope_tables(pos)
    hn = _rmsnorm(h, norm_mix_g[0], tm=tm, out_dtype=BF16)

    gdn_p, gdn_s, gconv_p, gconv_s, lru_p, lru_s, lconv_p, lconv_s = [], [], [], [], [], [], [], []
    ckv_p, ckv_s, kpe_p, kpe_s = [], [], [], []

    for i in range(depth):
        j = i // 2
        if i % 2 == 0:
            wa = w_in_a[j]
            cut_b = GDN_QKV + GDN_WIDTH
            cut_lx = cut_b + 2 * GDN_HEADS
            w_a = jnp.concatenate([
                wa[:, :cut_b], wa[:, cut_lx:], wa[:, cut_b:cut_lx],
                jnp.zeros((d, LANES - 2 * GDN_HEADS), F32)], axis=1).astype(BF16)
            ya = _mm(hn, w_a, single_buffer_w=True, tm=tm, tn=A_COLS, out_dtype=F32, name="in_proj_a")
            ya_p = ya.reshape(1, m, A_COLS)
            ya_s = ya[mp:].reshape(bs, ts, A_COLS)

            zero_buf = jnp.zeros((bp, SUBLANES, GDN_QKV), F32)
            parts_p = _gdn_intra(ya_p, zero_buf, conv_qkv_w[j], gdn_a_log[j], gdn_dt_bias[j], flat=True,
                                 n_groups=bp, nt=seq // GDN_CHUNK, bb=1, tl=GDN_CHUNK, l_real=GDN_CHUNK)
            o_gp, s_gp = _gdn_state(*parts_p, jnp.zeros((bp,) + state_gdn.shape[2:], F32), gdn_norm_g[j],
                                    nb=bp, nseg=1, tl=GDN_CHUNK)
            o_lp, h_lp = _lru_prompt(ya_p, jnp.zeros((bp, SUBLANES, LRU_WIDTH), F32),
                                     jnp.zeros((bp, 1, LRU_WIDTH), F32), conv_lru_w[j], conv_lru_b[j],
                                     lru_wa[j].astype(BF16), lru_ba[j].reshape(-1), lru_wi[j].astype(BF16),
                                     lru_bi[j].reshape(-1), lru_lambda[j], n_seq=bp, seq_rows=seq, tl=256)

            t_pad = SUBLANES
            ya_s8 = _pad_rows(ya_s, 0, t_pad)
            buf_g8 = _pad_rows(state_gdn_conv[j], SUBLANES - (CONV_W - 1), SUBLANES)
            seg = GDN_ROWS // t_pad
            parts_s = _gdn_intra(ya_s8, buf_g8, conv_qkv_w[j], gdn_a_log[j], gdn_dt_bias[j], flat=False,
                                 n_groups=bs // seg, nt=1, bb=seg, tl=t_pad, l_real=ts)
            o_gs, s_gs = _gdn_state(*parts_s, state_gdn[j], gdn_norm_g[j], nb=1, nseg=seg, tl=t_pad)
            o_gs = o_gs.reshape(bs, t_pad, GDN_WIDTH)
            lx_tm = ya_s[:, :, A_COL_LX:A_COL_LY].transpose(1, 0, 2)
            ly_tm = ya_s[:, :, A_COL_LY:A_COL_BA].transpose(1, 0, 2)
            o_ls_tm, h_ls = _lru_sample(lx_tm, ly_tm, state_lru_conv[j].transpose(1, 0, 2), state_lru[j],
                                        conv_lru_w[j], conv_lru_b[j], lru_wa[j].astype(BF16),
                                        lru_ba[j].reshape(-1), lru_wi[j].astype(BF16), lru_bi[j].reshape(-1),
                                        lru_lambda[j])
            mix = [(o_gp.reshape(mp, GDN_WIDTH), o_gs[:, :ts].reshape(ms, GDN_WIDTH)),
                   (o_lp.reshape(mp, LRU_WIDTH), o_ls_tm.transpose(1, 0, 2).reshape(ms, LRU_WIDTH))]
            h, hn = _mm(mix, w_out_a, layer=j, single_buffer_w=True, tm=tm // 2, tn=d, out_dtype=F32,
                        epilogue="residual_norm", residual=h, norm_g=norm_ffn_g[i], name="out_proj_a")

            keep = CONV_W - 1
            ya_pp = jnp.stack([ya[(b + 1) * seq - keep:(b + 1) * seq] for b in range(bp)])
            gdn_p.append(s_gp)
            gdn_s.append(s_gs)
            gconv_p.append(ya_pp[:, :, :GDN_QKV])
            gconv_s.append(_conv_state(state_gdn_conv[j], ya_s[:, :, :GDN_QKV]))
            lru_p.append(h_lp.reshape(bp, LRU_WIDTH))
            lru_s.append(h_ls)
            lconv_p.append(ya_pp[:, :, A_COL_LX:A_COL_LY])
            lconv_s.append(_conv_state(state_lru_conv[j], ya_s[:, :, A_COL_LX:A_COL_LY]))
        else:
            w_c = jnp.concatenate([w_in_c[j], jnp.zeros((d, LANES - QK_ROPE), F32)], axis=1).astype(BF16)
            yc = _mm(hn, w_c, tm=tm, tn=C_COLS, out_dtype=F32, name="in_proj_c")
            cqn, ckv, ckv_bf, kpe, kpe_bf = _mla_prep(yc, cos, sin, q_norm_g[j], kv_norm_g[j], tm=tm)
            w_q = w_uq[j].reshape(Q_LORA, MLA_HEADS, QK_NOPE + QK_ROPE)
            w_q = jnp.pad(w_q, ((0, 0), (0, 0), (0, Q_HEAD_COLS - QK_NOPE - QK_ROPE)))
            q = _mm_qrope(cqn, w_q.reshape(Q_LORA, MLA_HEADS * Q_HEAD_COLS).astype(BF16), cos, sin,
                          tm=tm, tn=MLA_HEADS * Q_HEAD_COLS)
            w_r = w_ukv[j].reshape(KV_LORA, MLA_HEADS, QK_NOPE + V_HEAD)

            kv = _mm(ckv_bf[:mp], w_ukv, layer=j, tm=tm, tn=w_ukv.shape[-1], out_dtype=BF16, name="kv_up")
            o_p = _flash(q, kv, kpe_bf, n_seq=bp, seq_len=seq, tq=512, heads=2)

            q_s = q[mp:]
            w_uk_t = w_r[:, :, :QK_NOPE].transpose(1, 2, 0).astype(BF16)
            w_uv = w_r[:, :, QK_NOPE:].transpose(1, 0, 2).astype(BF16)
            qlat = _head_mm(q_s, w_uk_t, a_width=QK_NOPE, a_col_stride=Q_HEAD_COLS // QK_NOPE,
                            out_dtype=BF16, name="q_latent")
            qlat = qlat.reshape(bs, ts * MLA_HEADS, KV_LORA)
            qpe = q_s.reshape(ms, MLA_HEADS, Q_HEAD_COLS)[:, :, QK_NOPE:].reshape(bs, ts * MLA_HEADS, LANES)
            ckv_new8 = _pad_rows(ckv[mp:].reshape(bs, ts, KV_LORA), 0, SUBLANES)
            kpe_new_t = kpe[mp:, :QK_ROPE].reshape(bs, ts, QK_ROPE).transpose(0, 2, 1)
            kpe_new_t = jnp.pad(kpe_new_t, ((0, 0), (0, 0), (0, PAGE_SIZE - ts)))
            lat = _paged(page_table, qlat, qpe, ckv_new8, kpe_new_t, cache_ckv,
                         cache_kpe.transpose(0, 1, 3, 2), layer=j, t_new=ts)
            o_s = _head_mm(lat.reshape(ms, MLA_HEADS * KV_LORA), w_uv, a_width=KV_LORA, a_col_stride=1,
                           out_dtype=BF16, name="v_up")
            h, hn = _mm([(o_p, o_s)], w_o_c, layer=j, single_buffer_w=True, tm=tm, tn=d,
                        out_dtype=F32, epilogue="residual_norm", residual=h, norm_g=norm_ffn_g[i],
                        name="out_proj_c")
            ckv_p.append(ckv[:mp].reshape(bp, seq, KV_LORA))
            ckv_s.append(ckv[mp:].reshape(bs, ts, KV_LORA))
            kpe_p.append(kpe[:mp, :QK_ROPE].reshape(bp, seq, QK_ROPE))
            kpe_s.append(kpe[mp:, :QK_ROPE].reshape(bs, ts, QK_ROPE))

        ff = _mm(hn, w_up, layer=i, tm=tm, tn=2048, out_dtype=BF16, epilogue="relu2", name="ffn_up")
        h = _mm(ff, w_down, layer=i, single_buffer_w=True, tm=tm, tn=512, out_dtype=F32, epilogue="residual",
                residual=h, name="ffn_down")
        p = jnp.concatenate([p_prompt[i].reshape(mp, -1), p_sample[i].reshape(ms, -1)], axis=0)
        if i + 1 < depth:
            h, hn = _ple(h, p, w_ple, w_ple_gate, norm_mix_g[i + 1], layer=i, tm=256, norm_dtype=BF16)
        else:
            y_p, y_s = _ple(h, p, w_ple, w_ple_gate, norm_final_g, layer=i, tm=256, norm_dtype=F32, split_rows=mp)

    return (y_p.reshape(bp, seq, d), y_s.reshape(bs, ts, d),
            jnp.stack(gdn_p), jnp.stack(gdn_s), jnp.stack(gconv_p), jnp.stack(gconv_s),
            jnp.stack(lru_p), jnp.stack(lru_s), jnp.stack(lconv_p), jnp.stack(lconv_s),
            jnp.stack(ckv_p), jnp.stack(ckv_s), jnp.stack(kpe_p), jnp.stack(kpe_s))
```
